```python
import jax, jax.numpy as jnp
from jax import lax
import numpy as np

D_MODEL = 1024
BATCH = 8
SEQ = 2048
DEPTH = 4
DEC_BATCH = 128
DEC_SEQ = 4
PAST_LEN = 16384
PAGE_SIZE = 128

N_MIXERS = 3
N_A = (DEPTH + 2) // 3
N_B = (DEPTH + 1) // 3
N_C = DEPTH // 3
N_DENSE = (DEPTH + 1) // 2
N_MOE = DEPTH // 2

CHUNK = 128
D_A = D_MODEL
A_GROUPS = 8
A_GROUP_DIM = D_A // A_GROUPS
D_B = D_MODEL
CONV_W = 3
D_C = D_MODEL
POOL_WINDOWS = (2, 4, 8, 16)
C_GROUPS = len(POOL_WINDOWS)
C_GROUP_DIM = D_C // C_GROUPS
POOL_CTX = max(POOL_WINDOWS) - 1
N_MEM = 256
X_HEADS = 4
X_HEAD_DIM = D_MODEL // X_HEADS
D_FF = ((8 * D_MODEL // 3) + 255) // 256 * 256
N_EXPERTS = 8
TOP_K = 2
D_FF_EXPERT = 7 * D_MODEL // 2
EPS = 1e-6

kernel_name = 'hybrid_gmlp_conv_pool_memxattn_moe_step'


def rmsnorm(x, g):
    xf = x.astype(jnp.float32)
    r = lax.rsqrt(jnp.mean(xf * xf, axis=-1, keepdims=True) + EPS)
    return (xf * r).astype(x.dtype) * g


def layernorm(x, g, b):
    xf = x.astype(jnp.float32)
    mu = jnp.mean(xf, axis=-1, keepdims=True)
    var = jnp.mean(jnp.square(xf - mu), axis=-1, keepdims=True)
    return ((xf - mu) * lax.rsqrt(var + EPS)).astype(x.dtype) * g + b


def chunk_gmlp_mixer(xn, w_in, ln_g, ln_b, w_s, b_s, w_out):
    bsz, L, _ = xn.shape
    z = jax.nn.gelu(xn @ w_in)
    u, v = jnp.split(z, 2, axis=-1)
    v = layernorm(v, ln_g, ln_b)
    n_chunks = -(-L // CHUNK)
    pad = n_chunks * CHUNK - L
    vc = jnp.pad(v, ((0, 0), (0, pad), (0, 0))).reshape(bsz, n_chunks, CHUNK, A_GROUPS, A_GROUP_DIM)
    mask = jnp.tril(jnp.ones((CHUNK, CHUNK), dtype=bool))
    ws = jnp.where(mask[None], w_s, jnp.zeros_like(w_s))
    s = jnp.einsum('gts,bcsgd->bctgd', ws, vc) + b_s.T[None, None, :, :, None]
    s = s.reshape(bsz, n_chunks * CHUNK, D_A)[:, :L]
    return (u * s) @ w_out, v


def short_conv_mixer(xn, prefix, w_in, conv_w, w_out):
    L = xn.shape[1]
    gate_b, gate_c, h = jnp.split(xn @ w_in, 3, axis=-1)
    zf = jnp.concatenate([prefix, gate_c * h], axis=1)
    conv = sum(conv_w[k] * zf[:, k:k + L] for k in range(CONV_W))
    return (gate_b * conv) @ w_out, zf[:, -(CONV_W - 1):]


def pool_mixer(xn, prefix, pos0, w_in, w_grp, scale, w_out):
    bsz, L, _ = xn.shape
    h = xn @ w_in
    hf = jnp.concatenate([prefix, h], axis=1)
    cs = jnp.pad(jnp.cumsum(hf.astype(jnp.float32), axis=1), ((0, 0), (1, 0), (0, 0)))
    P = POOL_CTX
    pos = pos0 + jnp.arange(L)
    means = []
    for g, w in enumerate(POOL_WINDOWS):
        sl = slice(g * C_GROUP_DIM, (g + 1) * C_GROUP_DIM)
        win_sum = cs[:, P + 1:P + 1 + L, sl] - cs[:, P + 1 - w:P + 1 - w + L, sl]
        cnt = jnp.minimum(w, pos + 1).astype(jnp.float32)[None, :, None]
        means.append(win_sum / cnt)
    d = (jnp.concatenate(means, axis=-1).astype(h.dtype) - h).reshape(bsz, L, C_GROUPS, C_GROUP_DIM)
    m = jnp.einsum('blgc,gce->blge', d, w_grp).reshape(bsz, L, D_C)
    return (m * scale) @ w_out, hf[:, -P:]


def mem_kv(mem, g, wk, wv):
    bsz = mem.shape[0]
    mn = rmsnorm(mem, g)
    k = (mn @ wk).reshape(bsz, N_MEM, X_HEADS, X_HEAD_DIM)
    v = (mn @ wv).reshape(bsz, N_MEM, X_HEADS, X_HEAD_DIM)
    return k, v


def cross_attn(xn, k, v, wq, wo):
    bsz, L, _ = xn.shape
    q = (xn @ wq).reshape(bsz, L, X_HEADS, X_HEAD_DIM)
    s = jnp.einsum('blhd,bmhd->bhlm', q, k).astype(jnp.float32) * (X_HEAD_DIM ** -0.5)
    p = jax.nn.softmax(s, axis=-1).astype(v.dtype)
    o = jnp.einsum('bhlm,bmhd->blhd', p, v).reshape(bsz, L, D_MODEL)
    return o @ wo


def swiglu(x, w1, w3, w2):
    return (jax.nn.silu(x @ w1) * (x @ w3)) @ w2


def moe_ffn(xn, router, w1, w3, w2):
    logits = (xn @ router).astype(jnp.float32)
    top_v, top_i = lax.top_k(logits, TOP_K)
    gates = jax.nn.softmax(top_v, axis=-1)
    comb = jnp.sum(jax.nn.one_hot(top_i, N_EXPERTS, dtype=jnp.float32) * gates[..., None], axis=-2)
    comb = comb.astype(xn.dtype)
    out = jnp.zeros_like(xn)
    for e in range(N_EXPERTS):
        out = out + comb[..., e:e + 1] * swiglu(xn, w1[e], w3[e], w2[e])
    return out


def run_trunk(x, mem_k, mem_v, conv_prefix, pool_prefix, pos0, p):
    conv_new, pool_new, v_new = [], [], []
    for i in range(DEPTH):
        kind, j = i % N_MIXERS, i // N_MIXERS
        xn = rmsnorm(x, p['norm_mix'][i])
        if kind == 0:
            y, vrows = chunk_gmlp_mixer(xn, p['a_w_in'][j], p['a_ln_g'][j], p['a_ln_b'][j],
                                        p['a_w_s'][j], p['a_b_s'][j], p['a_w_out'][j])
            v_new.append(vrows)
        elif kind == 1:
            y, st = short_conv_mixer(xn, conv_prefix[j], p['b_w_in'][j], p['b_conv'][j], p['b_w_out'][j])
            conv_new.append(st)
        else:
            y, st = pool_mixer(xn, pool_prefix[j], pos0, p['c_w_in'][j], p['c_w_grp'][j],
                               p['c_scale'][j], p['c_w_out'][j])
            pool_new.append(st)
        x = x + y
        x = x + cross_attn(rmsnorm(x, p['norm_xa'][i]), mem_k[i], mem_v[i], p['xa_wq'][i], p['xa_wo'][i])
        xn = rmsnorm(x, p['norm_ffn'][i])
        if i % 2 == 0:
            x = x + swiglu(xn, p['f_w1'][i // 2], p['f_w3'][i // 2], p['f_w2'][i // 2])
        else:
            x = x + moe_ffn(xn, p['m_router'][i // 2], p['m_w1'][i // 2], p['m_w3'][i // 2], p['m_w2'][i // 2])
    return rmsnorm(x, p['norm_final']), jnp.stack(conv_new), jnp.stack(pool_new), jnp.stack(v_new)


def setup_inputs(seed: int = 0) -> dict:
    key = jax.random.key(seed)
    ks = iter(jax.random.split(key, 64))

    def nrm(shape, scale=1.0):
        return jax.random.normal(next(ks), shape, jnp.float32) * scale

    def gain(shape):
        return 1.0 + 0.01 * nrm(shape)

    D = D_MODEL
    return {
        'x_prompt': nrm((BATCH, SEQ, D)),
        'x_sample': nrm((DEC_BATCH, DEC_SEQ, D)),
        'cache_mem_k': nrm((DEPTH, DEC_BATCH, N_MEM, X_HEADS, X_HEAD_DIM)),
        'cache_mem_v': nrm((DEPTH, DEC_BATCH, N_MEM, X_HEADS, X_HEAD_DIM)),
        'state_conv': nrm((N_B, DEC_BATCH, CONV_W - 1, D_B)),
        'state_pool': nrm((N_C, DEC_BATCH, POOL_CTX, D_C)),
        'mem_prompt': nrm((BATCH, N_MEM, D)),
        'norm_mix': gain((DEPTH, D)),
        'norm_xa': gain((DEPTH, D)),
        'norm_ffn': gain((DEPTH, D)),
        'norm_mem': gain((DEPTH, D)),
        'norm_final': gain((D,)),
        'a_w_in': nrm((N_A, D, 2 * D_A), D ** -0.5),
        'a_ln_g': gain((N_A, D_A)),
        'a_ln_b': nrm((N_A, D_A), 0.01),
        'a_w_s': nrm((N_A, A_GROUPS, CHUNK, CHUNK), CHUNK ** -0.5),
        'a_b_s': gain((N_A, A_GROUPS, CHUNK)),
        'a_w_out': nrm((N_A, D_A, D), D_A ** -0.5),
        'b_w_in': nrm((N_B, D, 3 * D_B), D ** -0.5),
        'b_conv': nrm((N_B, CONV_W, D_B), CONV_W ** -0.5),
        'b_w_out': nrm((N_B, D_B, D), D_B ** -0.5),
        'c_w_in': nrm((N_C, D, D_C), D ** -0.5),
        'c_w_grp': nrm((N_C, C_GROUPS, C_GROUP_DIM, C_GROUP_DIM), C_GROUP_DIM ** -0.5),
        'c_scale': 1.0 + 0.1 * nrm((N_C, D_C)),
        'c_w_out': nrm((N_C, D_C, D), D_C ** -0.5),
        'xa_wq': nrm((DEPTH, D, D), D ** -0.5),
        'xa_wk': nrm((DEPTH, D, D), D ** -0.5),
        'xa_wv': nrm((DEPTH, D, D), D ** -0.5),
        'xa_wo': nrm((DEPTH, D, D), D ** -0.5),
        'f_w1': nrm((N_DENSE, D, D_FF), D ** -0.5),
        'f_w3': nrm((N_DENSE, D, D_FF), D ** -0.5),
        'f_w2': nrm((N_DENSE, D_FF, D), D_FF ** -0.5),
        'm_router': nrm((N_MOE, D, N_EXPERTS), D ** -0.5),
        'm_w1': nrm((N_MOE, N_EXPERTS, D, D_FF_EXPERT), D ** -0.5),
        'm_w3': nrm((N_MOE, N_EXPERTS, D, D_FF_EXPERT), D ** -0.5),
        'm_w2': nrm((N_MOE, N_EXPERTS, D_FF_EXPERT, D), D_FF_EXPERT ** -0.5),
    }


def reference(x_prompt, x_sample, cache_mem_k, cache_mem_v, state_conv, state_pool, mem_prompt,
              norm_mix, norm_xa, norm_ffn, norm_mem, norm_final,
              a_w_in, a_ln_g, a_ln_b, a_w_s, a_b_s, a_w_out,
              b_w_in, b_conv, b_w_out,
              c_w_in, c_w_grp, c_scale, c_w_out,
              xa_wq, xa_wk, xa_wv, xa_wo,
              f_w1, f_w3, f_w2,
              m_router, m_w1, m_w3, m_w2):
    p = dict(norm_mix=norm_mix, norm_xa=norm_xa, norm_ffn=norm_ffn, norm_final=norm_final,
             a_w_in=a_w_in, a_ln_g=a_ln_g, a_ln_b=a_ln_b, a_w_s=a_w_s, a_b_s=a_b_s, a_w_out=a_w_out,
             b_w_in=b_w_in, b_conv=b_conv, b_w_out=b_w_out,
             c_w_in=c_w_in, c_w_grp=c_w_grp, c_scale=c_scale, c_w_out=c_w_out,
             xa_wq=xa_wq, xa_wo=xa_wo,
             f_w1=f_w1, f_w3=f_w3, f_w2=f_w2,
             m_router=m_router, m_w1=m_w1, m_w3=m_w3, m_w2=m_w2)

    kv = [mem_kv(mem_prompt, norm_mem[i], xa_wk[i], xa_wv[i]) for i in range(DEPTH)]
    mem_k_prompt = jnp.stack([k for k, _ in kv])
    mem_v_prompt = jnp.stack([v for _, v in kv])
    bp = x_prompt.shape[0]
    conv_zero = jnp.zeros((N_B, bp, CONV_W - 1, D_B), x_prompt.dtype)
    pool_zero = jnp.zeros((N_C, bp, POOL_CTX, D_C), x_prompt.dtype)
    y_prompt, conv_prompt, pool_prompt, _ = run_trunk(x_prompt, mem_k_prompt, mem_v_prompt,
                                                      conv_zero, pool_zero, 0, p)

    y_sample, conv_sample, pool_sample, chunk_v_sample = run_trunk(x_sample, cache_mem_k, cache_mem_v,
                                                                   state_conv, state_pool, PAST_LEN, p)
    return (y_prompt, y_sample, mem_k_prompt, mem_v_prompt, conv_prompt, pool_prompt,
            conv_sample, pool_sample, chunk_v_sample)
```

```python
import functools

import jax
import jax.numpy as jnp
from jax import lax
from jax.experimental import pallas as pl
from jax.experimental.pallas import tpu as pltpu

F32 = jnp.float32
BF16 = jnp.bfloat16

EPS = 1e-6
TM = 512
CHUNK = 128
A_GROUPS = 8
POOL_WINDOWS = (2, 4, 8, 16)
POOL_CTX = max(POOL_WINDOWS) - 1
X_HEADS = 4
TOP_K = 2
PAST_LEN = 16384
LANES = 128
MOE_TM = 1536
MOE_TF = 512
VMEM_LIMIT = 60 * 1024 * 1024


def _params(*sem):
    return pltpu.CompilerParams(dimension_semantics=sem, vmem_limit_bytes=VMEM_LIMIT)


def _const(shape):
    nd = len(shape)
    return pl.BlockSpec(shape, lambda *_: (0,) * nd, pipeline_mode=pl.Buffered(1))


def _rmsnorm(x, g):
    r = lax.rsqrt(jnp.mean(x * x, axis=-1, keepdims=True) + EPS)
    return (x * r) * g


def _bdot(a, w):
    return jnp.dot(a.astype(BF16), w, preferred_element_type=F32)


def _mem_kv_kernel(m_ref, g_ref, wk_ref, wv_ref, k_ref, v_ref):
    mn = _rmsnorm(m_ref[...], g_ref[...]).astype(BF16)
    k_ref[...] = jnp.dot(mn, wk_ref[...], preferred_element_type=F32)
    v_ref[...] = jnp.dot(mn, wv_ref[...], preferred_element_type=F32)


def _mem_kv(mem, g, wk, wv):
    rows, d = mem.shape
    depth = g.shape[0]
    out = jax.ShapeDtypeStruct((depth, rows, d), F32)
    wspec = pl.BlockSpec((None, d, d), lambda l, r: (l, 0, 0))
    ospec = pl.BlockSpec((None, TM, d), lambda l, r: (l, r, 0))
    return pl.pallas_call(
        _mem_kv_kernel,
        grid=(depth, rows // TM),
        in_specs=[pl.BlockSpec((TM, d), lambda l, r: (r, 0)),
                  pl.BlockSpec((None, 1, d), lambda l, r: (l, 0, 0)),
                  wspec, wspec],
        out_specs=[ospec, ospec],
        out_shape=[out, out],
        compiler_params=_params("arbitrary", "arbitrary"),
        name="mem_kv",
    )(mem, g, wk, wv)


def _mixer_a_kernel(x_ref, g_ref, win_ref, lng_ref, lnb_ref, ws_ref, bs_ref, sc_ref, sb_ref, wout_ref,
                    o_ref, vs_ref, u_scr, v_scr, s_scr, *, npt, bs, ls):
    i = pl.program_id(0)
    d = x_ref.shape[1]
    x = x_ref[...]
    z = jax.nn.gelu(_bdot(_rmsnorm(x, g_ref[...]), win_ref[...]))
    u_scr[...] = z[:, :d]
    v = z[:, d:]
    mu = jnp.mean(v, axis=-1, keepdims=True)
    var = jnp.mean(jnp.square(v - mu), axis=-1, keepdims=True)
    v_scr[...] = ((v - mu) * lax.rsqrt(var + EPS)) * lng_ref[...] + lnb_ref[...]
    gd = d // A_GROUPS

    @pl.when(i < npt)
    def _():
        rows = lax.broadcasted_iota(jnp.int32, (CHUNK, CHUNK), 0)
        cols = lax.broadcasted_iota(jnp.int32, (CHUNK, CHUNK), 1)
        for g in range(A_GROUPS):
            wg = jnp.where(cols <= rows, ws_ref[g], 0.0).astype(BF16)
            bg = bs_ref[g]
            for c in range(TM // CHUNK):
                vc = v_scr[c * CHUNK:(c + 1) * CHUNK, g * gd:(g + 1) * gd]
                s_scr[c * CHUNK:(c + 1) * CHUNK, g * gd:(g + 1) * gd] = (
                    jnp.dot(wg, vc.astype(BF16), preferred_element_type=F32) + bg)

    @pl.when(i == npt)
    def _():
        for t in range(ls):
            acc = jnp.broadcast_to(sb_ref[t:t + 1, :], (bs, d))
            for s in range(t + 1):
                acc = acc + sc_ref[t * ls + s:t * ls + s + 1, :] * v_scr[s * bs:(s + 1) * bs, :]
            s_scr[t * bs:(t + 1) * bs, :] = acc
        vs_ref[...] = v_scr[...]

    o_ref[...] = x + _bdot(u_scr[...] * s_scr[...], wout_ref[...])


def _mixer_a(x, g, w_in, ln_g, ln_b, w_s, b_s, w_out, *, npt, bs, ls):
    t, d = x.shape
    assert TM == bs * ls and TM % CHUNK == 0 and ls <= CHUNK
    gd = d // A_GROUPS
    sc = jnp.repeat(w_s[:, :ls, :ls].transpose(1, 2, 0).reshape(ls * ls, A_GROUPS), gd, axis=1)
    sb = jnp.repeat(b_s[:, :ls].T, gd, axis=1)
    tile = pl.BlockSpec((TM, d), lambda i: (i, 0))
    return pl.pallas_call(
        functools.partial(_mixer_a_kernel, npt=npt, bs=bs, ls=ls),
        grid=(npt + 1,),
        in_specs=[tile, _const((1, d)), _const(w_in.shape), _const((1, d)), _const((1, d)),
                  _const(w_s.shape), _const((A_GROUPS, CHUNK, 1)), _const(sc.shape), _const(sb.shape),
                  _const(w_out.shape)],
        out_specs=[tile, pl.BlockSpec((TM, d), lambda i: (0, 0))],
        out_shape=[jax.ShapeDtypeStruct((t, d), F32), jax.ShapeDtypeStruct((TM, d), F32)],
        scratch_shapes=[pltpu.VMEM((TM, d), F32)] * 3,
        compiler_params=_params("arbitrary"),
        name="mixer_a",
    )(x, g, w_in, ln_g, ln_b, w_s, b_s[:, :, None], sc, sb, w_out)


def _mixer_b_kernel(x_ref, g_ref, win_ref, cw_ref, pre_ref, wout_ref,
                    o_ref, tail_ref, zs_ref, conv_scr, carry_scr, *, npt, tps, bs):
    i = pl.program_id(0)
    d = x_ref.shape[1]
    x = x_ref[...]
    p = _bdot(_rmsnorm(x, g_ref[...]), win_ref[...])
    gate_b = p[:, :d]
    zc = p[:, d:2 * d] * p[:, 2 * d:]
    w0, w1, w2 = cw_ref[0:1, :], cw_ref[1:2, :], cw_ref[2:3, :]

    @pl.when(i == 0)
    def _():
        carry_scr[...] = jnp.zeros_like(carry_scr)

    @pl.when(i < npt)
    def _():
        keep = (i % tps) != 0
        c6 = jnp.where(keep, carry_scr[6:7, :], 0.0)
        c7 = jnp.where(keep, carry_scr[7:8, :], 0.0)
        rows = lax.broadcasted_iota(jnp.int32, (TM, 1), 0)
        sh1 = jnp.where(rows == 0, c7, pltpu.roll(zc, 1, 0))
        sh2 = jnp.where(rows == 0, c6, jnp.where(rows == 1, c7, pltpu.roll(zc, 2, 0)))
        conv_scr[...] = w0 * sh2 + w1 * sh1 + w2 * zc
        carry_scr[...] = zc[TM - 8:, :]
        tail_ref[...] = zc[TM - 8:, :]

    @pl.when(i == npt)
    def _():
        p0, p1 = pre_ref[0:bs, :], pre_ref[bs:2 * bs, :]
        sh1 = jnp.concatenate([p1, zc[:TM - bs]], axis=0)
        sh2 = jnp.concatenate([p0, p1, zc[:TM - 2 * bs]], axis=0)
        conv_scr[...] = w0 * sh2 + w1 * sh1 + w2 * zc
        zs_ref[...] = zc[TM - 2 * bs:, :]

    o_ref[...] = x + _bdot(gate_b * conv_scr[...], wout_ref[...])


def _mixer_b(x, g, w_in, conv_w, prefix, w_out, *, npt, tps, bs):
    t, d = x.shape
    nb = npt // tps
    tile = pl.BlockSpec((TM, d), lambda i: (i, 0))
    return pl.pallas_call(
        functools.partial(_mixer_b_kernel, npt=npt, tps=tps, bs=bs),
        grid=(npt + 1,),
        in_specs=[tile, _const((1, d)), _const(w_in.shape), _const(conv_w.shape), _const(prefix.shape),
                  _const(w_out.shape)],
        out_specs=[tile,
                   pl.BlockSpec((None, 8, d), lambda i: (jnp.minimum(i // tps, nb - 1), 0, 0)),
                   pl.BlockSpec((2 * bs, d), lambda i: (0, 0))],
        out_shape=[jax.ShapeDtypeStruct((t, d), F32), jax.ShapeDtypeStruct((nb, 8, d), F32),
                   jax.ShapeDtypeStruct((2 * bs, d), F32)],
        scratch_shapes=[pltpu.VMEM((TM, d), F32), pltpu.VMEM((8, d), F32)],
        compiler_params=_params("arbitrary"),
        name="mixer_b",
    )(x, g, w_in, conv_w, prefix, w_out)


def _mixer_c_kernel(x_ref, g_ref, win_ref, pre_ref, wgrp_ref, scale_ref, wout_ref,
                    o_ref, tail_ref, hs_ref, d_scr, hist_scr, *, npt, tps, bs, ls, pos0):
    i = pl.program_id(0)
    d = x_ref.shape[1]
    gd = d // len(POOL_WINDOWS)
    hp = POOL_CTX + 1
    x = x_ref[...]
    h = _bdot(_rmsnorm(x, g_ref[...]), win_ref[...])

    @pl.when(i == 0)
    def _():
        hist_scr[...] = jnp.zeros_like(hist_scr)

    @pl.when(i < npt)
    def _():
        keep = (i % tps) != 0
        ext = jnp.concatenate([jnp.where(keep, hist_scr[...], 0.0), h], axis=0)
        pos = (i % tps) * TM + lax.broadcasted_iota(jnp.int32, (TM, 1), 0)
        sums = ext
        width = 1
        for g, w in enumerate(POOL_WINDOWS):
            while width < w:
                sums = sums + pltpu.roll(sums, width, 0)
                width *= 2
            cnt = jnp.minimum(w, pos + 1).astype(F32)
            sl = slice(g * gd, (g + 1) * gd)
            d_scr[:, sl] = sums[hp:, sl] / cnt - h[:, sl]
        hist_scr[...] = h[TM - hp:, :]
        tail_ref[...] = h[TM - hp:, :]

    @pl.when(i == npt)
    def _():
        for g, w in enumerate(POOL_WINDOWS):
            sl = slice(g * gd, (g + 1) * gd)
            run = jnp.zeros((bs, gd), F32)
            tails = [run]
            for m in range(1, w):
                run = run + pre_ref[(POOL_CTX - m) * bs:(POOL_CTX - m + 1) * bs, sl]
                tails.append(run)
            for t in range(ls):
                n = min(w, t + 1)
                acc = tails[w - n]
                for j in range(n):
                    acc = acc + h[(t - j) * bs:(t - j + 1) * bs, sl]
                cnt = float(min(w, pos0 + t + 1))
                d_scr[t * bs:(t + 1) * bs, sl] = acc / cnt - h[t * bs:(t + 1) * bs, sl]
        hs_ref[...] = h

    m = jnp.concatenate(
        [_bdot(d_scr[:, g * gd:(g + 1) * gd], wgrp_ref[g]) for g in range(len(POOL_WINDOWS))], axis=1)
    o_ref[...] = x + _bdot(m * scale_ref[...], wout_ref[...])


def _mixer_c(x, g, w_in, prefix, w_grp, scale, w_out, *, npt, tps, bs, ls, pos0):
    t, d = x.shape
    nb = npt // tps
    hp = POOL_CTX + 1
    tile = pl.BlockSpec((TM, d), lambda i: (i, 0))
    return pl.pallas_call(
        functools.partial(_mixer_c_kernel, npt=npt, tps=tps, bs=bs, ls=ls, pos0=pos0),
        grid=(npt + 1,),
        in_specs=[tile, _const((1, d)), _const(w_in.shape), _const(prefix.shape), _const(w_grp.shape),
                  _const((1, d)), _const(w_out.shape)],
        out_specs=[tile,
                   pl.BlockSpec((None, hp, d), lambda i: (jnp.minimum(i // tps, nb - 1), 0, 0)),
                   pl.BlockSpec((TM, d), lambda i: (0, 0))],
        out_shape=[jax.ShapeDtypeStruct((t, d), F32), jax.ShapeDtypeStruct((nb, hp, d), F32),
                   jax.ShapeDtypeStruct((TM, d), F32)],
        scratch_shapes=[pltpu.VMEM((TM, d), F32), pltpu.VMEM((hp, d), F32)],
        compiler_params=_params("arbitrary"),
        name="mixer_c",
    )(x, g, w_in, prefix, w_grp, scale, w_out)


def _softmax_rows(s):
    e = jnp.exp(s - jnp.max(s, axis=-1, keepdims=True))
    return e / jnp.sum(e, axis=-1, keepdims=True)


def _xattn_kernel(x_ref, xs_ref, g_ref, wq_ref, wo_ref, kp_ref, vp_ref, ks_ref, vs_ref,
                  o_ref, q_scr, o_scr, *, npt, bs, ls, spb):
    i = pl.program_id(0)
    d = x_ref.shape[1]
    hd = d // X_HEADS
    qscale = float(hd) ** -0.5
    nt = (((1,), (1,)), ((), ()))

    @pl.when(i == 0)
    def _():
        q_scr[...] = _bdot(_rmsnorm(xs_ref[...], g_ref[...]), wq_ref[...]) * qscale

    @pl.when(i < npt)
    def _():
        x = x_ref[...]
        q = (_bdot(_rmsnorm(x, g_ref[...]), wq_ref[...]) * qscale).astype(BF16)
        k = kp_ref[...].astype(BF16)
        v = vp_ref[...].astype(BF16)
        heads = []
        for h in range(X_HEADS):
            sl = slice(h * hd, (h + 1) * hd)
            s = lax.dot_general(q[:, sl], k[:, sl], nt, preferred_element_type=F32)
            heads.append(_bdot(_softmax_rows(s), v[:, sl]))
        o_ref[...] = x + _bdot(jnp.concatenate(heads, axis=1), wo_ref[...])

        rows = lax.broadcasted_iota(jnp.int32, (X_HEADS * ls, d), 0)
        lanes = lax.broadcasted_iota(jnp.int32, (X_HEADS * ls, d), 1)
        own = (rows // ls) == (lanes // hd)
        for bb in range(spb):
            b = i * spb + bb
            qb = jnp.concatenate([q_scr[pl.ds(t * bs + b, 1), :] for t in range(ls)], axis=0)
            qblk = jnp.where(own, jnp.concatenate([qb] * X_HEADS, axis=0), 0.0)
            s = lax.dot_general(qblk.astype(BF16), ks_ref[bb].astype(BF16), nt,
                                preferred_element_type=F32)
            of = jnp.where(own, _bdot(_softmax_rows(s), vs_ref[bb].astype(BF16)), 0.0)
            ob = of[0:ls]
            for h in range(1, X_HEADS):
                ob = ob + of[h * ls:(h + 1) * ls]
            for t in range(ls):
                o_scr[pl.ds(t * bs + b, 1), :] = ob[t:t + 1]

    @pl.when(i == npt)
    def _():
        o_ref[...] = x_ref[...] + _bdot(o_scr[...], wo_ref[...])


def _xattn(x, g, wq, wo, kp, vp, ks, vs, layer, *, npt, tps, bs, ls):
    t, d = x.shape
    nb, nm = kp.shape[1], kp.shape[2]
    assert bs % npt == 0
    spb = bs // npt
    tile = pl.BlockSpec((TM, d), lambda i: (i, 0))
    pmem = pl.BlockSpec((None, None, nm, d), lambda i: (layer, jnp.minimum(i // tps, nb - 1), 0, 0))
    smem = pl.BlockSpec((None, spb, nm, d), lambda i: (layer, jnp.minimum(i, npt - 1), 0, 0))
    return pl.pallas_call(
        functools.partial(_xattn_kernel, npt=npt, bs=bs, ls=ls, spb=spb),
        grid=(npt + 1,),
        in_specs=[tile, pl.BlockSpec((TM, d), lambda i: (npt, 0), pipeline_mode=pl.Buffered(1)),
                  _const((1, d)), _const(wq.shape), _const(wo.shape), pmem, pmem, smem, smem],
        out_specs=tile,
        out_shape=jax.ShapeDtypeStruct((t, d), F32),
        scratch_shapes=[pltpu.VMEM((TM, d), F32), pltpu.VMEM((TM, d), F32)],
        compiler_params=_params("arbitrary"),
        name="xattn",
    )(x, x, g, wq, wo, kp, vp, ks, vs)


def _ffn_kernel(x_ref, g_ref, w1_ref, w3_ref, w2_ref, o_ref, *, chunks):
    x = x_ref[...]
    xn = _rmsnorm(x, g_ref[...]).astype(BF16)
    acc = x
    for lo, hi in chunks:
        h1 = jnp.dot(xn, w1_ref[:, lo:hi], preferred_element_type=F32)
        h3 = jnp.dot(xn, w3_ref[:, lo:hi], preferred_element_type=F32)
        acc = acc + _bdot(jax.nn.silu(h1) * h3, w2_ref[lo:hi, :])
    o_ref[...] = acc


def _ffn(x, g, w1, w3, w2):
    t, d = x.shape
    dff = w1.shape[1]
    step = 1024
    chunks = tuple((lo, min(lo + step, dff)) for lo in range(0, dff, step))
    tile = pl.BlockSpec((TM, d), lambda i: (i, 0))
    return pl.pallas_call(
        functools.partial(_ffn_kernel, chunks=chunks),
        grid=(t // TM,),
        in_specs=[tile, _const((1, d)), _const(w1.shape), _const(w3.shape), _const(w2.shape)],
        out_specs=tile,
        out_shape=jax.ShapeDtypeStruct((t, d), F32),
        compiler_params=_params("arbitrary"),
        name="ffn",
    )(x, g, w1, w3, w2)


def _router_kernel(x_ref, g_ref, r_ref, xn_ref, comb_ref, *, n_experts):
    xn = _rmsnorm(x_ref[...], g_ref[...])
    xh = xn.astype(BF16)
    xl = (xn - xh.astype(F32)).astype(BF16)
    r = r_ref[...]
    rh = r.astype(BF16)
    rl = (r - rh.astype(F32)).astype(BF16)
    logits = (jnp.dot(xh, rh, preferred_element_type=F32) + jnp.dot(xl, rh, preferred_element_type=F32)
              + jnp.dot(xh, rl, preferred_element_type=F32))
    lane = lax.broadcasted_iota(jnp.int32, logits.shape, 1).astype(F32)
    neg = jnp.float32(-jnp.inf)
    logits = jnp.where(lane < n_experts, logits, neg)
    v1 = jnp.max(logits, axis=-1, keepdims=True)
    i1 = jnp.min(jnp.where(logits == v1, lane, float(LANES)), axis=-1, keepdims=True)
    rest = jnp.where(lane == i1, neg, logits)
    v2 = jnp.max(rest, axis=-1, keepdims=True)
    i2 = jnp.min(jnp.where(rest == v2, lane, float(LANES)), axis=-1, keepdims=True)
    e2 = jnp.exp(v2 - v1)
    den = 1.0 + e2
    comb_ref[...] = jnp.where(lane == i1, 1.0 / den, jnp.where(lane == i2, e2 / den, 0.0))
    xn_ref[...] = xh


def _router(x, g, router):
    t, d = x.shape
    ne = router.shape[1]
    assert TOP_K == 2 and ne <= LANES
    rpad = jnp.pad(router, ((0, 0), (0, LANES - ne)))
    return pl.pallas_call(
        functools.partial(_router_kernel, n_experts=ne),
        grid=(t // TM,),
        in_specs=[pl.BlockSpec((TM, d), lambda i: (i, 0)), _const((1, d)), _const(rpad.shape)],
        out_specs=[pl.BlockSpec((TM, d), lambda i: (i, 0)), pl.BlockSpec((TM, LANES), lambda i: (i, 0))],
        out_shape=[jax.ShapeDtypeStruct((t, d), BF16), jax.ShapeDtypeStruct((t, LANES), F32)],
        compiler_params=_params("arbitrary"),
        name="router",
    )(x, g, rpad)


def _moe_kernel(xn_ref, comb_ref, x_ref, w1_ref, w3_ref, w2_ref, o_ref, acc_scr):
    e = pl.program_id(1)
    f = pl.program_id(2)

    @pl.when((e == 0) & (f == 0))
    def _():
        acc_scr[...] = x_ref[...]

    xn = xn_ref[...]
    h1 = jnp.dot(xn, w1_ref[...], preferred_element_type=F32)
    h3 = jnp.dot(xn, w3_ref[...], preferred_element_type=F32)
    y = _bdot(jax.nn.silu(h1) * h3, w2_ref[...])
    comb = comb_ref[...]
    lane = lax.broadcasted_iota(jnp.int32, comb.shape, 1)
    gate = jnp.sum(jnp.where(lane == e, comb, 0.0), axis=-1, keepdims=True)
    acc_scr[...] += gate * y

    @pl.when((e == pl.num_programs(1) - 1) & (f == pl.num_programs(2) - 1))
    def _():
        o_ref[...] = acc_scr[...]


def _moe(xn, comb, x, w1, w3, w2):
    t, d = x.shape
    ne, _, dff = w1.shape
    row = lambda i, e, f: (i, 0)
    return pl.pallas_call(
        _moe_kernel,
        grid=(t // MOE_TM, ne, dff // MOE_TF),
        in_specs=[pl.BlockSpec((MOE_TM, d), row), pl.BlockSpec((MOE_TM, LANES), row),
                  pl.BlockSpec((MOE_TM, d), row),
                  pl.BlockSpec((None, d, MOE_TF), lambda i, e, f: (e, 0, f)),
                  pl.BlockSpec((None, d, MOE_TF), lambda i, e, f: (e, 0, f)),
                  pl.BlockSpec((None, MOE_TF, d), lambda i, e, f: (e, f, 0))],
        out_specs=pl.BlockSpec((MOE_TM, d), row),
        out_shape=jax.ShapeDtypeStruct((t, d), F32),
        scratch_shapes=[pltpu.VMEM((MOE_TM, d), F32)],
        compiler_params=_params("arbitrary", "arbitrary", "arbitrary"),
        name="moe",
    )(xn, comb, x, w1, w3, w2)


def _final_kernel(x_ref, g_ref, op_ref, os_ref, *, npt):
    i = pl.program_id(0)
    y = _rmsnorm(x_ref[...], g_ref[...])

    @pl.when(i < npt)
    def _():
        op_ref[...] = y

    @pl.when(i == npt)
    def _():
        os_ref[...] = y


def _final_norm(x, g, *, npt):
    t, d = x.shape
    return pl.pallas_call(
        functools.partial(_final_kernel, npt=npt),
        grid=(npt + 1,),
        in_specs=[pl.BlockSpec((TM, d), lambda i: (i, 0)), _const((1, d))],
        out_specs=[pl.BlockSpec((TM, d), lambda i: (jnp.minimum(i, npt - 1), 0)),
                   pl.BlockSpec((TM, d), lambda i: (0, 0))],
        out_shape=[jax.ShapeDtypeStruct((npt * TM, d), F32), jax.ShapeDtypeStruct((TM, d), F32)],
        compiler_params=_params("arbitrary"),
        name="final_norm",
    )(x, g)


def kernel(x_prompt, x_sample, cache_mem_k, cache_mem_v, state_conv, state_pool, mem_prompt, norm_mix, norm_xa, norm_ffn, norm_mem, norm_final, a_w_in, a_ln_g, a_ln_b, a_w_s, a_b_s, a_w_out, b_w_in, b_conv, b_w_out, c_w_in, c_w_grp, c_scale, c_w_out, xa_wq, xa_wk, xa_wv, xa_wo, f_w1, f_w3, f_w2, m_router, m_w1, m_w3, m_w2):
    nb, seq, d = x_prompt.shape
    bs, ls, _ = x_sample.shape
    depth = norm_mix.shape[0]
    nm = mem_prompt.shape[1]
    assert seq % TM == 0 and TM == bs * ls and (nb * nm) % TM == 0
    tps = seq // TM
    npt = nb * tps
    pos0 = PAST_LEN
    dims = dict(npt=npt, bs=bs, ls=ls)
    bf = lambda w: w.astype(BF16)
    row = lambda v: v.reshape(1, d)
    to_pm = lambda a: a.transpose(1, 0, 2).reshape(-1, d)
    from_pm = lambda a, r: a.reshape(r, bs, d).transpose(1, 0, 2)

    mem_k, mem_v = _mem_kv(mem_prompt.reshape(nb * nm, d), norm_mem[:, None, :], bf(xa_wk), bf(xa_wv))
    kp = mem_k.reshape(depth, nb, nm, d)
    vp = mem_v.reshape(depth, nb, nm, d)
    ks = cache_mem_k.reshape(depth, bs, nm, d)
    vs = cache_mem_v.reshape(depth, bs, nm, d)

    x = jnp.concatenate([x_prompt.reshape(nb * seq, d), to_pm(x_sample)], axis=0)
    conv_p, conv_s, pool_p, pool_s, chunk_v = [], [], [], [], []
    for i in range(depth):
        kind, j = i % 3, i // 3
        g = row(norm_mix[i])
        if kind == 0:
            x, v_s = _mixer_a(x, g, bf(a_w_in[j]), row(a_ln_g[j]), row(a_ln_b[j]), a_w_s[j], a_b_s[j],
                              bf(a_w_out[j]), **dims)
            chunk_v.append(from_pm(v_s, ls))
        elif kind == 1:
            x, tail, z_s = _mixer_b(x, g, bf(b_w_in[j]), b_conv[j], to_pm(state_conv[j]), bf(b_w_out[j]),
                                    npt=npt, tps=tps, bs=bs)
            conv_p.append(tail[:, 6:, :])
            conv_s.append(from_pm(z_s, 2))
        else:
            x, tail, h_s = _mixer_c(x, g, bf(c_w_in[j]), to_pm(state_pool[j]), bf(c_w_grp[j]),
                                    row(c_scale[j]), bf(c_w_out[j]), tps=tps, pos0=pos0, **dims)
            pool_p.append(tail[:, 1:, :])
            pool_s.append(jnp.concatenate([state_pool[j], from_pm(h_s, ls)], axis=1)[:, -POOL_CTX:, :])
        x = _xattn(x, row(norm_xa[i]), bf(xa_wq[i]), bf(xa_wo[i]), kp, vp, ks, vs, i, tps=tps, **dims)
        g = row(norm_ffn[i])
        if i % 2 == 0:
            x = _ffn(x, g, bf(f_w1[i // 2]), bf(f_w3[i // 2]), bf(f_w2[i // 2]))
        else:
            xn, comb = _router(x, g, m_router[i // 2])
            x = _moe(xn, comb, x, bf(m_w1[i // 2]), bf(m_w3[i // 2]), bf(m_w2[i // 2]))
    y_p, y_s = _final_norm(x, row(norm_final), npt=npt)

    hd = d // X_HEADS
    return (y_p.reshape(nb, seq, d), from_pm(y_s, ls),
            mem_k.reshape(depth, nb, nm, X_HEADS, hd), mem_v.reshape(depth, nb, nm, X_HEADS, hd),
            jnp.stack(conv_p), jnp.stack(pool_p), jnp.stack(conv_s), jnp.stack(pool_s), jnp.stack(chunk_v))
```

```python
import functools

import jax
import jax.numpy as jnp
from jax import lax
from jax.experimental import pallas as pl
from jax.experimental.pallas import tpu as pltpu

F32 = jnp.float32
BF16 = jnp.bfloat16

EPS = 1e-6
TM = 512
CHUNK = 128
A_GROUPS = 8
POOL_WINDOWS = (2, 4, 8, 16)
POOL_CTX = max(POOL_WINDOWS) - 1
X_HEADS = 4
TOP_K = 2
PAST_LEN = 16384
LANES = 128
MOE_TM = 512
MOE_TF = 512
VMEM_LIMIT = 60 * 1024 * 1024


def _params(*sem):
    return pltpu.CompilerParams(dimension_semantics=sem, vmem_limit_bytes=VMEM_LIMIT)


def _const(shape):
    nd = len(shape)
    return pl.BlockSpec(shape, lambda *_: (0,) * nd, pipeline_mode=pl.Buffered(1))


def _rmsnorm(x, g):
    r = lax.rsqrt(jnp.mean(x * x, axis=-1, keepdims=True) + EPS)
    return (x * r) * g


def _bdot(a, w):
    return jnp.dot(a.astype(BF16), w, preferred_element_type=F32)


def _mem_kv_kernel(m_ref, g_ref, wk_ref, wv_ref, k_ref, v_ref, k5_ref, v5_ref):
    mn = _rmsnorm(m_ref[...], g_ref[...]).astype(BF16)
    nbt, nm, nh, hd = k5_ref.shape
    for w_ref, o_ref, o5_ref in ((wk_ref, k_ref, k5_ref), (wv_ref, v_ref, v5_ref)):
        y = jnp.dot(mn, w_ref[...], preferred_element_type=F32)
        o_ref[...] = y.astype(BF16)
        for b in range(nbt):
            for h in range(nh):
                o5_ref[b, :, h, :] = y[b * nm:(b + 1) * nm, h * hd:(h + 1) * hd]


def _mem_kv(mem, g, wk, wv, nm):
    rows, d = mem.shape
    depth = g.shape[0]
    hd = d // X_HEADS
    out = jax.ShapeDtypeStruct((depth, rows, d), BF16)
    out5 = jax.ShapeDtypeStruct((depth, rows // nm, nm, X_HEADS, hd), F32)
    wspec = pl.BlockSpec((None, d, d), lambda l, r: (l, 0, 0))
    ospec = pl.BlockSpec((None, TM, d), lambda l, r: (l, r, 0))
    o5spec = pl.BlockSpec((None, TM // nm, nm, X_HEADS, hd), lambda l, r: (l, r, 0, 0, 0))
    return pl.pallas_call(
        _mem_kv_kernel,
        grid=(depth, rows // TM),
        in_specs=[pl.BlockSpec((TM, d), lambda l, r: (r, 0)),
                  pl.BlockSpec((None, 1, d), lambda l, r: (l, 0, 0)),
                  wspec, wspec],
        out_specs=[ospec, ospec, o5spec, o5spec],
        out_shape=[out, out, out5, out5],
        compiler_params=_params("arbitrary", "arbitrary"),
        name="mem_kv",
    )(mem, g, wk, wv)


def _mixer_a_kernel(x_ref, g_ref, win_ref, lng_ref, lnb_ref, ws_ref, bs_ref, sc_ref, sb_ref, wout_ref,
                    o_ref, vs_ref, u_scr, v_scr, s_scr, *, npt, bs, ls):
    i = pl.program_id(0)
    d = x_ref.shape[1]
    x = x_ref[...]
    z = jax.nn.gelu(_bdot(_rmsnorm(x, g_ref[...]), win_ref[...]))
    u_scr[...] = z[:, :d]
    v = z[:, d:]
    mu = jnp.mean(v, axis=-1, keepdims=True)
    var = jnp.mean(jnp.square(v - mu), axis=-1, keepdims=True)
    v_scr[...] = ((v - mu) * lax.rsqrt(var + EPS)) * lng_ref[...] + lnb_ref[...]
    gd = d // A_GROUPS

    @pl.when(i < npt)
    def _():
        rows = lax.broadcasted_iota(jnp.int32, (CHUNK, CHUNK), 0)
        cols = lax.broadcasted_iota(jnp.int32, (CHUNK, CHUNK), 1)
        for g in range(A_GROUPS):
            wg = jnp.where(cols <= rows, ws_ref[g], 0.0).astype(BF16)
            bg = bs_ref[g]
            for c in range(TM // CHUNK):
                vc = v_scr[c * CHUNK:(c + 1) * CHUNK, g * gd:(g + 1) * gd]
                s_scr[c * CHUNK:(c + 1) * CHUNK, g * gd:(g + 1) * gd] = (
                    jnp.dot(wg, vc.astype(BF16), preferred_element_type=F32) + bg)

    @pl.when(i == npt)
    def _():
        for t in range(ls):
            acc = jnp.broadcast_to(sb_ref[t:t + 1, :], (bs, d))
            for s in range(t + 1):
                acc = acc + sc_ref[t * ls + s:t * ls + s + 1, :] * v_scr[s * bs:(s + 1) * bs, :]
            s_scr[t * bs:(t + 1) * bs, :] = acc
        vs_ref[...] = v_scr[...]

    o_ref[...] = x + _bdot(u_scr[...] * s_scr[...], wout_ref[...])


def _mixer_a(x, g, w_in, ln_g, ln_b, w_s, b_s, w_out, *, npt, bs, ls):
    t, d = x.shape
    assert TM == bs * ls and TM % CHUNK == 0 and ls <= CHUNK
    gd = d // A_GROUPS
    sc = jnp.repeat(w_s[:, :ls, :ls].transpose(1, 2, 0).reshape(ls * ls, A_GROUPS), gd, axis=1)
    sb = jnp.repeat(b_s[:, :ls].T, gd, axis=1)
    tile = pl.BlockSpec((TM, d), lambda i: (i, 0))
    return pl.pallas_call(
        functools.partial(_mixer_a_kernel, npt=npt, bs=bs, ls=ls),
        grid=(npt + 1,),
        in_specs=[tile, _const((1, d)), _const(w_in.shape), _const((1, d)), _const((1, d)),
                  _const(w_s.shape), _const((A_GROUPS, CHUNK, 1)), _const(sc.shape), _const(sb.shape),
                  _const(w_out.shape)],
        out_specs=[tile, pl.BlockSpec((TM, d), lambda i: (0, 0))],
        out_shape=[jax.ShapeDtypeStruct((t, d), F32), jax.ShapeDtypeStruct((TM, d), F32)],
        scratch_shapes=[pltpu.VMEM((TM, d), F32)] * 3,
        compiler_params=_params("arbitrary"),
        name="mixer_a",
    )(x, g, w_in, ln_g, ln_b, w_s, b_s[:, :, None], sc, sb, w_out)


def _mixer_b_kernel(x_ref, g_ref, win_ref, cw_ref, pre_ref, wout_ref,
                    o_ref, tail_ref, zs_ref, conv_scr, carry_scr, *, npt, tps, bs):
    i = pl.program_id(0)
    d = x_ref.shape[1]
    x = x_ref[...]
    p = _bdot(_rmsnorm(x, g_ref[...]), win_ref[...])
    gate_b = p[:, :d]
    zc = p[:, d:2 * d] * p[:, 2 * d:]
    w0, w1, w2 = cw_ref[0:1, :], cw_ref[1:2, :], cw_ref[2:3, :]

    @pl.when(i == 0)
    def _():
        carry_scr[...] = jnp.zeros_like(carry_scr)

    @pl.when(i < npt)
    def _():
        keep = (i % tps) != 0
        c6 = jnp.where(keep, carry_scr[6:7, :], 0.0)
        c7 = jnp.where(keep, carry_scr[7:8, :], 0.0)
        rows = lax.broadcasted_iota(jnp.int32, (TM, 1), 0)
        sh1 = jnp.where(rows == 0, c7, pltpu.roll(zc, 1, 0))
        sh2 = jnp.where(rows == 0, c6, jnp.where(rows == 1, c7, pltpu.roll(zc, 2, 0)))
        conv_scr[...] = w0 * sh2 + w1 * sh1 + w2 * zc
        carry_scr[...] = zc[TM - 8:, :]
        tail_ref[...] = zc[TM - 8:, :]

    @pl.when(i == npt)
    def _():
        p0, p1 = pre_ref[0:bs, :], pre_ref[bs:2 * bs, :]
        sh1 = jnp.concatenate([p1, zc[:TM - bs]], axis=0)
        sh2 = jnp.concatenate([p0, p1, zc[:TM - 2 * bs]], axis=0)
        conv_scr[...] = w0 * sh2 + w1 * sh1 + w2 * zc
        zs_ref[...] = zc[TM - 2 * bs:, :]

    o_ref[...] = x + _bdot(gate_b * conv_scr[...], wout_ref[...])


def _mixer_b(x, g, w_in, conv_w, prefix, w_out, *, npt, tps, bs):
    t, d = x.shape
    nb = npt // tps
    tile = pl.BlockSpec((TM, d), lambda i: (i, 0))
    return pl.pallas_call(
        functools.partial(_mixer_b_kernel, npt=npt, tps=tps, bs=bs),
        grid=(npt + 1,),
        in_specs=[tile, _const((1, d)), _const(w_in.shape), _const(conv_w.shape), _const(prefix.shape),
                  _const(w_out.shape)],
        out_specs=[tile,
                   pl.BlockSpec((None, 8, d), lambda i: (jnp.minimum(i // tps, nb - 1), 0, 0)),
                   pl.BlockSpec((2 * bs, d), lambda i: (0, 0))],
        out_shape=[jax.ShapeDtypeStruct((t, d), F32), jax.ShapeDtypeStruct((nb, 8, d), F32),
                   jax.ShapeDtypeStruct((2 * bs, d), F32)],
        scratch_shapes=[pltpu.VMEM((TM, d), F32), pltpu.VMEM((8, d), F32)],
        compiler_params=_params("arbitrary"),
        name="mixer_b",
    )(x, g, w_in, conv_w, prefix, w_out)


def _mixer_c_kernel(x_ref, g_ref, win_ref, pre_ref, wgrp_ref, scale_ref, wout_ref,
                    o_ref, tail_ref, hs_ref, d_scr, hist_scr, *, npt, tps, bs, ls, pos0):
    i = pl.program_id(0)
    d = x_ref.shape[1]
    gd = d // len(POOL_WINDOWS)
    hp = POOL_CTX + 1
    x = x_ref[...]
    h = _bdot(_rmsnorm(x, g_ref[...]), win_ref[...])

    @pl.when(i == 0)
    def _():
        hist_scr[...] = jnp.zeros_like(hist_scr)

    @pl.when(i < npt)
    def _():
        keep = (i % tps) != 0
        ext = jnp.concatenate([jnp.where(keep, hist_scr[...], 0.0), h], axis=0)
        pos = (i % tps) * TM + lax.broadcasted_iota(jnp.int32, (TM, 1), 0)
        sums = ext
        width = 1
        for g, w in enumerate(POOL_WINDOWS):
            while width < w:
                sums = sums + pltpu.roll(sums, width, 0)
                width *= 2
            cnt = jnp.minimum(w, pos + 1).astype(F32)
            sl = slice(g * gd, (g + 1) * gd)
            d_scr[:, sl] = sums[hp:, sl] / cnt - h[:, sl]
        hist_scr[...] = h[TM - hp:, :]
        tail_ref[...] = h[TM - hp:, :]

    @pl.when(i == npt)
    def _():
        for g, w in enumerate(POOL_WINDOWS):
            sl = slice(g * gd, (g + 1) * gd)
            run = jnp.zeros((bs, gd), F32)
            tails = [run]
            for m in range(1, w):
                run = run + pre_ref[(POOL_CTX - m) * bs:(POOL_CTX - m + 1) * bs, sl]
                tails.append(run)
            for t in range(ls):
                n = min(w, t + 1)
                acc = tails[w - n]
                for j in range(n):
                    acc = acc + h[(t - j) * bs:(t - j + 1) * bs, sl]
                cnt = float(min(w, pos0 + t + 1))
                d_scr[t * bs:(t + 1) * bs, sl] = acc / cnt - h[t * bs:(t + 1) * bs, sl]
        hs_ref[...] = h

    m = jnp.concatenate(
        [_bdot(d_scr[:, g * gd:(g + 1) * gd], wgrp_ref[g]) for g in range(len(POOL_WINDOWS))], axis=1)
    o_ref[...] = x + _bdot(m * scale_ref[...], wout_ref[...])


def _mixer_c(x, g, w_in, prefix, w_grp, scale, w_out, *, npt, tps, bs, ls, pos0):
    t, d = x.shape
    nb = npt // tps
    hp = POOL_CTX + 1
    tile = pl.BlockSpec((TM, d), lambda i: (i, 0))
    return pl.pallas_call(
        functools.partial(_mixer_c_kernel, npt=npt, tps=tps, bs=bs, ls=ls, pos0=pos0),
        grid=(npt + 1,),
        in_specs=[tile, _const((1, d)), _const(w_in.shape), _const(prefix.shape), _const(w_grp.shape),
                  _const((1, d)), _const(w_out.shape)],
        out_specs=[tile,
                   pl.BlockSpec((None, hp, d), lambda i: (jnp.minimum(i // tps, nb - 1), 0, 0)),
                   pl.BlockSpec((TM, d), lambda i: (0, 0))],
        out_shape=[jax.ShapeDtypeStruct((t, d), F32), jax.ShapeDtypeStruct((nb, hp, d), F32),
                   jax.ShapeDtypeStruct((TM, d), F32)],
        scratch_shapes=[pltpu.VMEM((TM, d), F32), pltpu.VMEM((hp, d), F32)],
        compiler_params=_params("arbitrary"),
        name="mixer_c",
    )(x, g, w_in, prefix, w_grp, scale, w_out)


def _softmax_rows(s):
    e = jnp.exp(s - jnp.max(s, axis=-1, keepdims=True))
    return e / jnp.sum(e, axis=-1, keepdims=True)


def _xattn_kernel(x_ref, xs_ref, g_ref, wq_ref, wo_ref, kp_ref, vp_ref, ks_ref, vs_ref,
                  o_ref, q_scr, o_scr, *, npt, bs, ls, spb):
    i = pl.program_id(0)
    d = x_ref.shape[1]
    hd = d // X_HEADS
    qscale = float(hd) ** -0.5
    nt = (((1,), (1,)), ((), ()))

    @pl.when(i == 0)
    def _():
        q_scr[...] = _bdot(_rmsnorm(xs_ref[...], g_ref[...]), wq_ref[...]) * qscale

    @pl.when(i < npt)
    def _():
        x = x_ref[...]
        q = (_bdot(_rmsnorm(x, g_ref[...]), wq_ref[...]) * qscale).astype(BF16)
        k = kp_ref[...]
        v = vp_ref[...]
        heads = []
        for h in range(X_HEADS):
            sl = slice(h * hd, (h + 1) * hd)
            s = lax.dot_general(q[:, sl], k[:, sl], nt, preferred_element_type=F32)
            heads.append(_bdot(_softmax_rows(s), v[:, sl]))
        o_ref[...] = x + _bdot(jnp.concatenate(heads, axis=1), wo_ref[...])

        nm = ks_ref.shape[1]
        rows = lax.broadcasted_iota(jnp.int32, (X_HEADS * ls, nm * X_HEADS), 0)
        cols = lax.broadcasted_iota(jnp.int32, (X_HEADS * ls, nm * X_HEADS), 1)
        own = (rows // ls) == (cols % X_HEADS)
        for bb in range(spb):
            b = i * spb + bb
            qb = jnp.concatenate([q_scr[pl.ds(t * bs + b, 1), :] for t in range(ls)], axis=0)
            qh = jnp.concatenate([qb[:, h * hd:(h + 1) * hd] for h in range(X_HEADS)], axis=0)
            kr = ks_ref[bb].reshape(nm * X_HEADS, hd).astype(BF16)
            vr = vs_ref[bb].reshape(nm * X_HEADS, hd).astype(BF16)
            s = lax.dot_general(qh.astype(BF16), kr, nt, preferred_element_type=F32)
            of = _bdot(_softmax_rows(jnp.where(own, s, -jnp.inf)), vr)
            ob = jnp.concatenate([of[h * ls:(h + 1) * ls] for h in range(X_HEADS)], axis=1)
            for t in range(ls):
                o_scr[pl.ds(t * bs + b, 1), :] = ob[t:t + 1]

    @pl.when(i == npt)
    def _():
        o_ref[...] = x_ref[...] + _bdot(o_scr[...], wo_ref[...])


def _xattn(x, g, wq, wo, kp, vp, ks, vs, layer, *, npt, tps, bs, ls):
    t, d = x.shape
    nb, nm = kp.shape[1], kp.shape[2]
    assert bs % npt == 0
    spb = bs // npt
    tile = pl.BlockSpec((TM, d), lambda i: (i, 0))
    pmem = pl.BlockSpec((None, None, nm, d), lambda i: (layer, jnp.minimum(i // tps, nb - 1), 0, 0))
    smem = pl.BlockSpec((None, spb, nm, X_HEADS, d // X_HEADS),
                        lambda i: (layer, jnp.minimum(i, npt - 1), 0, 0, 0))
    return pl.pallas_call(
        functools.partial(_xattn_kernel, npt=npt, bs=bs, ls=ls, spb=spb),
        grid=(npt + 1,),
        in_specs=[tile, pl.BlockSpec((TM, d), lambda i: (npt, 0), pipeline_mode=pl.Buffered(1)),
                  _const((1, d)), _const(wq.shape), _const(wo.shape), pmem, pmem, smem, smem],
        out_specs=tile,
        out_shape=jax.ShapeDtypeStruct((t, d), F32),
        scratch_shapes=[pltpu.VMEM((TM, d), F32), pltpu.VMEM((TM, d), F32)],
        compiler_params=_params("arbitrary"),
        name="xattn",
    )(x, x, g, wq, wo, kp, vp, ks, vs)


def _ffn_kernel(x_ref, g_ref, w1_ref, w3_ref, w2_ref, o_ref, *, chunks):
    x = x_ref[...]
    xn = _rmsnorm(x, g_ref[...]).astype(BF16)
    acc = x
    for lo, hi in chunks:
        h1 = jnp.dot(xn, w1_ref[:, lo:hi], preferred_element_type=F32)
        h3 = jnp.dot(xn, w3_ref[:, lo:hi], preferred_element_type=F32)
        acc = acc + _bdot(jax.nn.silu(h1) * h3, w2_ref[lo:hi, :])
    o_ref[...] = acc


def _ffn(x, g, w1, w3, w2):
    t, d = x.shape
    dff = w1.shape[1]
    step = 1024
    chunks = tuple((lo, min(lo + step, dff)) for lo in range(0, dff, step))
    tile = pl.BlockSpec((TM, d), lambda i: (i, 0))
    return pl.pallas_call(
        functools.partial(_ffn_kernel, chunks=chunks),
        grid=(t // TM,),
        in_specs=[tile, _const((1, d)), _const(w1.shape), _const(w3.shape), _const(w2.shape)],
        out_specs=tile,
        out_shape=jax.ShapeDtypeStruct((t, d), F32),
        compiler_params=_params("arbitrary"),
        name="ffn",
    )(x, g, w1, w3, w2)


def _router_kernel(x_ref, g_ref, r_ref, tri_ref, xn_ref, meta_ref, cnt_ref, carry_scr, *, n_experts):
    i = pl.program_id(0)

    @pl.when(i == 0)
    def _():
        carry_scr[...] = jnp.zeros_like(carry_scr)

    xn = _rmsnorm(x_ref[...], g_ref[...])
    xn_ref[...] = xn
    xh = xn.astype(BF16)
    xl = (xn - xh.astype(F32)).astype(BF16)
    r = r_ref[...]
    rh = r.astype(BF16)
    rl = (r - rh.astype(F32)).astype(BF16)
    logits = (jnp.dot(xh, rh, preferred_element_type=F32) + jnp.dot(xl, rh, preferred_element_type=F32)
              + jnp.dot(xh, rl, preferred_element_type=F32))
    lane = lax.broadcasted_iota(jnp.int32, logits.shape, 1).astype(F32)
    neg = jnp.float32(-jnp.inf)
    logits = jnp.where(lane < n_experts, logits, neg)
    v1 = jnp.max(logits, axis=-1, keepdims=True)
    i1 = jnp.min(jnp.where(logits == v1, lane, float(LANES)), axis=-1, keepdims=True)
    rest = jnp.where(lane == i1, neg, logits)
    v2 = jnp.max(rest, axis=-1, keepdims=True)
    i2 = jnp.min(jnp.where(rest == v2, lane, float(LANES)), axis=-1, keepdims=True)
    e2 = jnp.exp(v2 - v1)
    den = 1.0 + e2
    hit = jnp.where(lane == i1, 1.0, jnp.where(lane == i2, 1.0, 0.0))
    before = jnp.dot(tri_ref[...], hit.astype(BF16), preferred_element_type=F32) + carry_scr[0:1, :]
    r1 = jnp.sum(jnp.where(lane == i1, before, 0.0), axis=-1, keepdims=True)
    r2 = jnp.sum(jnp.where(lane == i2, before, 0.0), axis=-1, keepdims=True)
    cols = (i1, i2, r1, r2, 1.0 / den, e2 / den)
    meta = jnp.zeros_like(logits)
    for c, val in enumerate(cols):
        meta = jnp.where(lane == c, val, meta)
    meta_ref[...] = meta
    carry_scr[...] = carry_scr[...] + jnp.sum(hit, axis=0, keepdims=True)
    cnt_ref[...] = carry_scr[...]


def _router(x, g, router):
    t, d = x.shape
    ne = router.shape[1]
    assert TOP_K == 2 and ne <= LANES
    rpad = jnp.pad(router, ((0, 0), (0, LANES - ne)))
    tri = jnp.tril(jnp.ones((TM, TM), BF16), -1)
    return pl.pallas_call(
        functools.partial(_router_kernel, n_experts=ne),
        grid=(t // TM,),
        in_specs=[pl.BlockSpec((TM, d), lambda i: (i, 0)), _const((1, d)), _const(rpad.shape),
                  _const(tri.shape)],
        out_specs=[pl.BlockSpec((TM, d), lambda i: (i, 0)), pl.BlockSpec((TM, LANES), lambda i: (i, 0)),
                   pl.BlockSpec((8, LANES), lambda i: (0, 0))],
        out_shape=[jax.ShapeDtypeStruct((t, d), F32), jax.ShapeDtypeStruct((t, LANES), F32),
                   jax.ShapeDtypeStruct((8, LANES), F32)],
        scratch_shapes=[pltpu.VMEM((8, LANES), F32)],
        compiler_params=_params("arbitrary"),
        name="router",
    )(x, g, rpad, tri)


def _dispatch_kernel(dst_ref, xn_ref, xs_ref, z_scr, zsem, sem, *, n_fill):
    i = pl.program_id(0)

    @pl.when(i == 0)
    def _():
        z_scr[...] = jnp.zeros_like(z_scr)
        fills = [pltpu.make_async_copy(z_scr, xs_ref.at[pl.ds(j * MOE_TM, MOE_TM)], zsem)
                 for j in range(n_fill)]
        for c in fills:
            c.start()
        for c in fills:
            c.wait()

    def body(r, carry):
        src = xn_ref.at[pl.ds(i * TM + r, 1)]
        for k in range(TOP_K):
            pltpu.make_async_copy(src, xs_ref.at[pl.ds(dst_ref[TOP_K * r + k], 1)], sem).start()
        return carry

    lax.fori_loop(0, TM, body, 0, unroll=8)
    for _ in range(TOP_K):
        pltpu.make_async_copy(xn_ref.at[pl.ds(0, TM)], xs_ref.at[pl.ds(0, TM)], sem).wait()


def _dispatch(dst, xn, n_slot_tiles):
    t, d = xn.shape
    return pl.pallas_call(
        functools.partial(_dispatch_kernel, n_fill=n_slot_tiles),
        grid=(t // TM,),
        in_specs=[pl.BlockSpec((TOP_K * TM,), lambda i: (i,), memory_space=pltpu.SMEM),
                  pl.BlockSpec(memory_space=pl.ANY)],
        out_specs=pl.BlockSpec(memory_space=pl.ANY),
        out_shape=jax.ShapeDtypeStruct((n_slot_tiles * MOE_TM, d), F32),
        scratch_shapes=[pltpu.VMEM((MOE_TM, d), F32), pltpu.SemaphoreType.DMA(()),
                        pltpu.SemaphoreType.DMA(())],
        compiler_params=_params("arbitrary"),
        name="dispatch",
    )(dst, xn)


def _moe_kernel(te_ref, tf_ref, nt_ref, xs_ref, w1_ref, w3_ref, w2_ref, ys_ref,
                xb_scr, w1_scr, w3_scr, w2_scr):
    i = pl.program_id(0)
    f = pl.program_id(1)

    @pl.when(i < nt_ref[0])
    def _():
        @pl.when(f == 0)
        def _():
            xb_scr[...] = xs_ref[...].astype(BF16)

        @pl.when(tf_ref[i] == 1)
        def _():
            w1_scr[f] = w1_ref[...].astype(BF16)
            w3_scr[f] = w3_ref[...].astype(BF16)
            w2_scr[f] = w2_ref[...].astype(BF16)

        xb = xb_scr[...]
        h1 = jnp.dot(xb, w1_scr[f], preferred_element_type=F32)
        h3 = jnp.dot(xb, w3_scr[f], preferred_element_type=F32)
        y = _bdot(jax.nn.silu(h1) * h3, w2_scr[f])

        @pl.when(f == 0)
        def _():
            ys_ref[...] = y

        @pl.when(f > 0)
        def _():
            ys_ref[...] += y

    @pl.when((i >= nt_ref[0]) & (f == 0))
    def _():
        ys_ref[...] = jnp.zeros_like(ys_ref)


def _moe(te, tf, nt, xs, w1, w3, w2):
    p, d = xs.shape
    ne, _, dff = w1.shape
    nf = dff // MOE_TF
    fsel = lambda i, f, tf: jnp.where(tf[i] == 1, f, nf - 1)
    return pl.pallas_call(
        _moe_kernel,
        grid_spec=pltpu.PrefetchScalarGridSpec(
            num_scalar_prefetch=3,
            grid=(p // MOE_TM, nf),
            in_specs=[pl.BlockSpec((MOE_TM, d), lambda i, f, te, tf, nt: (i, 0)),
                      pl.BlockSpec((None, d, MOE_TF), lambda i, f, te, tf, nt: (te[i], 0, fsel(i, f, tf))),
                      pl.BlockSpec((None, d, MOE_TF), lambda i, f, te, tf, nt: (te[i], 0, fsel(i, f, tf))),
                      pl.BlockSpec((None, MOE_TF, d), lambda i, f, te, tf, nt: (te[i], fsel(i, f, tf), 0))],
            out_specs=pl.BlockSpec((MOE_TM, d), lambda i, f, te, tf, nt: (i, 0)),
            scratch_shapes=[pltpu.VMEM((MOE_TM, d), BF16), pltpu.VMEM((nf, d, MOE_TF), BF16),
                            pltpu.VMEM((nf, d, MOE_TF), BF16), pltpu.VMEM((nf, MOE_TF, d), BF16)]),
        out_shape=jax.ShapeDtypeStruct((p, d), F32),
        compiler_params=_params("arbitrary", "arbitrary"),
        name="moe",
    )(te, tf, nt, xs, w1, w3, w2)


def _combine_kernel(dst_ref, x_ref, meta_ref, ys_ref, o_ref, a_scr, b_scr, sem):
    def body(r, carry):
        pltpu.make_async_copy(ys_ref.at[pl.ds(dst_ref[TOP_K * r], 1)], a_scr.at[pl.ds(r, 1)],
                              sem.at[0]).start()
        pltpu.make_async_copy(ys_ref.at[pl.ds(dst_ref[TOP_K * r + 1], 1)], b_scr.at[pl.ds(r, 1)],
                              sem.at[1]).start()
        return carry

    lax.fori_loop(0, TM, body, 0, unroll=8)
    pltpu.make_async_copy(ys_ref.at[pl.ds(0, TM)], a_scr, sem.at[0]).wait()
    pltpu.make_async_copy(ys_ref.at[pl.ds(0, TM)], b_scr, sem.at[1]).wait()
    meta = meta_ref[...]
    o_ref[...] = x_ref[...] + (meta[:, 4:5] * a_scr[...] + meta[:, 5:6] * b_scr[...])


def _combine(dst, x, meta, ys):
    t, d = x.shape
    tile = pl.BlockSpec((TM, d), lambda i: (i, 0))
    return pl.pallas_call(
        _combine_kernel,
        grid=(t // TM,),
        in_specs=[pl.BlockSpec((TOP_K * TM,), lambda i: (i,), memory_space=pltpu.SMEM), tile,
                  pl.BlockSpec((TM, LANES), lambda i: (i, 0)), pl.BlockSpec(memory_space=pl.ANY)],
        out_specs=tile,
        out_shape=jax.ShapeDtypeStruct((t, d), F32),
        scratch_shapes=[pltpu.VMEM((TM, d), F32), pltpu.VMEM((TM, d), F32), pltpu.SemaphoreType.DMA((2,))],
        compiler_params=_params("arbitrary"),
        name="combine",
    )(dst, x, meta, ys)


def _moe_ffn(x, g, router, w1, w3, w2):
    t, d = x.shape
    ne = router.shape[1]
    xn, meta, cnt = _router(x, g, router)
    n = cnt[0, :ne].astype(jnp.int32)
    tiles = (n + MOE_TM - 1) // MOE_TM
    ends = jnp.cumsum(tiles)
    n_tiles = (t * TOP_K) // MOE_TM + ne
    off = (ends - tiles) * MOE_TM
    e = meta[:, :TOP_K].astype(jnp.int32)
    dst = (jnp.take(off, e) + meta[:, TOP_K:2 * TOP_K].astype(jnp.int32)).reshape(-1)
    tid = jnp.minimum(jnp.arange(n_tiles, dtype=jnp.int32), ends[-1] - 1)
    te = jnp.minimum(jnp.searchsorted(ends, tid, side="right"), ne - 1).astype(jnp.int32)
    first = (tid == jnp.take(ends - tiles, te)) & (jnp.arange(n_tiles) < ends[-1])
    xs = _dispatch(dst, xn, n_tiles)
    ys = _moe(te, first.astype(jnp.int32), ends[-1:].astype(jnp.int32), xs, w1, w3, w2)
    return _combine(dst, x, meta, ys)


def _final_kernel(x_ref, g_ref, op_ref, os_ref, *, npt):
    i = pl.program_id(0)
    y = _rmsnorm(x_ref[...], g_ref[...])

    @pl.when(i < npt)
    def _():
        op_ref[...] = y

    @pl.when(i == npt)
    def _():
        os_ref[...] = y


def _final_norm(x, g, *, npt):
    t, d = x.shape
    return pl.pallas_call(
        functools.partial(_final_kernel, npt=npt),
        grid=(npt + 1,),
        in_specs=[pl.BlockSpec((TM, d), lambda i: (i, 0)), _const((1, d))],
        out_specs=[pl.BlockSpec((TM, d), lambda i: (jnp.minimum(i, npt - 1), 0)),
                   pl.BlockSpec((TM, d), lambda i: (0, 0))],
        out_shape=[jax.ShapeDtypeStruct((npt * TM, d), F32), jax.ShapeDtypeStruct((TM, d), F32)],
        compiler_params=_params("arbitrary"),
        name="final_norm",
    )(x, g)


def kernel(x_prompt, x_sample, cache_mem_k, cache_mem_v, state_conv, state_pool, mem_prompt, norm_mix, norm_xa, norm_ffn, norm_mem, norm_final, a_w_in, a_ln_g, a_ln_b, a_w_s, a_b_s, a_w_out, b_w_in, b_conv, b_w_out, c_w_in, c_w_grp, c_scale, c_w_out, xa_wq, xa_wk, xa_wv, xa_wo, f_w1, f_w3, f_w2, m_router, m_w1, m_w3, m_w2):
    nb, seq, d = x_prompt.shape
    bs, ls, _ = x_sample.shape
    depth = norm_mix.shape[0]
    nm = mem_prompt.shape[1]
    assert seq % TM == 0 and TM == bs * ls and (nb * nm) % TM == 0
    tps = seq // TM
    npt = nb * tps
    pos0 = PAST_LEN
    dims = dict(npt=npt, bs=bs, ls=ls)
    bf = lambda w: w.astype(BF16)
    row = lambda v: v.reshape(1, d)
    to_pm = lambda a: a.transpose(1, 0, 2).reshape(-1, d)
    from_pm = lambda a, r: a.reshape(r, bs, d).transpose(1, 0, 2)

    mem_k, mem_v, mem_k5, mem_v5 = _mem_kv(mem_prompt.reshape(nb * nm, d), norm_mem[:, None, :],
                                           bf(xa_wk), bf(xa_wv), nm)
    kp = mem_k.reshape(depth, nb, nm, d)
    vp = mem_v.reshape(depth, nb, nm, d)
    ks, vs = cache_mem_k, cache_mem_v

    x = jnp.concatenate([x_prompt.reshape(nb * seq, d), to_pm(x_sample)], axis=0)
    conv_p, conv_s, pool_p, pool_s, chunk_v = [], [], [], [], []
    for i in range(depth):
        kind, j = i % 3, i // 3
        g = row(norm_mix[i])
        if kind == 0:
            x, v_s = _mixer_a(x, g, bf(a_w_in[j]), row(a_ln_g[j]), row(a_ln_b[j]), a_w_s[j], a_b_s[j],
                              bf(a_w_out[j]), **dims)
            chunk_v.append(from_pm(v_s, ls))
        elif kind == 1:
            x, tail, z_s = _mixer_b(x, g, bf(b_w_in[j]), b_conv[j], to_pm(state_conv[j]), bf(b_w_out[j]),
                                    npt=npt, tps=tps, bs=bs)
            conv_p.append(tail[:, 6:, :])
            conv_s.append(from_pm(z_s, 2))
        else:
            x, tail, h_s = _mixer_c(x, g, bf(c_w_in[j]), to_pm(state_pool[j]), bf(c_w_grp[j]),
                                    row(c_scale[j]), bf(c_w_out[j]), tps=tps, pos0=pos0, **dims)
            pool_p.append(tail[:, 1:, :])
            pool_s.append(jnp.concatenate([state_pool[j], from_pm(h_s, ls)], axis=1)[:, -POOL_CTX:, :])
        x = _xattn(x, row(norm_xa[i]), bf(xa_wq[i]), bf(xa_wo[i]), kp, vp, ks, vs, i, tps=tps, **dims)
        g = row(norm_ffn[i])
        if i % 2 == 0:
            x = _ffn(x, g, bf(f_w1[i // 2]), bf(f_w3[i // 2]), bf(f_w2[i // 2]))
        else:
            x = _moe_ffn(x, g, m_router[i // 2], m_w1[i // 2], m_w3[i // 2], m_w2[i // 2])
    y_p, y_s = _final_norm(x, row(norm_final), npt=npt)

    return (y_p.reshape(nb, seq, d), from_pm(y_s, ls), mem_k5, mem_v5,
            jnp.stack(conv_p), jnp.stack(pool_p), jnp.stack(conv_s), jnp.stack(pool_s), jnp.stack(chunk_v))
```

```python
import functools

import jax
import jax.numpy as jnp
from jax import lax
from jax.experimental import pallas as pl
from jax.experimental.pallas import tpu as pltpu

F32 = jnp.float32
BF16 = jnp.bfloat16

EPS = 1e-6
TM = 512
CHUNK = 128
A_GROUPS = 8
POOL_WINDOWS = (2, 4, 8, 16)
POOL_CTX = max(POOL_WINDOWS) - 1
X_HEADS = 4
TOP_K = 2
PAST_LEN = 16384
LANES = 128
MOE_TM = 512
MOE_TF = 512
VMEM_LIMIT = 60 * 1024 * 1024


def _params(*sem):
    return pltpu.CompilerParams(dimension_semantics=sem, vmem_limit_bytes=VMEM_LIMIT)


def _const(shape):
    nd = len(shape)
    return pl.BlockSpec(shape, lambda *_: (0,) * nd, pipeline_mode=pl.Buffered(1))


def _rmsnorm(x, g):
    r = lax.rsqrt(jnp.mean(x * x, axis=-1, keepdims=True) + EPS)
    return (x * r) * g


def _bdot(a, w):
    return jnp.dot(a.astype(BF16), w, preferred_element_type=F32)


def _mem_kv_kernel(m_ref, g_ref, wk_ref, wv_ref, k_ref, v_ref, k5_ref, v5_ref):
    mn = _rmsnorm(m_ref[...], g_ref[...]).astype(BF16)
    nbt, nm, nh, hd = k5_ref.shape
    for w_ref, o_ref, o5_ref in ((wk_ref, k_ref, k5_ref), (wv_ref, v_ref, v5_ref)):
        y = jnp.dot(mn, w_ref[...], preferred_element_type=F32)
        o_ref[...] = y.astype(BF16)
        for b in range(nbt):
            for h in range(nh):
                o5_ref[b, :, h, :] = y[b * nm:(b + 1) * nm, h * hd:(h + 1) * hd]


def _mem_kv(mem, g, wk, wv, nm):
    rows, d = mem.shape
    depth = g.shape[0]
    hd = d // X_HEADS
    out = jax.ShapeDtypeStruct((depth, rows, d), BF16)
    out5 = jax.ShapeDtypeStruct((depth, rows // nm, nm, X_HEADS, hd), F32)
    wspec = pl.BlockSpec((None, d, d), lambda l, r: (l, 0, 0))
    ospec = pl.BlockSpec((None, TM, d), lambda l, r: (l, r, 0))
    o5spec = pl.BlockSpec((None, TM // nm, nm, X_HEADS, hd), lambda l, r: (l, r, 0, 0, 0))
    return pl.pallas_call(
        _mem_kv_kernel,
        grid=(depth, rows // TM),
        in_specs=[pl.BlockSpec((TM, d), lambda l, r: (r, 0)),
                  pl.BlockSpec((None, 1, d), lambda l, r: (l, 0, 0)),
                  wspec, wspec],
        out_specs=[ospec, ospec, o5spec, o5spec],
        out_shape=[out, out, out5, out5],
        compiler_params=_params("arbitrary", "arbitrary"),
        name="mem_kv",
    )(mem, g, wk, wv)


def _mixer_a_kernel(x_ref, g_ref, win_ref, lng_ref, lnb_ref, ws_ref, bs_ref, sc_ref, sb_ref, wout_ref,
                    o_ref, vs_ref, u_scr, v_scr, s_scr, *, npt, bs, ls):
    i = pl.program_id(0)
    d = x_ref.shape[1]
    x = x_ref[...]
    z = jax.nn.gelu(_bdot(_rmsnorm(x, g_ref[...]), win_ref[...]))
    u_scr[...] = z[:, :d]
    v = z[:, d:]
    mu = jnp.mean(v, axis=-1, keepdims=True)
    var = jnp.mean(jnp.square(v - mu), axis=-1, keepdims=True)
    v_scr[...] = ((v - mu) * lax.rsqrt(var + EPS)) * lng_ref[...] + lnb_ref[...]
    gd = d // A_GROUPS

    @pl.when(i < npt)
    def _():
        rows = lax.broadcasted_iota(jnp.int32, (CHUNK, CHUNK), 0)
        cols = lax.broadcasted_iota(jnp.int32, (CHUNK, CHUNK), 1)
        for g in range(A_GROUPS):
            wg = jnp.where(cols <= rows, ws_ref[g], 0.0).astype(BF16)
            bg = bs_ref[g]
            for c in range(TM // CHUNK):
                vc = v_scr[c * CHUNK:(c + 1) * CHUNK, g * gd:(g + 1) * gd]
                s_scr[c * CHUNK:(c + 1) * CHUNK, g * gd:(g + 1) * gd] = (
                    jnp.dot(wg, vc.astype(BF16), preferred_element_type=F32) + bg)

    @pl.when(i == npt)
    def _():
        for t in range(ls):
            acc = jnp.broadcast_to(sb_ref[t:t + 1, :], (bs, d))
            for s in range(t + 1):
                acc = acc + sc_ref[t * ls + s:t * ls + s + 1, :] * v_scr[s * bs:(s + 1) * bs, :]
            s_scr[t * bs:(t + 1) * bs, :] = acc
        vs_ref[...] = v_scr[...]

    o_ref[...] = x + _bdot(u_scr[...] * s_scr[...], wout_ref[...])


def _mixer_a(x, g, w_in, ln_g, ln_b, w_s, b_s, w_out, *, npt, bs, ls):
    t, d = x.shape
    assert TM == bs * ls and TM % CHUNK == 0 and ls <= CHUNK
    gd = d // A_GROUPS
    sc = jnp.repeat(w_s[:, :ls, :ls].transpose(1, 2, 0).reshape(ls * ls, A_GROUPS), gd, axis=1)
    sb = jnp.repeat(b_s[:, :ls].T, gd, axis=1)
    tile = pl.BlockSpec((TM, d), lambda i: (i, 0))
    return pl.pallas_call(
        functools.partial(_mixer_a_kernel, npt=npt, bs=bs, ls=ls),
        grid=(npt + 1,),
        in_specs=[tile, _const((1, d)), _const(w_in.shape), _const((1, d)), _const((1, d)),
                  _const(w_s.shape), _const((A_GROUPS, CHUNK, 1)), _const(sc.shape), _const(sb.shape),
                  _const(w_out.shape)],
        out_specs=[tile, pl.BlockSpec((TM, d), lambda i: (0, 0))],
        out_shape=[jax.ShapeDtypeStruct((t, d), F32), jax.ShapeDtypeStruct((TM, d), F32)],
        scratch_shapes=[pltpu.VMEM((TM, d), F32)] * 3,
        compiler_params=_params("arbitrary"),
        name="mixer_a",
    )(x, g, w_in, ln_g, ln_b, w_s, b_s[:, :, None], sc, sb, w_out)


def _mixer_b_kernel(x_ref, g_ref, win_ref, cw_ref, pre_ref, wout_ref,
                    o_ref, tail_ref, zs_ref, conv_scr, carry_scr, *, npt, tps, bs):
    i = pl.program_id(0)
    d = x_ref.shape[1]
    x = x_ref[...]
    p = _bdot(_rmsnorm(x, g_ref[...]), win_ref[...])
    gate_b = p[:, :d]
    zc = p[:, d:2 * d] * p[:, 2 * d:]
    w0, w1, w2 = cw_ref[0:1, :], cw_ref[1:2, :], cw_ref[2:3, :]

    @pl.when(i == 0)
    def _():
        carry_scr[...] = jnp.zeros_like(carry_scr)

    @pl.when(i < npt)
    def _():
        keep = (i % tps) != 0
        c6 = jnp.where(keep, carry_scr[6:7, :], 0.0)
        c7 = jnp.where(keep, carry_scr[7:8, :], 0.0)
        rows = lax.broadcasted_iota(jnp.int32, (TM, 1), 0)
        sh1 = jnp.where(rows == 0, c7, pltpu.roll(zc, 1, 0))
        sh2 = jnp.where(rows == 0, c6, jnp.where(rows == 1, c7, pltpu.roll(zc, 2, 0)))
        conv_scr[...] = w0 * sh2 + w1 * sh1 + w2 * zc
        carry_scr[...] = zc[TM - 8:, :]
        tail_ref[...] = zc[TM - 8:, :]

    @pl.when(i == npt)
    def _():
        p0, p1 = pre_ref[0:bs, :], pre_ref[bs:2 * bs, :]
        sh1 = jnp.concatenate([p1, zc[:TM - bs]], axis=0)
        sh2 = jnp.concatenate([p0, p1, zc[:TM - 2 * bs]], axis=0)
        conv_scr[...] = w0 * sh2 + w1 * sh1 + w2 * zc
        zs_ref[...] = zc[TM - 2 * bs:, :]

    o_ref[...] = x + _bdot(gate_b * conv_scr[...], wout_ref[...])


def _mixer_b(x, g, w_in, conv_w, prefix, w_out, *, npt, tps, bs):
    t, d = x.shape
    nb = npt // tps
    tile = pl.BlockSpec((TM, d), lambda i: (i, 0))
    return pl.pallas_call(
        functools.partial(_mixer_b_kernel, npt=npt, tps=tps, bs=bs),
        grid=(npt + 1,),
        in_specs=[tile, _const((1, d)), _const(w_in.shape), _const(conv_w.shape), _const(prefix.shape),
                  _const(w_out.shape)],
        out_specs=[tile,
                   pl.BlockSpec((None, 8, d), lambda i: (jnp.minimum(i // tps, nb - 1), 0, 0)),
                   pl.BlockSpec((2 * bs, d), lambda i: (0, 0))],
        out_shape=[jax.ShapeDtypeStruct((t, d), F32), jax.ShapeDtypeStruct((nb, 8, d), F32),
                   jax.ShapeDtypeStruct((2 * bs, d), F32)],
        scratch_shapes=[pltpu.VMEM((TM, d), F32), pltpu.VMEM((8, d), F32)],
        compiler_params=_params("arbitrary"),
        name="mixer_b",
    )(x, g, w_in, conv_w, prefix, w_out)


def _mixer_c_kernel(x_ref, g_ref, win_ref, pre_ref, wgrp_ref, scale_ref, wout_ref,
                    o_ref, tail_ref, hs_ref, d_scr, hist_scr, *, npt, tps, bs, ls, pos0):
    i = pl.program_id(0)
    d = x_ref.shape[1]
    gd = d // len(POOL_WINDOWS)
    hp = POOL_CTX + 1
    x = x_ref[...]
    h = _bdot(_rmsnorm(x, g_ref[...]), win_ref[...])

    @pl.when(i == 0)
    def _():
        hist_scr[...] = jnp.zeros_like(hist_scr)

    @pl.when(i < npt)
    def _():
        keep = (i % tps) != 0
        ext = jnp.concatenate([jnp.where(keep, hist_scr[...], 0.0), h], axis=0)
        pos = (i % tps) * TM + lax.broadcasted_iota(jnp.int32, (TM, 1), 0)
        sums = ext
        width = 1
        for g, w in enumerate(POOL_WINDOWS):
            while width < w:
                sums = sums + pltpu.roll(sums, width, 0)
                width *= 2
            cnt = jnp.minimum(w, pos + 1).astype(F32)
            sl = slice(g * gd, (g + 1) * gd)
            d_scr[:, sl] = sums[hp:, sl] / cnt - h[:, sl]
        hist_scr[...] = h[TM - hp:, :]
        tail_ref[...] = h[TM - hp:, :]

    @pl.when(i == npt)
    def _():
        for g, w in enumerate(POOL_WINDOWS):
            sl = slice(g * gd, (g + 1) * gd)
            run = jnp.zeros((bs, gd), F32)
            tails = [run]
            for m in range(1, w):
                run = run + pre_ref[(POOL_CTX - m) * bs:(POOL_CTX - m + 1) * bs, sl]
                tails.append(run)
            for t in range(ls):
                n = min(w, t + 1)
                acc = tails[w - n]
                for j in range(n):
                    acc = acc + h[(t - j) * bs:(t - j + 1) * bs, sl]
                cnt = float(min(w, pos0 + t + 1))
                d_scr[t * bs:(t + 1) * bs, sl] = acc / cnt - h[t * bs:(t + 1) * bs, sl]
        hs_ref[...] = h

    m = jnp.concatenate(
        [_bdot(d_scr[:, g * gd:(g + 1) * gd], wgrp_ref[g]) for g in range(len(POOL_WINDOWS))], axis=1)
    o_ref[...] = x + _bdot(m * scale_ref[...], wout_ref[...])


def _mixer_c(x, g, w_in, prefix, w_grp, scale, w_out, *, npt, tps, bs, ls, pos0):
    t, d = x.shape
    nb = npt // tps
    hp = POOL_CTX + 1
    tile = pl.BlockSpec((TM, d), lambda i: (i, 0))
    return pl.pallas_call(
        functools.partial(_mixer_c_kernel, npt=npt, tps=tps, bs=bs, ls=ls, pos0=pos0),
        grid=(npt + 1,),
        in_specs=[tile, _const((1, d)), _const(w_in.shape), _const(prefix.shape), _const(w_grp.shape),
                  _const((1, d)), _const(w_out.shape)],
        out_specs=[tile,
                   pl.BlockSpec((None, hp, d), lambda i: (jnp.minimum(i // tps, nb - 1), 0, 0)),
                   pl.BlockSpec((TM, d), lambda i: (0, 0))],
        out_shape=[jax.ShapeDtypeStruct((t, d), F32), jax.ShapeDtypeStruct((nb, hp, d), F32),
                   jax.ShapeDtypeStruct((TM, d), F32)],
        scratch_shapes=[pltpu.VMEM((TM, d), F32), pltpu.VMEM((hp, d), F32)],
        compiler_params=_params("arbitrary"),
        name="mixer_c",
    )(x, g, w_in, prefix, w_grp, scale, w_out)


def _softmax_rows(s):
    e = jnp.exp(s - jnp.max(s, axis=-1, keepdims=True))
    return e / jnp.sum(e, axis=-1, keepdims=True)


def _xattn_kernel(x_ref, xs_ref, g_ref, wq_ref, wo_ref, kp_ref, vp_ref, ks_ref, vs_ref,
                  o_ref, q_scr, o_scr, *, npt, bs, ls, spb):
    i = pl.program_id(0)
    d = x_ref.shape[1]
    hd = d // X_HEADS
    qscale = float(hd) ** -0.5
    nt = (((1,), (1,)), ((), ()))

    @pl.when(i == 0)
    def _():
        q_scr[...] = _bdot(_rmsnorm(xs_ref[...], g_ref[...]), wq_ref[...]) * qscale

    @pl.when(i < npt)
    def _():
        x = x_ref[...]
        q = (_bdot(_rmsnorm(x, g_ref[...]), wq_ref[...]) * qscale).astype(BF16)
        k = kp_ref[...]
        v = vp_ref[...]
        heads = []
        for h in range(X_HEADS):
            sl = slice(h * hd, (h + 1) * hd)
            s = lax.dot_general(q[:, sl], k[:, sl], nt, preferred_element_type=F32)
            heads.append(_bdot(_softmax_rows(s), v[:, sl]))
        o_ref[...] = x + _bdot(jnp.concatenate(heads, axis=1), wo_ref[...])

        nm = ks_ref.shape[1]
        rows = lax.broadcasted_iota(jnp.int32, (X_HEADS * ls, nm * X_HEADS), 0)
        cols = lax.broadcasted_iota(jnp.int32, (X_HEADS * ls, nm * X_HEADS), 1)
        own = (rows // ls) == (cols % X_HEADS)
        for bb in range(spb):
            b = i * spb + bb
            qb = jnp.concatenate([q_scr[pl.ds(t * bs + b, 1), :] for t in range(ls)], axis=0)
            qh = jnp.concatenate([qb[:, h * hd:(h + 1) * hd] for h in range(X_HEADS)], axis=0)
            kr = ks_ref[bb].reshape(nm * X_HEADS, hd).astype(BF16)
            vr = vs_ref[bb].reshape(nm * X_HEADS, hd).astype(BF16)
            s = lax.dot_general(qh.astype(BF16), kr, nt, preferred_element_type=F32)
            of = _bdot(_softmax_rows(jnp.where(own, s, -jnp.inf)), vr)
            ob = jnp.concatenate([of[h * ls:(h + 1) * ls] for h in range(X_HEADS)], axis=1)
            for t in range(ls):
                o_scr[pl.ds(t * bs + b, 1), :] = ob[t:t + 1]

    @pl.when(i == npt)
    def _():
        o_ref[...] = x_ref[...] + _bdot(o_scr[...], wo_ref[...])


def _xattn(x, g, wq, wo, kp, vp, ks, vs, layer, *, npt, tps, bs, ls):
    t, d = x.shape
    nb, nm = kp.shape[1], kp.shape[2]
    assert bs % npt == 0
    spb = bs // npt
    tile = pl.BlockSpec((TM, d), lambda i: (i, 0))
    pmem = pl.BlockSpec((None, None, nm, d), lambda i: (layer, jnp.minimum(i // tps, nb - 1), 0, 0))
    smem = pl.BlockSpec((None, spb, nm, X_HEADS, d // X_HEADS),
                        lambda i: (layer, jnp.minimum(i, npt - 1), 0, 0, 0))
    return pl.pallas_call(
        functools.partial(_xattn_kernel, npt=npt, bs=bs, ls=ls, spb=spb),
        grid=(npt + 1,),
        in_specs=[tile, pl.BlockSpec((TM, d), lambda i: (npt, 0), pipeline_mode=pl.Buffered(1)),
                  _const((1, d)), _const(wq.shape), _const(wo.shape), pmem, pmem, smem, smem],
        out_specs=tile,
        out_shape=jax.ShapeDtypeStruct((t, d), F32),
        scratch_shapes=[pltpu.VMEM((TM, d), F32), pltpu.VMEM((TM, d), F32)],
        compiler_params=_params("arbitrary"),
        name="xattn",
    )(x, x, g, wq, wo, kp, vp, ks, vs)


def _ffn_kernel(x_ref, g_ref, w1_ref, w3_ref, w2_ref, o_ref, *, chunks):
    x = x_ref[...]
    xn = _rmsnorm(x, g_ref[...]).astype(BF16)
    acc = x
    for lo, hi in chunks:
        h1 = jnp.dot(xn, w1_ref[:, lo:hi], preferred_element_type=F32)
        h3 = jnp.dot(xn, w3_ref[:, lo:hi], preferred_element_type=F32)
        acc = acc + _bdot(jax.nn.silu(h1) * h3, w2_ref[lo:hi, :])
    o_ref[...] = acc


def _ffn(x, g, w1, w3, w2):
    t, d = x.shape
    dff = w1.shape[1]
    step = 1024
    chunks = tuple((lo, min(lo + step, dff)) for lo in range(0, dff, step))
    tile = pl.BlockSpec((TM, d), lambda i: (i, 0))
    return pl.pallas_call(
        functools.partial(_ffn_kernel, chunks=chunks),
        grid=(t // TM,),
        in_specs=[tile, _const((1, d)), _const(w1.shape), _const(w3.shape), _const(w2.shape)],
        out_specs=tile,
        out_shape=jax.ShapeDtypeStruct((t, d), F32),
        compiler_params=_params("arbitrary"),
        name="ffn",
    )(x, g, w1, w3, w2)


def _router_kernel(x_ref, g_ref, r_ref, tri_ref, xn_ref, meta_ref, cnt_ref, carry_scr, *, n_experts):
    i = pl.program_id(0)

    @pl.when(i == 0)
    def _():
        carry_scr[...] = jnp.zeros_like(carry_scr)

    xn = _rmsnorm(x_ref[...], g_ref[...])
    xn_ref[...] = xn
    xh = xn.astype(BF16)
    xl = (xn - xh.astype(F32)).astype(BF16)
    r = r_ref[...]
    rh = r.astype(BF16)
    rl = (r - rh.astype(F32)).astype(BF16)
    logits = (jnp.dot(xh, rh, preferred_element_type=F32) + jnp.dot(xl, rh, preferred_element_type=F32)
              + jnp.dot(xh, rl, preferred_element_type=F32))
    lane = lax.broadcasted_iota(jnp.int32, logits.shape, 1).astype(F32)
    neg = jnp.float32(-jnp.inf)
    logits = jnp.where(lane < n_experts, logits, neg)
    v1 = jnp.max(logits, axis=-1, keepdims=True)
    i1 = jnp.min(jnp.where(logits == v1, lane, float(LANES)), axis=-1, keepdims=True)
    rest = jnp.where(lane == i1, neg, logits)
    v2 = jnp.max(rest, axis=-1, keepdims=True)
    i2 = jnp.min(jnp.where(rest == v2, lane, float(LANES)), axis=-1, keepdims=True)
    e2 = jnp.exp(v2 - v1)
    den = 1.0 + e2
    hit = jnp.where(lane == i1, 1.0, jnp.where(lane == i2, 1.0, 0.0))
    before = jnp.dot(tri_ref[...], hit.astype(BF16), preferred_element_type=F32) + carry_scr[0:1, :]
    r1 = jnp.sum(jnp.where(lane == i1, before, 0.0), axis=-1, keepdims=True)
    r2 = jnp.sum(jnp.where(lane == i2, before, 0.0), axis=-1, keepdims=True)
    cols = (i1, i2, r1, r2, 1.0 / den, e2 / den)
    meta = jnp.zeros_like(logits)
    for c, val in enumerate(cols):
        meta = jnp.where(lane == c, val, meta)
    meta_ref[...] = meta
    carry_scr[...] = carry_scr[...] + jnp.sum(hit, axis=0, keepdims=True)
    cnt_ref[...] = carry_scr[...]


def _router(x, g, router):
    t, d = x.shape
    ne = router.shape[1]
    assert TOP_K == 2 and ne <= LANES
    rpad = jnp.pad(router, ((0, 0), (0, LANES - ne)))
    tri = jnp.tril(jnp.ones((TM, TM), BF16), -1)
    return pl.pallas_call(
        functools.partial(_router_kernel, n_experts=ne),
        grid=(t // TM,),
        in_specs=[pl.BlockSpec((TM, d), lambda i: (i, 0)), _const((1, d)), _const(rpad.shape),
                  _const(tri.shape)],
        out_specs=[pl.BlockSpec((TM, d), lambda i: (i, 0)), pl.BlockSpec((TM, LANES), lambda i: (i, 0)),
                   pl.BlockSpec((8, LANES), lambda i: (0, 0))],
        out_shape=[jax.ShapeDtypeStruct((t, d), F32), jax.ShapeDtypeStruct((t, LANES), F32),
                   jax.ShapeDtypeStruct((8, LANES), F32)],
        scratch_shapes=[pltpu.VMEM((8, LANES), F32)],
        compiler_params=_params("arbitrary"),
        name="router",
    )(x, g, rpad, tri)


def _dispatch_kernel(dst_ref, xn_ref, xs_ref, z_scr, zsem, sem, *, n_fill):
    i = pl.program_id(0)

    @pl.when(i == 0)
    def _():
        z_scr[...] = jnp.zeros_like(z_scr)
        fills = [pltpu.make_async_copy(z_scr, xs_ref.at[pl.ds(j * MOE_TM, MOE_TM)], zsem)
                 for j in range(n_fill)]
        for c in fills:
            c.start()
        for c in fills:
            c.wait()

    def body(r, carry):
        src = xn_ref.at[pl.ds(r, 1)]
        for k in range(TOP_K):
            pltpu.make_async_copy(src, xs_ref.at[pl.ds(dst_ref[TOP_K * r + k], 1)], sem).start()
        return carry

    lax.fori_loop(0, TM, body, 0, unroll=8)
    for _ in range(TOP_K):
        pltpu.make_async_copy(xn_ref, xs_ref.at[pl.ds(0, TM)], sem).wait()


def _dispatch(dst, xn, n_slot_tiles):
    t, d = xn.shape
    return pl.pallas_call(
        functools.partial(_dispatch_kernel, n_fill=n_slot_tiles),
        grid=(t // TM,),
        in_specs=[pl.BlockSpec((TOP_K * TM,), lambda i: (i,), memory_space=pltpu.SMEM),
                  pl.BlockSpec((TM, d), lambda i: (i, 0))],
        out_specs=pl.BlockSpec(memory_space=pl.ANY),
        out_shape=jax.ShapeDtypeStruct((n_slot_tiles * MOE_TM, d), F32),
        scratch_shapes=[pltpu.VMEM((MOE_TM, d), F32), pltpu.SemaphoreType.DMA(()),
                        pltpu.SemaphoreType.DMA(())],
        compiler_params=_params("arbitrary"),
        name="dispatch",
    )(dst, xn)


def _moe_kernel(te_ref, tf_ref, nt_ref, xs_ref, w1_ref, w3_ref, w2_ref, ys_ref,
                xb_scr, w1_scr, w3_scr, w2_scr):
    i = pl.program_id(0)
    f = pl.program_id(1)

    @pl.when(i < nt_ref[0])
    def _():
        @pl.when(f == 0)
        def _():
            xb_scr[...] = xs_ref[...].astype(BF16)

        @pl.when(tf_ref[i] == 1)
        def _():
            w1_scr[f] = w1_ref[...].astype(BF16)
            w3_scr[f] = w3_ref[...].astype(BF16)
            w2_scr[f] = w2_ref[...].astype(BF16)

        xb = xb_scr[...]
        h1 = jnp.dot(xb, w1_scr[f], preferred_element_type=F32)
        h3 = jnp.dot(xb, w3_scr[f], preferred_element_type=F32)
        y = _bdot(jax.nn.silu(h1) * h3, w2_scr[f])

        @pl.when(f == 0)
        def _():
            ys_ref[...] = y

        @pl.when(f > 0)
        def _():
            ys_ref[...] += y

    @pl.when((i >= nt_ref[0]) & (f == 0))
    def _():
        ys_ref[...] = jnp.zeros_like(ys_ref)


def _moe(te, tf, nt, xs, w1, w3, w2, layer):
    p, d = xs.shape
    _, ne, _, dff = w1.shape
    nf = dff // MOE_TF
    fsel = lambda i, f, tf: jnp.where(tf[i] == 1, f, nf - 1)
    return pl.pallas_call(
        _moe_kernel,
        grid_spec=pltpu.PrefetchScalarGridSpec(
            num_scalar_prefetch=3,
            grid=(p // MOE_TM, nf),
            in_specs=[pl.BlockSpec((MOE_TM, d), lambda i, f, te, tf, nt: (i, 0)),
                      pl.BlockSpec((None, None, d, MOE_TF),
                                   lambda i, f, te, tf, nt: (layer, te[i], 0, fsel(i, f, tf))),
                      pl.BlockSpec((None, None, d, MOE_TF),
                                   lambda i, f, te, tf, nt: (layer, te[i], 0, fsel(i, f, tf))),
                      pl.BlockSpec((None, None, MOE_TF, d),
                                   lambda i, f, te, tf, nt: (layer, te[i], fsel(i, f, tf), 0))],
            out_specs=pl.BlockSpec((MOE_TM, d), lambda i, f, te, tf, nt: (i, 0)),
            scratch_shapes=[pltpu.VMEM((MOE_TM, d), BF16), pltpu.VMEM((nf, d, MOE_TF), BF16),
                            pltpu.VMEM((nf, d, MOE_TF), BF16), pltpu.VMEM((nf, MOE_TF, d), BF16)]),
        out_shape=jax.ShapeDtypeStruct((p, d), F32),
        compiler_params=_params("arbitrary", "arbitrary"),
        name="moe",
    )(te, tf, nt, xs, w1, w3, w2)


def _combine_kernel(dst_ref, x_ref, meta_ref, ys_ref, o_ref, a_scr, b_scr, sem):
    def body(r, carry):
        pltpu.make_async_copy(ys_ref.at[pl.ds(dst_ref[TOP_K * r], 1)], a_scr.at[pl.ds(r, 1)],
                              sem.at[0]).start()
        pltpu.make_async_copy(ys_ref.at[pl.ds(dst_ref[TOP_K * r + 1], 1)], b_scr.at[pl.ds(r, 1)],
                              sem.at[1]).start()
        return carry

    lax.fori_loop(0, TM, body, 0, unroll=8)
    pltpu.make_async_copy(ys_ref.at[pl.ds(0, TM)], a_scr, sem.at[0]).wait()
    pltpu.make_async_copy(ys_ref.at[pl.ds(0, TM)], b_scr, sem.at[1]).wait()
    meta = meta_ref[...]
    o_ref[...] = x_ref[...] + (meta[:, 4:5] * a_scr[...] + meta[:, 5:6] * b_scr[...])


def _combine(dst, x, meta, ys):
    t, d = x.shape
    tile = pl.BlockSpec((TM, d), lambda i: (i, 0))
    return pl.pallas_call(
        _combine_kernel,
        grid=(t // TM,),
        in_specs=[pl.BlockSpec((TOP_K * TM,), lambda i: (i,), memory_space=pltpu.SMEM), tile,
                  pl.BlockSpec((TM, LANES), lambda i: (i, 0)), pl.BlockSpec(memory_space=pl.ANY)],
        out_specs=tile,
        out_shape=jax.ShapeDtypeStruct((t, d), F32),
        scratch_shapes=[pltpu.VMEM((TM, d), F32), pltpu.VMEM((TM, d), F32), pltpu.SemaphoreType.DMA((2,))],
        compiler_params=_params("arbitrary"),
        name="combine",
    )(dst, x, meta, ys)


def _moe_ffn(x, g, router, w1, w3, w2, layer):
    t, d = x.shape
    ne = router.shape[1]
    xn, meta, cnt = _router(x, g, router)
    n = cnt[0, :ne].astype(jnp.int32)
    tiles = (n + MOE_TM - 1) // MOE_TM
    ends = jnp.cumsum(tiles)
    n_tiles = (t * TOP_K) // MOE_TM + ne
    off = (ends - tiles) * MOE_TM
    e = meta[:, :TOP_K].astype(jnp.int32)
    dst = (jnp.take(off, e) + meta[:, TOP_K:2 * TOP_K].astype(jnp.int32)).reshape(-1)
    tid = jnp.minimum(jnp.arange(n_tiles, dtype=jnp.int32), ends[-1] - 1)
    te = jnp.sum((tid[:, None] >= ends[None, :]).astype(jnp.int32), axis=1)
    first = (tid == jnp.take(ends - tiles, te)) & (jnp.arange(n_tiles) < ends[-1])
    xs = _dispatch(dst, xn, n_tiles)
    ys = _moe(te, first.astype(jnp.int32), ends[-1:].astype(jnp.int32), xs, w1, w3, w2, layer)
    return _combine(dst, x, meta, ys)


def _final_kernel(x_ref, g_ref, op_ref, os_ref, *, npt):
    i = pl.program_id(0)
    y = _rmsnorm(x_ref[...], g_ref[...])

    @pl.when(i < npt)
    def _():
        op_ref[...] = y

    @pl.when(i == npt)
    def _():
        os_ref[...] = y


def _final_norm(x, g, *, npt):
    t, d = x.shape
    return pl.pallas_call(
        functools.partial(_final_kernel, npt=npt),
        grid=(npt + 1,),
        in_specs=[pl.BlockSpec((TM, d), lambda i: (i, 0)), _const((1, d))],
        out_specs=[pl.BlockSpec((TM, d), lambda i: (jnp.minimum(i, npt - 1), 0)),
                   pl.BlockSpec((TM, d), lambda i: (0, 0))],
        out_shape=[jax.ShapeDtypeStruct((npt * TM, d), F32), jax.ShapeDtypeStruct((TM, d), F32)],
        compiler_params=_params("arbitrary"),
        name="final_norm",
    )(x, g)


def kernel(x_prompt, x_sample, cache_mem_k, cache_mem_v, state_conv, state_pool, mem_prompt, norm_mix, norm_xa, norm_ffn, norm_mem, norm_final, a_w_in, a_ln_g, a_ln_b, a_w_s, a_b_s, a_w_out, b_w_in, b_conv, b_w_out, c_w_in, c_w_grp, c_scale, c_w_out, xa_wq, xa_wk, xa_wv, xa_wo, f_w1, f_w3, f_w2, m_router, m_w1, m_w3, m_w2):
    nb, seq, d = x_prompt.shape
    bs, ls, _ = x_sample.shape
    depth = norm_mix.shape[0]
    nm = mem_prompt.shape[1]
    assert seq % TM == 0 and TM == bs * ls and (nb * nm) % TM == 0
    tps = seq // TM
    npt = nb * tps
    pos0 = PAST_LEN
    dims = dict(npt=npt, bs=bs, ls=ls)
    bf = lambda w: w.astype(BF16)
    row = lambda v: v.reshape(1, d)
    to_pm = lambda a: a.transpose(1, 0, 2).reshape(-1, d)
    from_pm = lambda a, r: a.reshape(r, bs, d).transpose(1, 0, 2)

    mem_k, mem_v, mem_k5, mem_v5 = _mem_kv(mem_prompt.reshape(nb * nm, d), norm_mem[:, None, :],
                                           bf(xa_wk), bf(xa_wv), nm)
    kp = mem_k.reshape(depth, nb, nm, d)
    vp = mem_v.reshape(depth, nb, nm, d)
    ks, vs = cache_mem_k, cache_mem_v

    x = jnp.concatenate([x_prompt.reshape(nb * seq, d), to_pm(x_sample)], axis=0)
    conv_p, conv_s, pool_p, pool_s, chunk_v = [], [], [], [], []
    for i in range(depth):
        kind, j = i % 3, i // 3
        g = row(norm_mix[i])
        if kind == 0:
            x, v_s = _mixer_a(x, g, bf(a_w_in[j]), row(a_ln_g[j]), row(a_ln_b[j]), a_w_s[j], a_b_s[j],
                              bf(a_w_out[j]), **dims)
            chunk_v.append(from_pm(v_s, ls))
        elif kind == 1:
            x, tail, z_s = _mixer_b(x, g, bf(b_w_in[j]), b_conv[j], to_pm(state_conv[j]), bf(b_w_out[j]),
                                    npt=npt, tps=tps, bs=bs)
            conv_p.append(tail[:, 6:, :])
            conv_s.append(from_pm(z_s, 2))
        else:
            x, tail, h_s = _mixer_c(x, g, bf(c_w_in[j]), to_pm(state_pool[j]), bf(c_w_grp[j]),
                                    row(c_scale[j]), bf(c_w_out[j]), tps=tps, pos0=pos0, **dims)
            pool_p.append(tail[:, 1:, :])
            pool_s.append(jnp.concatenate([state_pool[j], from_pm(h_s, ls)], axis=1)[:, -POOL_CTX:, :])
        x = _xattn(x, row(norm_xa[i]), bf(xa_wq[i]), bf(xa_wo[i]), kp, vp, ks, vs, i, tps=tps, **dims)
        g = row(norm_ffn[i])
        if i % 2 == 0:
            x = _ffn(x, g, bf(f_w1[i // 2]), bf(f_w3[i // 2]), bf(f_w2[i // 2]))
        else:
            x = _moe_ffn(x, g, m_router[i // 2], m_w1, m_w3, m_w2, i // 2)
    y_p, y_s = _final_norm(x, row(norm_final), npt=npt)

    return (y_p.reshape(nb, seq, d), from_pm(y_s, ls), mem_k5, mem_v5,
            jnp.stack(conv_p), jnp.stack(pool_p), jnp.stack(conv_s), jnp.stack(pool_s), jnp.stack(chunk_v))
```

```python
import functools

import jax
import jax.numpy as jnp
from jax import lax
from jax.experimental import pallas as pl
from jax.experimental.pallas import tpu as pltpu

F32 = jnp.float32
BF16 = jnp.bfloat16

EPS = 1e-6
TM = 512
CHUNK = 128
A_GROUPS = 8
POOL_WINDOWS = (2, 4, 8, 16)
POOL_CTX = max(POOL_WINDOWS) - 1
X_HEADS = 4
TOP_K = 2
PAST_LEN = 16384
LANES = 128
MOE_TM = 512
MOE_TF = 512
VMEM_LIMIT = 60 * 1024 * 1024


def _params(*sem):
    return pltpu.CompilerParams(dimension_semantics=sem, vmem_limit_bytes=VMEM_LIMIT)


def _const(shape):
    nd = len(shape)
    return pl.BlockSpec(shape, lambda *_: (0,) * nd, pipeline_mode=pl.Buffered(1))


def _rmsnorm(x, g):
    r = lax.rsqrt(jnp.mean(x * x, axis=-1, keepdims=True) + EPS)
    return (x * r) * g


def _bdot(a, w):
    return jnp.dot(a.astype(BF16), w, preferred_element_type=F32)


def _mem_kv_kernel(m_ref, g_ref, wk_ref, wv_ref, k_ref, v_ref, k5_ref, v5_ref):
    mn = _rmsnorm(m_ref[...], g_ref[...]).astype(BF16)
    nbt, nm, nh, hd = k5_ref.shape
    for w_ref, o_ref, o5_ref in ((wk_ref, k_ref, k5_ref), (wv_ref, v_ref, v5_ref)):
        y = jnp.dot(mn, w_ref[...], preferred_element_type=F32)
        o_ref[...] = y.astype(BF16)
        for b in range(nbt):
            for h in range(nh):
                o5_ref[b, :, h, :] = y[b * nm:(b + 1) * nm, h * hd:(h + 1) * hd]


def _mem_kv(mem, g, wk, wv, nm):
    rows, d = mem.shape
    depth = g.shape[0]
    hd = d // X_HEADS
    out = jax.ShapeDtypeStruct((depth, rows, d), BF16)
    out5 = jax.ShapeDtypeStruct((depth, rows // nm, nm, X_HEADS, hd), F32)
    wspec = pl.BlockSpec((None, d, d), lambda l, r: (l, 0, 0))
    ospec = pl.BlockSpec((None, TM, d), lambda l, r: (l, r, 0))
    o5spec = pl.BlockSpec((None, TM // nm, nm, X_HEADS, hd), lambda l, r: (l, r, 0, 0, 0))
    return pl.pallas_call(
        _mem_kv_kernel,
        grid=(depth, rows // TM),
        in_specs=[pl.BlockSpec((TM, d), lambda l, r: (r, 0)),
                  pl.BlockSpec((None, 1, d), lambda l, r: (l, 0, 0)),
                  wspec, wspec],
        out_specs=[ospec, ospec, o5spec, o5spec],
        out_shape=[out, out, out5, out5],
        compiler_params=_params("arbitrary", "arbitrary"),
        name="mem_kv",
    )(mem, g, wk, wv)


def _mixer_a_kernel(x_ref, g_ref, win_ref, lng_ref, lnb_ref, ws_ref, bs_ref, sc_ref, sb_ref, wout_ref,
                    o_ref, vs_ref, u_scr, v_scr, s_scr, *, npt, bs, ls):
    i = pl.program_id(0)
    d = x_ref.shape[1]
    x = x_ref[...]
    z = jax.nn.gelu(_bdot(_rmsnorm(x, g_ref[...]), win_ref[...]))
    u_scr[...] = z[:, :d]
    v = z[:, d:]
    mu = jnp.mean(v, axis=-1, keepdims=True)
    var = jnp.mean(jnp.square(v - mu), axis=-1, keepdims=True)
    v_scr[...] = ((v - mu) * lax.rsqrt(var + EPS)) * lng_ref[...] + lnb_ref[...]
    gd = d // A_GROUPS

    @pl.when(i < npt)
    def _():
        rows = lax.broadcasted_iota(jnp.int32, (CHUNK, CHUNK), 0)
        cols = lax.broadcasted_iota(jnp.int32, (CHUNK, CHUNK), 1)
        for g in range(A_GROUPS):
            wg = jnp.where(cols <= rows, ws_ref[g], 0.0).astype(BF16)
            bg = bs_ref[g]
            for c in range(TM // CHUNK):
                vc = v_scr[c * CHUNK:(c + 1) * CHUNK, g * gd:(g + 1) * gd]
                s_scr[c * CHUNK:(c + 1) * CHUNK, g * gd:(g + 1) * gd] = (
                    jnp.dot(wg, vc.astype(BF16), preferred_element_type=F32) + bg)

    @pl.when(i == npt)
    def _():
        for t in range(ls):
            acc = jnp.broadcast_to(sb_ref[t:t + 1, :], (bs, d))
            for s in range(t + 1):
                acc = acc + sc_ref[t * ls + s:t * ls + s + 1, :] * v_scr[s * bs:(s + 1) * bs, :]
            s_scr[t * bs:(t + 1) * bs, :] = acc
        vs_ref[...] = v_scr[...]

    o_ref[...] = x + _bdot(u_scr[...] * s_scr[...], wout_ref[...])


def _mixer_a(x, g, w_in, ln_g, ln_b, w_s, b_s, w_out, *, npt, bs, ls):
    t, d = x.shape
    assert TM == bs * ls and TM % CHUNK == 0 and ls <= CHUNK
    gd = d // A_GROUPS
    sc = jnp.repeat(w_s[:, :ls, :ls].transpose(1, 2, 0).reshape(ls * ls, A_GROUPS), gd, axis=1)
    sb = jnp.repeat(b_s[:, :ls].T, gd, axis=1)
    tile = pl.BlockSpec((TM, d), lambda i: (i, 0))
    return pl.pallas_call(
        functools.partial(_mixer_a_kernel, npt=npt, bs=bs, ls=ls),
        grid=(npt + 1,),
        in_specs=[tile, _const((1, d)), _const(w_in.shape), _const((1, d)), _const((1, d)),
                  _const(w_s.shape), _const((A_GROUPS, CHUNK, 1)), _const(sc.shape), _const(sb.shape),
                  _const(w_out.shape)],
        out_specs=[tile, pl.BlockSpec((TM, d), lambda i: (0, 0))],
        out_shape=[jax.ShapeDtypeStruct((t, d), F32), jax.ShapeDtypeStruct((TM, d), F32)],
        scratch_shapes=[pltpu.VMEM((TM, d), F32)] * 3,
        compiler_params=_params("arbitrary"),
        name="mixer_a",
    )(x, g, w_in, ln_g, ln_b, w_s, b_s[:, :, None], sc, sb, w_out)


def _mixer_b_kernel(x_ref, g_ref, win_ref, cw_ref, pre_ref, wout_ref,
                    o_ref, tail_ref, zs_ref, conv_scr, carry_scr, *, npt, tps, bs):
    i = pl.program_id(0)
    d = x_ref.shape[1]
    x = x_ref[...]
    p = _bdot(_rmsnorm(x, g_ref[...]), win_ref[...])
    gate_b = p[:, :d]
    zc = p[:, d:2 * d] * p[:, 2 * d:]
    w0, w1, w2 = cw_ref[0:1, :], cw_ref[1:2, :], cw_ref[2:3, :]

    @pl.when(i == 0)
    def _():
        carry_scr[...] = jnp.zeros_like(carry_scr)

    @pl.when(i < npt)
    def _():
        keep = (i % tps) != 0
        c6 = jnp.where(keep, carry_scr[6:7, :], 0.0)
        c7 = jnp.where(keep, carry_scr[7:8, :], 0.0)
        rows = lax.broadcasted_iota(jnp.int32, (TM, 1), 0)
        sh1 = jnp.where(rows == 0, c7, pltpu.roll(zc, 1, 0))
        sh2 = jnp.where(rows == 0, c6, jnp.where(rows == 1, c7, pltpu.roll(zc, 2, 0)))
        conv_scr[...] = w0 * sh2 + w1 * sh1 + w2 * zc
        carry_scr[...] = zc[TM - 8:, :]
        tail_ref[...] = zc[TM - 8:, :]

    @pl.when(i == npt)
    def _():
        p0, p1 = pre_ref[0:bs, :], pre_ref[bs:2 * bs, :]
        sh1 = jnp.concatenate([p1, zc[:TM - bs]], axis=0)
        sh2 = jnp.concatenate([p0, p1, zc[:TM - 2 * bs]], axis=0)
        conv_scr[...] = w0 * sh2 + w1 * sh1 + w2 * zc
        zs_ref[...] = zc[TM - 2 * bs:, :]

    o_ref[...] = x + _bdot(gate_b * conv_scr[...], wout_ref[...])


def _mixer_b(x, g, w_in, conv_w, prefix, w_out, *, npt, tps, bs):
    t, d = x.shape
    nb = npt // tps
    tile = pl.BlockSpec((TM, d), lambda i: (i, 0))
    return pl.pallas_call(
        functools.partial(_mixer_b_kernel, npt=npt, tps=tps, bs=bs),
        grid=(npt + 1,),
        in_specs=[tile, _const((1, d)), _const(w_in.shape), _const(conv_w.shape), _const(prefix.shape),
                  _const(w_out.shape)],
        out_specs=[tile,
                   pl.BlockSpec((None, 8, d), lambda i: (jnp.minimum(i // tps, nb - 1), 0, 0)),
                   pl.BlockSpec((2 * bs, d), lambda i: (0, 0))],
        out_shape=[jax.ShapeDtypeStruct((t, d), F32), jax.ShapeDtypeStruct((nb, 8, d), F32),
                   jax.ShapeDtypeStruct((2 * bs, d), F32)],
        scratch_shapes=[pltpu.VMEM((TM, d), F32), pltpu.VMEM((8, d), F32)],
        compiler_params=_params("arbitrary"),
        name="mixer_b",
    )(x, g, w_in, conv_w, prefix, w_out)


def _mixer_c_kernel(x_ref, g_ref, win_ref, pre_ref, wgrp_ref, scale_ref, wout_ref,
                    o_ref, tail_ref, hs_ref, d_scr, hist_scr, *, npt, tps, bs, ls, pos0):
    i = pl.program_id(0)
    d = x_ref.shape[1]
    gd = d // len(POOL_WINDOWS)
    hp = POOL_CTX + 1
    x = x_ref[...]
    h = _bdot(_rmsnorm(x, g_ref[...]), win_ref[...])

    @pl.when(i == 0)
    def _():
        hist_scr[...] = jnp.zeros_like(hist_scr)

    @pl.when(i < npt)
    def _():
        keep = (i % tps) != 0
        ext = jnp.concatenate([jnp.where(keep, hist_scr[...], 0.0), h], axis=0)
        pos = (i % tps) * TM + lax.broadcasted_iota(jnp.int32, (TM, 1), 0)
        sums = ext
        width = 1
        for g, w in enumerate(POOL_WINDOWS):
            while width < w:
                sums = sums + pltpu.roll(sums, width, 0)
                width *= 2
            cnt = jnp.minimum(w, pos + 1).astype(F32)
            sl = slice(g * gd, (g + 1) * gd)
            d_scr[:, sl] = sums[hp:, sl] / cnt - h[:, sl]
        hist_scr[...] = h[TM - hp:, :]
        tail_ref[...] = h[TM - hp:, :]

    @pl.when(i == npt)
    def _():
        for g, w in enumerate(POOL_WINDOWS):
            sl = slice(g * gd, (g + 1) * gd)
            run = jnp.zeros((bs, gd), F32)
            tails = [run]
            for m in range(1, w):
                run = run + pre_ref[(POOL_CTX - m) * bs:(POOL_CTX - m + 1) * bs, sl]
                tails.append(run)
            for t in range(ls):
                n = min(w, t + 1)
                acc = tails[w - n]
                for j in range(n):
                    acc = acc + h[(t - j) * bs:(t - j + 1) * bs, sl]
                cnt = float(min(w, pos0 + t + 1))
                d_scr[t * bs:(t + 1) * bs, sl] = acc / cnt - h[t * bs:(t + 1) * bs, sl]
        hs_ref[...] = h

    m = jnp.concatenate(
        [_bdot(d_scr[:, g * gd:(g + 1) * gd], wgrp_ref[g]) for g in range(len(POOL_WINDOWS))], axis=1)
    o_ref[...] = x + _bdot(m * scale_ref[...], wout_ref[...])


def _mixer_c(x, g, w_in, prefix, w_grp, scale, w_out, *, npt, tps, bs, ls, pos0):
    t, d = x.shape
    nb = npt // tps
    hp = POOL_CTX + 1
    tile = pl.BlockSpec((TM, d), lambda i: (i, 0))
    return pl.pallas_call(
        functools.partial(_mixer_c_kernel, npt=npt, tps=tps, bs=bs, ls=ls, pos0=pos0),
        grid=(npt + 1,),
        in_specs=[tile, _const((1, d)), _const(w_in.shape), _const(prefix.shape), _const(w_grp.shape),
                  _const((1, d)), _const(w_out.shape)],
        out_specs=[tile,
                   pl.BlockSpec((None, hp, d), lambda i: (jnp.minimum(i // tps, nb - 1), 0, 0)),
                   pl.BlockSpec((TM, d), lambda i: (0, 0))],
        out_shape=[jax.ShapeDtypeStruct((t, d), F32), jax.ShapeDtypeStruct((nb, hp, d), F32),
                   jax.ShapeDtypeStruct((TM, d), F32)],
        scratch_shapes=[pltpu.VMEM((TM, d), F32), pltpu.VMEM((hp, d), F32)],
        compiler_params=_params("arbitrary"),
        name="mixer_c",
    )(x, g, w_in, prefix, w_grp, scale, w_out)


def _softmax_rows(s):
    e = jnp.exp(s - jnp.max(s, axis=-1, keepdims=True))
    return e / jnp.sum(e, axis=-1, keepdims=True)


def _xattn_kernel(x_ref, xs_ref, g_ref, wq_ref, wo_ref, kp_ref, vp_ref, ks_ref, vs_ref,
                  o_ref, q_scr, o_scr, *, npt, bs, ls, spb):
    i = pl.program_id(0)
    d = x_ref.shape[1]
    hd = d // X_HEADS
    qscale = float(hd) ** -0.5
    nt = (((1,), (1,)), ((), ()))

    @pl.when(i == 0)
    def _():
        q_scr[...] = _bdot(_rmsnorm(xs_ref[...], g_ref[...]), wq_ref[...]) * qscale

    @pl.when(i < npt)
    def _():
        x = x_ref[...]
        q = (_bdot(_rmsnorm(x, g_ref[...]), wq_ref[...]) * qscale).astype(BF16)
        k = kp_ref[...]
        v = vp_ref[...]
        heads = []
        for h in range(X_HEADS):
            sl = slice(h * hd, (h + 1) * hd)
            s = lax.dot_general(q[:, sl], k[:, sl], nt, preferred_element_type=F32)
            heads.append(_bdot(_softmax_rows(s), v[:, sl]))
        o_ref[...] = x + _bdot(jnp.concatenate(heads, axis=1), wo_ref[...])

        nm = ks_ref.shape[1]
        rows = lax.broadcasted_iota(jnp.int32, (X_HEADS * ls, nm * X_HEADS), 0)
        cols = lax.broadcasted_iota(jnp.int32, (X_HEADS * ls, nm * X_HEADS), 1)
        own = (rows // ls) == (cols % X_HEADS)
        for bb in range(spb):
            b = i * spb + bb
            qb = jnp.concatenate([q_scr[pl.ds(t * bs + b, 1), :] for t in range(ls)], axis=0)
            qh = jnp.concatenate([qb[:, h * hd:(h + 1) * hd] for h in range(X_HEADS)], axis=0)
            kr = ks_ref[bb].reshape(nm * X_HEADS, hd).astype(BF16)
            vr = vs_ref[bb].reshape(nm * X_HEADS, hd).astype(BF16)
            s = lax.dot_general(qh.astype(BF16), kr, nt, preferred_element_type=F32)
            of = _bdot(_softmax_rows(jnp.where(own, s, -jnp.inf)), vr)
            ob = jnp.concatenate([of[h * ls:(h + 1) * ls] for h in range(X_HEADS)], axis=1)
            for t in range(ls):
                o_scr[pl.ds(t * bs + b, 1), :] = ob[t:t + 1]

    @pl.when(i == npt)
    def _():
        o_ref[...] = x_ref[...] + _bdot(o_scr[...], wo_ref[...])


def _xattn(x, g, wq, wo, kp, vp, ks, vs, layer, *, npt, tps, bs, ls):
    t, d = x.shape
    nb, nm = kp.shape[1], kp.shape[2]
    assert bs % npt == 0
    spb = bs // npt
    tile = pl.BlockSpec((TM, d), lambda i: (i, 0))
    pmem = pl.BlockSpec((None, None, nm, d), lambda i: (layer, jnp.minimum(i // tps, nb - 1), 0, 0))
    smem = pl.BlockSpec((None, spb, nm, X_HEADS, d // X_HEADS),
                        lambda i: (layer, jnp.minimum(i, npt - 1), 0, 0, 0))
    return pl.pallas_call(
        functools.partial(_xattn_kernel, npt=npt, bs=bs, ls=ls, spb=spb),
        grid=(npt + 1,),
        in_specs=[tile, pl.BlockSpec((TM, d), lambda i: (npt, 0), pipeline_mode=pl.Buffered(1)),
                  _const((1, d)), _const(wq.shape), _const(wo.shape), pmem, pmem, smem, smem],
        out_specs=tile,
        out_shape=jax.ShapeDtypeStruct((t, d), F32),
        scratch_shapes=[pltpu.VMEM((TM, d), F32), pltpu.VMEM((TM, d), F32)],
        compiler_params=_params("arbitrary"),
        name="xattn",
    )(x, x, g, wq, wo, kp, vp, ks, vs)


def _ffn_kernel(x_ref, g_ref, w1_ref, w3_ref, w2_ref, o_ref, *, chunks):
    x = x_ref[...]
    xn = _rmsnorm(x, g_ref[...]).astype(BF16)
    acc = x
    for lo, hi in chunks:
        h1 = jnp.dot(xn, w1_ref[:, lo:hi], preferred_element_type=F32)
        h3 = jnp.dot(xn, w3_ref[:, lo:hi], preferred_element_type=F32)
        acc = acc + _bdot(jax.nn.silu(h1) * h3, w2_ref[lo:hi, :])
    o_ref[...] = acc


def _ffn(x, g, w1, w3, w2):
    t, d = x.shape
    dff = w1.shape[1]
    step = 1024
    chunks = tuple((lo, min(lo + step, dff)) for lo in range(0, dff, step))
    tile = pl.BlockSpec((TM, d), lambda i: (i, 0))
    return pl.pallas_call(
        functools.partial(_ffn_kernel, chunks=chunks),
        grid=(t // TM,),
        in_specs=[tile, _const((1, d)), _const(w1.shape), _const(w3.shape), _const(w2.shape)],
        out_specs=tile,
        out_shape=jax.ShapeDtypeStruct((t, d), F32),
        compiler_params=_params("arbitrary"),
        name="ffn",
    )(x, g, w1, w3, w2)


def _router_kernel(x_ref, g_ref, r_ref, tri_ref, xn_ref, meta_ref, cnt_ref, carry_scr, *, n_experts):
    i = pl.program_id(0)

    @pl.when(i == 0)
    def _():
        carry_scr[...] = jnp.zeros_like(carry_scr)

    xn = _rmsnorm(x_ref[...], g_ref[...])
    xn_ref[...] = xn
    xh = xn.astype(BF16)
    xl = (xn - xh.astype(F32)).astype(BF16)
    r = r_ref[...]
    rh = r.astype(BF16)
    rl = (r - rh.astype(F32)).astype(BF16)
    logits = (jnp.dot(xh, rh, preferred_element_type=F32) + jnp.dot(xl, rh, preferred_element_type=F32)
              + jnp.dot(xh, rl, preferred_element_type=F32))
    lane = lax.broadcasted_iota(jnp.int32, logits.shape, 1).astype(F32)
    neg = jnp.float32(-jnp.inf)
    logits = jnp.where(lane < n_experts, logits, neg)
    v1 = jnp.max(logits, axis=-1, keepdims=True)
    i1 = jnp.min(jnp.where(logits == v1, lane, float(LANES)), axis=-1, keepdims=True)
    rest = jnp.where(lane == i1, neg, logits)
    v2 = jnp.max(rest, axis=-1, keepdims=True)
    i2 = jnp.min(jnp.where(rest == v2, lane, float(LANES)), axis=-1, keepdims=True)
    e2 = jnp.exp(v2 - v1)
    den = 1.0 + e2
    hit = jnp.where(lane == i1, 1.0, jnp.where(lane == i2, 1.0, 0.0))
    before = jnp.dot(tri_ref[...], hit.astype(BF16), preferred_element_type=F32) + carry_scr[0:1, :]
    r1 = jnp.sum(jnp.where(lane == i1, before, 0.0), axis=-1, keepdims=True)
    r2 = jnp.sum(jnp.where(lane == i2, before, 0.0), axis=-1, keepdims=True)
    cols = (i1, i2, r1, r2, 1.0 / den, e2 / den)
    meta = jnp.zeros_like(logits)
    for c, val in enumerate(cols):
        meta = jnp.where(lane == c, val, meta)
    meta_ref[...] = meta
    carry_scr[...] = carry_scr[...] + jnp.sum(hit, axis=0, keepdims=True)
    cnt_ref[...] = carry_scr[...]


def _router(x, g, router):
    t, d = x.shape
    ne = router.shape[1]
    assert TOP_K == 2 and ne <= LANES
    rpad = jnp.pad(router, ((0, 0), (0, LANES - ne)))
    tri = jnp.tril(jnp.ones((TM, TM), BF16), -1)
    return pl.pallas_call(
        functools.partial(_router_kernel, n_experts=ne),
        grid=(t // TM,),
        in_specs=[pl.BlockSpec((TM, d), lambda i: (i, 0)), _const((1, d)), _const(rpad.shape),
                  _const(tri.shape)],
        out_specs=[pl.BlockSpec((TM, d), lambda i: (i, 0)), pl.BlockSpec((TM, LANES), lambda i: (i, 0)),
                   pl.BlockSpec((8, LANES), lambda i: (0, 0))],
        out_shape=[jax.ShapeDtypeStruct((t, d), F32), jax.ShapeDtypeStruct((t, LANES), F32),
                   jax.ShapeDtypeStruct((8, LANES), F32)],
        scratch_shapes=[pltpu.VMEM((8, LANES), F32)],
        compiler_params=_params("arbitrary"),
        name="router",
    )(x, g, rpad, tri)


def _dispatch_kernel(dst_ref, xn_ref, xs_ref, z_scr, r_scr, zsem, sem, *, n_fill):
    i = pl.program_id(0)

    @pl.when(i == 0)
    def _():
        z_scr[...] = jnp.zeros_like(z_scr)
        fills = [pltpu.make_async_copy(z_scr, xs_ref.at[pl.ds(j * MOE_TM, MOE_TM)], zsem)
                 for j in range(n_fill)]
        for c in fills:
            c.start()
        for c in fills:
            c.wait()

    r_scr[:, 0, :] = xn_ref[...]

    def body(r, carry):
        for k in range(TOP_K):
            pltpu.make_async_copy(r_scr.at[r], xs_ref.at[dst_ref[TOP_K * r + k]], sem).start()
        return carry

    lax.fori_loop(0, TM, body, 0, unroll=8)
    for _ in range(TOP_K):
        pltpu.make_async_copy(r_scr, xs_ref.at[pl.ds(0, TM)], sem).wait()


def _dispatch(dst, xn, n_slot_tiles):
    t, d = xn.shape
    return pl.pallas_call(
        functools.partial(_dispatch_kernel, n_fill=n_slot_tiles),
        grid=(t // TM,),
        in_specs=[pl.BlockSpec((TOP_K * TM,), lambda i: (i,), memory_space=pltpu.SMEM),
                  pl.BlockSpec((TM, d), lambda i: (i, 0))],
        out_specs=pl.BlockSpec(memory_space=pl.ANY),
        out_shape=jax.ShapeDtypeStruct((n_slot_tiles * MOE_TM, 1, d), F32),
        scratch_shapes=[pltpu.VMEM((MOE_TM, 1, d), F32), pltpu.VMEM((TM, 1, d), F32),
                        pltpu.SemaphoreType.DMA(()), pltpu.SemaphoreType.DMA(())],
        compiler_params=_params("arbitrary"),
        name="dispatch",
    )(dst, xn)


def _moe_kernel(te_ref, tf_ref, nt_ref, xs_ref, w1_hbm, w3_hbm, w2_hbm, ys_ref,
                s1, s3, s2, w1_scr, w3_scr, w2_scr, sem, *, layer, n_tiles):
    i = pl.program_id(0)
    nf = w1_scr.shape[0]
    valid = i < nt_ref[0]
    first = valid & (tf_ref[i] == 1)
    nxt = jnp.minimum(i + 1, n_tiles - 1)
    next_first = (i + 1 < nt_ref[0]) & (tf_ref[nxt] == 1)

    def chunk(e, f, slot):
        cols = pl.ds(f * MOE_TF, MOE_TF)
        return (pltpu.make_async_copy(w1_hbm.at[layer, e, :, cols], s1.at[slot], sem.at[slot]),
                pltpu.make_async_copy(w3_hbm.at[layer, e, :, cols], s3.at[slot], sem.at[slot]),
                pltpu.make_async_copy(w2_hbm.at[layer, e, cols, :], s2.at[slot], sem.at[slot]))

    def start(e, f, slot):
        for c in chunk(e, f, slot):
            c.start()

    @pl.when(first & (i == 0))
    def _():
        start(te_ref[i], 0, 0)

    @pl.when(valid & jnp.logical_not(first) & next_first)
    def _():
        start(te_ref[nxt], 0, 0)

    @pl.when(valid)
    def _():
        xb = xs_ref[:, 0, :].astype(BF16)
        acc = None
        for f in range(nf):
            slot = f % 2

            @pl.when(first)
            def _():
                for c in chunk(te_ref[i], f, slot):
                    c.wait()
                if f + 1 < nf:
                    start(te_ref[i], f + 1, 1 - slot)
                w1_scr[f] = s1[slot].astype(BF16)
                w3_scr[f] = s3[slot].astype(BF16)
                w2_scr[f] = s2[slot].astype(BF16)

            h1 = jnp.dot(xb, w1_scr[f], preferred_element_type=F32)
            h3 = jnp.dot(xb, w3_scr[f], preferred_element_type=F32)
            y = _bdot(jax.nn.silu(h1) * h3, w2_scr[f])
            acc = y if acc is None else acc + y
        ys_ref[:, 0, :] = acc

    @pl.when(first & next_first)
    def _():
        start(te_ref[nxt], 0, 0)

    @pl.when(jnp.logical_not(valid))
    def _():
        ys_ref[...] = jnp.zeros_like(ys_ref)


def _moe(te, tf, nt, xs, w1, w3, w2, layer):
    p, _, d = xs.shape
    nf = w1.shape[3] // MOE_TF
    n_tiles = p // MOE_TM
    tile = pl.BlockSpec((MOE_TM, 1, d), lambda i, te, tf, nt: (i, 0, 0))
    hbm = pl.BlockSpec(memory_space=pl.ANY)
    return pl.pallas_call(
        functools.partial(_moe_kernel, layer=layer, n_tiles=n_tiles),
        grid_spec=pltpu.PrefetchScalarGridSpec(
            num_scalar_prefetch=3,
            grid=(n_tiles,),
            in_specs=[tile, hbm, hbm, hbm],
            out_specs=tile,
            scratch_shapes=[pltpu.VMEM((2, d, MOE_TF), F32), pltpu.VMEM((2, d, MOE_TF), F32),
                            pltpu.VMEM((2, MOE_TF, d), F32),
                            pltpu.VMEM((nf, d, MOE_TF), BF16), pltpu.VMEM((nf, d, MOE_TF), BF16),
                            pltpu.VMEM((nf, MOE_TF, d), BF16), pltpu.SemaphoreType.DMA((2,))]),
        out_shape=jax.ShapeDtypeStruct((p, 1, d), F32),
        compiler_params=_params("arbitrary"),
        name="moe",
    )(te, tf, nt, xs, w1, w3, w2)


def _combine_kernel(dst_ref, x_ref, meta_ref, ys_ref, o_ref, a_scr, b_scr, sem):
    def body(r, carry):
        pltpu.make_async_copy(ys_ref.at[dst_ref[TOP_K * r]], a_scr.at[r], sem.at[0]).start()
        pltpu.make_async_copy(ys_ref.at[dst_ref[TOP_K * r + 1]], b_scr.at[r], sem.at[1]).start()
        return carry

    lax.fori_loop(0, TM, body, 0, unroll=8)
    pltpu.make_async_copy(ys_ref.at[pl.ds(0, TM)], a_scr, sem.at[0]).wait()
    pltpu.make_async_copy(ys_ref.at[pl.ds(0, TM)], b_scr, sem.at[1]).wait()
    meta = meta_ref[...]
    o_ref[...] = x_ref[...] + (meta[:, 4:5] * a_scr[:, 0, :] + meta[:, 5:6] * b_scr[:, 0, :])


def _combine(dst, x, meta, ys):
    t, d = x.shape
    tile = pl.BlockSpec((TM, d), lambda i: (i, 0))
    return pl.pallas_call(
        _combine_kernel,
        grid=(t // TM,),
        in_specs=[pl.BlockSpec((TOP_K * TM,), lambda i: (i,), memory_space=pltpu.SMEM), tile,
                  pl.BlockSpec((TM, LANES), lambda i: (i, 0)), pl.BlockSpec(memory_space=pl.ANY)],
        out_specs=tile,
        out_shape=jax.ShapeDtypeStruct((t, d), F32),
        scratch_shapes=[pltpu.VMEM((TM, 1, d), F32), pltpu.VMEM((TM, 1, d), F32),
                        pltpu.SemaphoreType.DMA((2,))],
        compiler_params=_params("arbitrary"),
        name="combine",
    )(dst, x, meta, ys)


def _moe_ffn(x, g, router, w1, w3, w2, layer):
    t, d = x.shape
    ne = router.shape[1]
    xn, meta, cnt = _router(x, g, router)
    n = cnt[0, :ne].astype(jnp.int32)
    tiles = (n + MOE_TM - 1) // MOE_TM
    ends = jnp.cumsum(tiles)
    n_tiles = (t * TOP_K) // MOE_TM + ne
    off = (ends - tiles) * MOE_TM
    e = meta[:, :TOP_K].astype(jnp.int32)
    dst = (jnp.take(off, e) + meta[:, TOP_K:2 * TOP_K].astype(jnp.int32)).reshape(-1)
    tid = jnp.minimum(jnp.arange(n_tiles, dtype=jnp.int32), ends[-1] - 1)
    te = jnp.sum((tid[:, None] >= ends[None, :]).astype(jnp.int32), axis=1)
    first = (tid == jnp.take(ends - tiles, te)) & (jnp.arange(n_tiles) < ends[-1])
    xs = _dispatch(dst, xn, n_tiles)
    ys = _moe(te, first.astype(jnp.int32), ends[-1:].astype(jnp.int32), xs, w1, w3, w2, layer)
    return _combine(dst, x, meta, ys)


def _final_kernel(x_ref, g_ref, op_ref, os_ref, *, npt):
    i = pl.program_id(0)
    y = _rmsnorm(x_ref[...], g_ref[...])

    @pl.when(i < npt)
    def _():
        op_ref[...] = y

    @pl.when(i == npt)
    def _():
        os_ref[...] = y


def _final_norm(x, g, *, npt):
    t, d = x.shape
    return pl.pallas_call(
        functools.partial(_final_kernel, npt=npt),
        grid=(npt + 1,),
        in_specs=[pl.BlockSpec((TM, d), lambda i: (i, 0)), _const((1, d))],
        out_specs=[pl.BlockSpec((TM, d), lambda i: (jnp.minimum(i, npt - 1), 0)),
                   pl.BlockSpec((TM, d), lambda i: (0, 0))],
        out_shape=[jax.ShapeDtypeStruct((npt * TM, d), F32), jax.ShapeDtypeStruct((TM, d), F32)],
        compiler_params=_params("arbitrary"),
        name="final_norm",
    )(x, g)


def kernel(x_prompt, x_sample, cache_mem_k, cache_mem_v, state_conv, state_pool, mem_prompt, norm_mix, norm_xa, norm_ffn, norm_mem, norm_final, a_w_in, a_ln_g, a_ln_b, a_w_s, a_b_s, a_w_out, b_w_in, b_conv, b_w_out, c_w_in, c_w_grp, c_scale, c_w_out, xa_wq, xa_wk, xa_wv, xa_wo, f_w1, f_w3, f_w2, m_router, m_w1, m_w3, m_w2):
    nb, seq, d = x_prompt.shape
    bs, ls, _ = x_sample.shape
    depth = norm_mix.shape[0]
    nm = mem_prompt.shape[1]
    assert seq % TM == 0 and TM == bs * ls and (nb * nm) % TM == 0
    tps = seq // TM
    npt = nb * tps
    pos0 = PAST_LEN
    dims = dict(npt=npt, bs=bs, ls=ls)
    bf = lambda w: w.astype(BF16)
    row = lambda v: v.reshape(1, d)
    to_pm = lambda a: a.transpose(1, 0, 2).reshape(-1, d)
    from_pm = lambda a, r: a.reshape(r, bs, d).transpose(1, 0, 2)

    mem_k, mem_v, mem_k5, mem_v5 = _mem_kv(mem_prompt.reshape(nb * nm, d), norm_mem[:, None, :],
                                           bf(xa_wk), bf(xa_wv), nm)
    kp = mem_k.reshape(depth, nb, nm, d)
    vp = mem_v.reshape(depth, nb, nm, d)
    ks, vs = cache_mem_k, cache_mem_v

    x = jnp.concatenate([x_prompt.reshape(nb * seq, d), to_pm(x_sample)], axis=0)
    conv_p, conv_s, pool_p, pool_s, chunk_v = [], [], [], [], []
    for i in range(depth):
        kind, j = i % 3, i // 3
        g = row(norm_mix[i])
        if kind == 0:
            x, v_s = _mixer_a(x, g, bf(a_w_in[j]), row(a_ln_g[j]), row(a_ln_b[j]), a_w_s[j], a_b_s[j],
                              bf(a_w_out[j]), **dims)
            chunk_v.append(from_pm(v_s, ls))
        elif kind == 1:
            x, tail, z_s = _mixer_b(x, g, bf(b_w_in[j]), b_conv[j], to_pm(state_conv[j]), bf(b_w_out[j]),
                                    npt=npt, tps=tps, bs=bs)
            conv_p.append(tail[:, 6:, :])
            conv_s.append(from_pm(z_s, 2))
        else:
            x, tail, h_s = _mixer_c(x, g, bf(c_w_in[j]), to_pm(state_pool[j]), bf(c_w_grp[j]),
                                    row(c_scale[j]), bf(c_w_out[j]), tps=tps, pos0=pos0, **dims)
            pool_p.append(tail[:, 1:, :])
            pool_s.append(jnp.concatenate([state_pool[j], from_pm(h_s, ls)], axis=1)[:, -POOL_CTX:, :])
        x = _xattn(x, row(norm_xa[i]), bf(xa_wq[i]), bf(xa_wo[i]), kp, vp, ks, vs, i, tps=tps, **dims)
        g = row(norm_ffn[i])
        if i % 2 == 0:
            x = _ffn(x, g, bf(f_w1[i // 2]), bf(f_w3[i // 2]), bf(f_w2[i // 2]))
        else:
            x = _moe_ffn(x, g, m_router[i // 2], m_w1, m_w3, m_w2, i // 2)
    y_p, y_s = _final_norm(x, row(norm_final), npt=npt)

    return (y_p.reshape(nb, seq, d), from_pm(y_s, ls), mem_k5, mem_v5,
            jnp.stack(conv_p), jnp.stack(pool_p), jnp.stack(conv_s), jnp.stack(pool_s), jnp.stack(chunk_v))
```

```python
import functools

import jax
import jax.numpy as jnp
from jax import lax
from jax.experimental import pallas as pl
from jax.experimental.pallas import tpu as pltpu

F32 = jnp.float32
BF16 = jnp.bfloat16

EPS = 1e-6
TM = 512
CHUNK = 128
A_GROUPS = 8
POOL_WINDOWS = (2, 4, 8, 16)
POOL_CTX = max(POOL_WINDOWS) - 1
X_HEADS = 4
TOP_K = 2
PAST_LEN = 16384
LANES = 128
MOE_TM = 512
MOE_TF = 512
MOE_CHUNK = 1024
VMEM_LIMIT = 60 * 1024 * 1024


def _params(*sem):
    return pltpu.CompilerParams(dimension_semantics=sem, vmem_limit_bytes=VMEM_LIMIT)


def _const(shape):
    nd = len(shape)
    return pl.BlockSpec(shape, lambda *_: (0,) * nd, pipeline_mode=pl.Buffered(1))


def _rmsnorm(x, g):
    r = lax.rsqrt(jnp.mean(x * x, axis=-1, keepdims=True) + EPS)
    return (x * r) * g


def _bdot(a, w):
    return jnp.dot(a.astype(BF16), w, preferred_element_type=F32)


def _mem_kv_kernel(m_ref, g_ref, wk_ref, wv_ref, k_ref, v_ref, k5_ref, v5_ref):
    mn = _rmsnorm(m_ref[...], g_ref[...]).astype(BF16)
    nbt, nm, nh, hd = k5_ref.shape
    for w_ref, o_ref, o5_ref in ((wk_ref, k_ref, k5_ref), (wv_ref, v_ref, v5_ref)):
        y = jnp.dot(mn, w_ref[...], preferred_element_type=F32)
        o_ref[...] = y.astype(BF16)
        for b in range(nbt):
            for h in range(nh):
                o5_ref[b, :, h, :] = y[b * nm:(b + 1) * nm, h * hd:(h + 1) * hd]


def _mem_kv(mem, g, wk, wv, nm):
    rows, d = mem.shape
    depth = g.shape[0]
    hd = d // X_HEADS
    out = jax.ShapeDtypeStruct((depth, rows, d), BF16)
    out5 = jax.ShapeDtypeStruct((depth, rows // nm, nm, X_HEADS, hd), F32)
    wspec = pl.BlockSpec((None, d, d), lambda l, r: (l, 0, 0))
    ospec = pl.BlockSpec((None, TM, d), lambda l, r: (l, r, 0))
    o5spec = pl.BlockSpec((None, TM // nm, nm, X_HEADS, hd), lambda l, r: (l, r, 0, 0, 0))
    return pl.pallas_call(
        _mem_kv_kernel,
        grid=(depth, rows // TM),
        in_specs=[pl.BlockSpec((TM, d), lambda l, r: (r, 0)),
                  pl.BlockSpec((None, 1, d), lambda l, r: (l, 0, 0)),
                  wspec, wspec],
        out_specs=[ospec, ospec, o5spec, o5spec],
        out_shape=[out, out, out5, out5],
        compiler_params=_params("arbitrary", "arbitrary"),
        name="mem_kv",
    )(mem, g, wk, wv)


def _mixer_a_kernel(x_ref, g_ref, win_ref, lng_ref, lnb_ref, ws_ref, bs_ref, sc_ref, sb_ref, wout_ref,
                    o_ref, vs_ref, u_scr, v_scr, s_scr, *, npt, bs, ls):
    i = pl.program_id(0)
    d = x_ref.shape[1]
    x = x_ref[...]
    z = jax.nn.gelu(_bdot(_rmsnorm(x, g_ref[...]), win_ref[...]))
    u_scr[...] = z[:, :d]
    v = z[:, d:]
    mu = jnp.mean(v, axis=-1, keepdims=True)
    var = jnp.mean(jnp.square(v - mu), axis=-1, keepdims=True)
    v_scr[...] = ((v - mu) * lax.rsqrt(var + EPS)) * lng_ref[...] + lnb_ref[...]
    gd = d // A_GROUPS

    @pl.when(i < npt)
    def _():
        rows = lax.broadcasted_iota(jnp.int32, (CHUNK, CHUNK), 0)
        cols = lax.broadcasted_iota(jnp.int32, (CHUNK, CHUNK), 1)
        for g in range(A_GROUPS):
            wg = jnp.where(cols <= rows, ws_ref[g], 0.0).astype(BF16)
            bg = bs_ref[g]
            for c in range(TM // CHUNK):
                vc = v_scr[c * CHUNK:(c + 1) * CHUNK, g * gd:(g + 1) * gd]
                s_scr[c * CHUNK:(c + 1) * CHUNK, g * gd:(g + 1) * gd] = (
                    jnp.dot(wg, vc.astype(BF16), preferred_element_type=F32) + bg)

    @pl.when(i == npt)
    def _():
        for t in range(ls):
            acc = jnp.broadcast_to(sb_ref[t:t + 1, :], (bs, d))
            for s in range(t + 1):
                acc = acc + sc_ref[t * ls + s:t * ls + s + 1, :] * v_scr[s * bs:(s + 1) * bs, :]
            s_scr[t * bs:(t + 1) * bs, :] = acc
        vs_ref[...] = v_scr[...]

    o_ref[...] = x + _bdot(u_scr[...] * s_scr[...], wout_ref[...])


def _mixer_a(x, g, w_in, ln_g, ln_b, w_s, b_s, w_out, *, npt, bs, ls):
    t, d = x.shape
    assert TM == bs * ls and TM % CHUNK == 0 and ls <= CHUNK
    gd = d // A_GROUPS
    sc = jnp.repeat(w_s[:, :ls, :ls].transpose(1, 2, 0).reshape(ls * ls, A_GROUPS), gd, axis=1)
    sb = jnp.repeat(b_s[:, :ls].T, gd, axis=1)
    tile = pl.BlockSpec((TM, d), lambda i: (i, 0))
    return pl.pallas_call(
        functools.partial(_mixer_a_kernel, npt=npt, bs=bs, ls=ls),
        grid=(npt + 1,),
        in_specs=[tile, _const((1, d)), _const(w_in.shape), _const((1, d)), _const((1, d)),
                  _const(w_s.shape), _const((A_GROUPS, CHUNK, 1)), _const(sc.shape), _const(sb.shape),
                  _const(w_out.shape)],
        out_specs=[tile, pl.BlockSpec((TM, d), lambda i: (0, 0))],
        out_shape=[jax.ShapeDtypeStruct((t, d), F32), jax.ShapeDtypeStruct((TM, d), F32)],
        scratch_shapes=[pltpu.VMEM((TM, d), F32)] * 3,
        compiler_params=_params("arbitrary"),
        name="mixer_a",
    )(x, g, w_in, ln_g, ln_b, w_s, b_s[:, :, None], sc, sb, w_out)


def _mixer_b_kernel(x_ref, g_ref, win_ref, cw_ref, pre_ref, wout_ref,
                    o_ref, tail_ref, zs_ref, conv_scr, carry_scr, *, npt, tps, bs):
    i = pl.program_id(0)
    d = x_ref.shape[1]
    x = x_ref[...]
    p = _bdot(_rmsnorm(x, g_ref[...]), win_ref[...])
    gate_b = p[:, :d]
    zc = p[:, d:2 * d] * p[:, 2 * d:]
    w0, w1, w2 = cw_ref[0:1, :], cw_ref[1:2, :], cw_ref[2:3, :]

    @pl.when(i == 0)
    def _():
        carry_scr[...] = jnp.zeros_like(carry_scr)

    @pl.when(i < npt)
    def _():
        keep = (i % tps) != 0
        c6 = jnp.where(keep, carry_scr[6:7, :], 0.0)
        c7 = jnp.where(keep, carry_scr[7:8, :], 0.0)
        rows = lax.broadcasted_iota(jnp.int32, (TM, 1), 0)
        sh1 = jnp.where(rows == 0, c7, pltpu.roll(zc, 1, 0))
        sh2 = jnp.where(rows == 0, c6, jnp.where(rows == 1, c7, pltpu.roll(zc, 2, 0)))
        conv_scr[...] = w0 * sh2 + w1 * sh1 + w2 * zc
        carry_scr[...] = zc[TM - 8:, :]
        tail_ref[...] = zc[TM - 8:, :]

    @pl.when(i == npt)
    def _():
        p0, p1 = pre_ref[0:bs, :], pre_ref[bs:2 * bs, :]
        sh1 = jnp.concatenate([p1, zc[:TM - bs]], axis=0)
        sh2 = jnp.concatenate([p0, p1, zc[:TM - 2 * bs]], axis=0)
        conv_scr[...] = w0 * sh2 + w1 * sh1 + w2 * zc
        zs_ref[...] = zc[TM - 2 * bs:, :]

    o_ref[...] = x + _bdot(gate_b * conv_scr[...], wout_ref[...])


def _mixer_b(x, g, w_in, conv_w, prefix, w_out, *, npt, tps, bs):
    t, d = x.shape
    nb = npt // tps
    tile = pl.BlockSpec((TM, d), lambda i: (i, 0))
    return pl.pallas_call(
        functools.partial(_mixer_b_kernel, npt=npt, tps=tps, bs=bs),
        grid=(npt + 1,),
        in_specs=[tile, _const((1, d)), _const(w_in.shape), _const(conv_w.shape), _const(prefix.shape),
                  _const(w_out.shape)],
        out_specs=[tile,
                   pl.BlockSpec((None, 8, d), lambda i: (jnp.minimum(i // tps, nb - 1), 0, 0)),
                   pl.BlockSpec((2 * bs, d), lambda i: (0, 0))],
        out_shape=[jax.ShapeDtypeStruct((t, d), F32), jax.ShapeDtypeStruct((nb, 8, d), F32),
                   jax.ShapeDtypeStruct((2 * bs, d), F32)],
        scratch_shapes=[pltpu.VMEM((TM, d), F32), pltpu.VMEM((8, d), F32)],
        compiler_params=_params("arbitrary"),
        name="mixer_b",
    )(x, g, w_in, conv_w, prefix, w_out)


def _mixer_c_kernel(x_ref, g_ref, win_ref, pre_ref, wgrp_ref, scale_ref, wout_ref,
                    o_ref, tail_ref, hs_ref, d_scr, hist_scr, *, npt, tps, bs, ls, pos0):
    i = pl.program_id(0)
    d = x_ref.shape[1]
    gd = d // len(POOL_WINDOWS)
    hp = POOL_CTX + 1
    x = x_ref[...]
    h = _bdot(_rmsnorm(x, g_ref[...]), win_ref[...])

    @pl.when(i == 0)
    def _():
        hist_scr[...] = jnp.zeros_like(hist_scr)

    @pl.when(i < npt)
    def _():
        keep = (i % tps) != 0
        ext = jnp.concatenate([jnp.where(keep, hist_scr[...], 0.0), h], axis=0)
        pos = (i % tps) * TM + lax.broadcasted_iota(jnp.int32, (TM, 1), 0)
        sums = ext
        width = 1
        for g, w in enumerate(POOL_WINDOWS):
            while width < w:
                sums = sums + pltpu.roll(sums, width, 0)
                width *= 2
            cnt = jnp.minimum(w, pos + 1).astype(F32)
            sl = slice(g * gd, (g + 1) * gd)
            d_scr[:, sl] = sums[hp:, sl] / cnt - h[:, sl]
        hist_scr[...] = h[TM - hp:, :]
        tail_ref[...] = h[TM - hp:, :]

    @pl.when(i == npt)
    def _():
        for g, w in enumerate(POOL_WINDOWS):
            sl = slice(g * gd, (g + 1) * gd)
            run = jnp.zeros((bs, gd), F32)
            tails = [run]
            for m in range(1, w):
                run = run + pre_ref[(POOL_CTX - m) * bs:(POOL_CTX - m + 1) * bs, sl]
                tails.append(run)
            for t in range(ls):
                n = min(w, t + 1)
                acc = tails[w - n]
                for j in range(n):
                    acc = acc + h[(t - j) * bs:(t - j + 1) * bs, sl]
                cnt = float(min(w, pos0 + t + 1))
                d_scr[t * bs:(t + 1) * bs, sl] = acc / cnt - h[t * bs:(t + 1) * bs, sl]
        hs_ref[...] = h

    m = jnp.concatenate(
        [_bdot(d_scr[:, g * gd:(g + 1) * gd], wgrp_ref[g]) for g in range(len(POOL_WINDOWS))], axis=1)
    o_ref[...] = x + _bdot(m * scale_ref[...], wout_ref[...])


def _mixer_c(x, g, w_in, prefix, w_grp, scale, w_out, *, npt, tps, bs, ls, pos0):
    t, d = x.shape
    nb = npt // tps
    hp = POOL_CTX + 1
    tile = pl.BlockSpec((TM, d), lambda i: (i, 0))
    return pl.pallas_call(
        functools.partial(_mixer_c_kernel, npt=npt, tps=tps, bs=bs, ls=ls, pos0=pos0),
        grid=(npt + 1,),
        in_specs=[tile, _const((1, d)), _const(w_in.shape), _const(prefix.shape), _const(w_grp.shape),
                  _const((1, d)), _const(w_out.shape)],
        out_specs=[tile,
                   pl.BlockSpec((None, hp, d), lambda i: (jnp.minimum(i // tps, nb - 1), 0, 0)),
                   pl.BlockSpec((TM, d), lambda i: (0, 0))],
        out_shape=[jax.ShapeDtypeStruct((t, d), F32), jax.ShapeDtypeStruct((nb, hp, d), F32),
                   jax.ShapeDtypeStruct((TM, d), F32)],
        scratch_shapes=[pltpu.VMEM((TM, d), F32), pltpu.VMEM((hp, d), F32)],
        compiler_params=_params("arbitrary"),
        name="mixer_c",
    )(x, g, w_in, prefix, w_grp, scale, w_out)


def _softmax_rows(s):
    e = jnp.exp(s - jnp.max(s, axis=-1, keepdims=True))
    return e / jnp.sum(e, axis=-1, keepdims=True)


def _xattn_kernel(x_ref, xs_ref, g_ref, wq_ref, wo_ref, kp_ref, vp_ref, ks_ref, vs_ref,
                  o_ref, q_scr, o_scr, *, npt, bs, ls, spb):
    i = pl.program_id(0)
    d = x_ref.shape[1]
    hd = d // X_HEADS
    qscale = float(hd) ** -0.5
    nt = (((1,), (1,)), ((), ()))

    @pl.when(i == 0)
    def _():
        q_scr[...] = _bdot(_rmsnorm(xs_ref[...], g_ref[...]), wq_ref[...]) * qscale

    @pl.when(i < npt)
    def _():
        x = x_ref[...]
        q = (_bdot(_rmsnorm(x, g_ref[...]), wq_ref[...]) * qscale).astype(BF16)
        k = kp_ref[...]
        v = vp_ref[...]
        heads = []
        for h in range(X_HEADS):
            sl = slice(h * hd, (h + 1) * hd)
            s = lax.dot_general(q[:, sl], k[:, sl], nt, preferred_element_type=F32)
            heads.append(_bdot(_softmax_rows(s), v[:, sl]))
        o_ref[...] = x + _bdot(jnp.concatenate(heads, axis=1), wo_ref[...])

        nm = ks_ref.shape[1]
        rows = lax.broadcasted_iota(jnp.int32, (X_HEADS * ls, nm * X_HEADS), 0)
        cols = lax.broadcasted_iota(jnp.int32, (X_HEADS * ls, nm * X_HEADS), 1)
        own = (rows // ls) == (cols % X_HEADS)
        for bb in range(spb):
            b = i * spb + bb
            qb = jnp.concatenate([q_scr[pl.ds(t * bs + b, 1), :] for t in range(ls)], axis=0)
            qh = jnp.concatenate([qb[:, h * hd:(h + 1) * hd] for h in range(X_HEADS)], axis=0)
            kr = ks_ref[bb].reshape(nm * X_HEADS, hd).astype(BF16)
            vr = vs_ref[bb].reshape(nm * X_HEADS, hd).astype(BF16)
            s = lax.dot_general(qh.astype(BF16), kr, nt, preferred_element_type=F32)
            of = _bdot(_softmax_rows(jnp.where(own, s, -jnp.inf)), vr)
            ob = jnp.concatenate([of[h * ls:(h + 1) * ls] for h in range(X_HEADS)], axis=1)
            for t in range(ls):
                o_scr[pl.ds(t * bs + b, 1), :] = ob[t:t + 1]

    @pl.when(i == npt)
    def _():
        o_ref[...] = x_ref[...] + _bdot(o_scr[...], wo_ref[...])


def _xattn(x, g, wq, wo, kp, vp, ks, vs, layer, *, npt, tps, bs, ls):
    t, d = x.shape
    nb, nm = kp.shape[1], kp.shape[2]
    assert bs % npt == 0
    spb = bs // npt
    tile = pl.BlockSpec((TM, d), lambda i: (i, 0))
    pmem = pl.BlockSpec((None, None, nm, d), lambda i: (layer, jnp.minimum(i // tps, nb - 1), 0, 0))
    smem = pl.BlockSpec((None, spb, nm, X_HEADS, d // X_HEADS),
                        lambda i: (layer, jnp.minimum(i, npt - 1), 0, 0, 0))
    return pl.pallas_call(
        functools.partial(_xattn_kernel, npt=npt, bs=bs, ls=ls, spb=spb),
        grid=(npt + 1,),
        in_specs=[tile, pl.BlockSpec((TM, d), lambda i: (npt, 0), pipeline_mode=pl.Buffered(1)),
                  _const((1, d)), _const(wq.shape), _const(wo.shape), pmem, pmem, smem, smem],
        out_specs=tile,
        out_shape=jax.ShapeDtypeStruct((t, d), F32),
        scratch_shapes=[pltpu.VMEM((TM, d), F32), pltpu.VMEM((TM, d), F32)],
        compiler_params=_params("arbitrary"),
        name="xattn",
    )(x, x, g, wq, wo, kp, vp, ks, vs)


def _ffn_kernel(x_ref, g_ref, w1_ref, w3_ref, w2_ref, o_ref, *, chunks):
    x = x_ref[...]
    xn = _rmsnorm(x, g_ref[...]).astype(BF16)
    acc = x
    for lo, hi in chunks:
        h1 = jnp.dot(xn, w1_ref[:, lo:hi], preferred_element_type=F32)
        h3 = jnp.dot(xn, w3_ref[:, lo:hi], preferred_element_type=F32)
        acc = acc + _bdot(jax.nn.silu(h1) * h3, w2_ref[lo:hi, :])
    o_ref[...] = acc


def _ffn(x, g, w1, w3, w2):
    t, d = x.shape
    dff = w1.shape[1]
    step = 1024
    chunks = tuple((lo, min(lo + step, dff)) for lo in range(0, dff, step))
    tile = pl.BlockSpec((TM, d), lambda i: (i, 0))
    return pl.pallas_call(
        functools.partial(_ffn_kernel, chunks=chunks),
        grid=(t // TM,),
        in_specs=[tile, _const((1, d)), _const(w1.shape), _const(w3.shape), _const(w2.shape)],
        out_specs=tile,
        out_shape=jax.ShapeDtypeStruct((t, d), F32),
        compiler_params=_params("arbitrary"),
        name="ffn",
    )(x, g, w1, w3, w2)


def _router_kernel(x_ref, g_ref, r_ref, tri_ref, xn_ref, meta_ref, cnt_ref, carry_scr, *, n_experts):
    i = pl.program_id(0)

    @pl.when(i == 0)
    def _():
        carry_scr[...] = jnp.zeros_like(carry_scr)

    xn = _rmsnorm(x_ref[...], g_ref[...])
    xn_ref[...] = xn
    xh = xn.astype(BF16)
    xl = (xn - xh.astype(F32)).astype(BF16)
    r = r_ref[...]
    rh = r.astype(BF16)
    rl = (r - rh.astype(F32)).astype(BF16)
    logits = (jnp.dot(xh, rh, preferred_element_type=F32) + jnp.dot(xl, rh, preferred_element_type=F32)
              + jnp.dot(xh, rl, preferred_element_type=F32))
    lane = lax.broadcasted_iota(jnp.int32, logits.shape, 1).astype(F32)
    neg = jnp.float32(-jnp.inf)
    logits = jnp.where(lane < n_experts, logits, neg)
    v1 = jnp.max(logits, axis=-1, keepdims=True)
    i1 = jnp.min(jnp.where(logits == v1, lane, float(LANES)), axis=-1, keepdims=True)
    rest = jnp.where(lane == i1, neg, logits)
    v2 = jnp.max(rest, axis=-1, keepdims=True)
    i2 = jnp.min(jnp.where(rest == v2, lane, float(LANES)), axis=-1, keepdims=True)
    e2 = jnp.exp(v2 - v1)
    den = 1.0 + e2
    hit = jnp.where(lane == i1, 1.0, jnp.where(lane == i2, 1.0, 0.0))
    before = jnp.dot(tri_ref[...], hit.astype(BF16), preferred_element_type=F32) + carry_scr[0:1, :]
    r1 = jnp.sum(jnp.where(lane == i1, before, 0.0), axis=-1, keepdims=True)
    r2 = jnp.sum(jnp.where(lane == i2, before, 0.0), axis=-1, keepdims=True)
    cols = (i1, i2, r1, r2, 1.0 / den, e2 / den)
    meta = jnp.zeros_like(logits)
    for c, val in enumerate(cols):
        meta = jnp.where(lane == c, val, meta)
    meta_ref[...] = meta
    carry_scr[...] = carry_scr[...] + jnp.sum(hit, axis=0, keepdims=True)
    cnt_ref[...] = carry_scr[...]


def _router(x, g, router):
    t, d = x.shape
    ne = router.shape[1]
    assert TOP_K == 2 and ne <= LANES
    rpad = jnp.pad(router, ((0, 0), (0, LANES - ne)))
    tri = jnp.tril(jnp.ones((TM, TM), BF16), -1)
    return pl.pallas_call(
        functools.partial(_router_kernel, n_experts=ne),
        grid=(t // TM,),
        in_specs=[pl.BlockSpec((TM, d), lambda i: (i, 0)), _const((1, d)), _const(rpad.shape),
                  _const(tri.shape)],
        out_specs=[pl.BlockSpec((TM, d), lambda i: (i, 0)), pl.BlockSpec((TM, LANES), lambda i: (i, 0)),
                   pl.BlockSpec((8, LANES), lambda i: (0, 0))],
        out_shape=[jax.ShapeDtypeStruct((t, d), F32), jax.ShapeDtypeStruct((t, LANES), F32),
                   jax.ShapeDtypeStruct((8, LANES), F32)],
        scratch_shapes=[pltpu.VMEM((8, LANES), F32)],
        compiler_params=_params("arbitrary"),
        name="router",
    )(x, g, rpad, tri)


def _dispatch_kernel(dst_ref, xn_ref, xs_ref, z_scr, zsem, sem, *, n_fill):
    i = pl.program_id(0)

    @pl.when(i == 0)
    def _():
        z_scr[...] = jnp.zeros_like(z_scr)
        fills = [pltpu.make_async_copy(z_scr, xs_ref.at[pl.ds(j * MOE_TM, MOE_TM)], zsem)
                 for j in range(n_fill)]
        for c in fills:
            c.start()
        for c in fills:
            c.wait()

    def body(r, carry):
        src = xn_ref.at[pl.ds(r, 1)]
        for k in range(TOP_K):
            pltpu.make_async_copy(src, xs_ref.at[pl.ds(dst_ref[TOP_K * r + k], 1)], sem).start(priority=k)
        return carry

    lax.fori_loop(0, TM, body, 0, unroll=8)
    for _ in range(TOP_K):
        pltpu.make_async_copy(xn_ref, xs_ref.at[pl.ds(0, TM)], sem).wait()


def _dispatch(dst, xn, n_slot_tiles):
    t, d = xn.shape
    return pl.pallas_call(
        functools.partial(_dispatch_kernel, n_fill=n_slot_tiles),
        grid=(t // TM,),
        in_specs=[pl.BlockSpec((TOP_K * TM,), lambda i: (i,), memory_space=pltpu.SMEM),
                  pl.BlockSpec((TM, d), lambda i: (i, 0))],
        out_specs=pl.BlockSpec(memory_space=pl.ANY),
        out_shape=jax.ShapeDtypeStruct((n_slot_tiles * MOE_TM, d), F32),
        scratch_shapes=[pltpu.VMEM((MOE_TM, d), F32), pltpu.SemaphoreType.DMA(()),
                        pltpu.SemaphoreType.DMA(())],
        compiler_params=_params("arbitrary"),
        name="dispatch",
    )(dst, xn)


def _moe_kernel(te_ref, tf_ref, nt_ref, xs_ref, w1_hbm, w3_hbm, w2_hbm, ys_ref,
                s1, s3, s2, w1_scr, w3_scr, w2_scr, sem, *, layer, n_tiles):
    i = pl.program_id(0)
    dff = w1_scr.shape[1]
    nf = dff // MOE_TF
    valid = i < nt_ref[0]
    first = valid & (tf_ref[i] == 1)
    nxt = jnp.minimum(i + 1, n_tiles - 1)
    next_first = (i + 1 < nt_ref[0]) & (tf_ref[nxt] == 1)

    def chunk(e, f, slot):
        cols = pl.ds(f * MOE_TF, MOE_TF)
        return (pltpu.make_async_copy(w1_hbm.at[layer, e, :, cols], s1.at[slot], sem.at[slot]),
                pltpu.make_async_copy(w3_hbm.at[layer, e, :, cols], s3.at[slot], sem.at[slot]),
                pltpu.make_async_copy(w2_hbm.at[layer, e, cols, :], s2.at[slot], sem.at[slot]))

    def start(e, f, slot):
        for c in chunk(e, f, slot):
            c.start()

    def swiglu(xb, lo, hi):
        h1 = jnp.dot(xb, w1_scr[:, lo:hi], preferred_element_type=F32)
        h3 = jnp.dot(xb, w3_scr[:, lo:hi], preferred_element_type=F32)
        return _bdot(jax.nn.silu(h1) * h3, w2_scr[lo:hi, :])

    @pl.when(first & (i == 0))
    def _():
        start(te_ref[i], 0, 0)

    @pl.when(first)
    def _():
        xb = xs_ref[...].astype(BF16)
        acc = None
        for f in range(nf):
            slot = f % 2
            for c in chunk(te_ref[i], f, slot):
                c.wait()
            if f + 1 < nf:
                start(te_ref[i], f + 1, 1 - slot)
            lo, hi = f * MOE_TF, (f + 1) * MOE_TF
            w1_scr[:, lo:hi] = s1[slot].astype(BF16)
            w3_scr[:, lo:hi] = s3[slot].astype(BF16)
            w2_scr[lo:hi, :] = s2[slot].astype(BF16)
            y = swiglu(xb, lo, hi)
            acc = y if acc is None else acc + y
        ys_ref[...] = acc

        @pl.when(next_first)
        def _():
            start(te_ref[nxt], 0, 0)

    @pl.when(valid & jnp.logical_not(first))
    def _():
        @pl.when(next_first)
        def _():
            start(te_ref[nxt], 0, 0)

        xb = xs_ref[...].astype(BF16)
        acc = None
        for lo in range(0, dff, MOE_CHUNK):
            y = swiglu(xb, lo, min(lo + MOE_CHUNK, dff))
            acc = y if acc is None else acc + y
        ys_ref[...] = acc

    @pl.when(jnp.logical_not(valid))
    def _():
        ys_ref[...] = jnp.zeros_like(ys_ref)


def _moe(te, tf, nt, xs, w1, w3, w2, layer):
    p, d = xs.shape
    dff = w1.shape[3]
    n_tiles = p // MOE_TM
    tile = pl.BlockSpec((MOE_TM, d), lambda i, te, tf, nt: (i, 0))
    hbm = pl.BlockSpec(memory_space=pl.ANY)
    return pl.pallas_call(
        functools.partial(_moe_kernel, layer=layer, n_tiles=n_tiles),
        grid_spec=pltpu.PrefetchScalarGridSpec(
            num_scalar_prefetch=3,
            grid=(n_tiles,),
            in_specs=[tile, hbm, hbm, hbm],
            out_specs=tile,
            scratch_shapes=[pltpu.VMEM((2, d, MOE_TF), F32), pltpu.VMEM((2, d, MOE_TF), F32),
                            pltpu.VMEM((2, MOE_TF, d), F32),
                            pltpu.VMEM((d, dff), BF16), pltpu.VMEM((d, dff), BF16),
                            pltpu.VMEM((dff, d), BF16), pltpu.SemaphoreType.DMA((2,))]),
        out_shape=jax.ShapeDtypeStruct((p, d), F32),
        compiler_params=_params("arbitrary"),
        name="moe",
    )(te, tf, nt, xs, w1, w3, w2)


def _combine_kernel(dst_ref, x_ref, meta_ref, ys_ref, o_ref, a_scr, b_scr, sem):
    def body(r, carry):
        pltpu.make_async_copy(ys_ref.at[pl.ds(dst_ref[TOP_K * r], 1)], a_scr.at[pl.ds(r, 1)],
                              sem.at[0]).start(priority=0)
        pltpu.make_async_copy(ys_ref.at[pl.ds(dst_ref[TOP_K * r + 1], 1)], b_scr.at[pl.ds(r, 1)],
                              sem.at[1]).start(priority=1)
        return carry

    lax.fori_loop(0, TM, body, 0, unroll=8)
    pltpu.make_async_copy(ys_ref.at[pl.ds(0, TM)], a_scr, sem.at[0]).wait()
    pltpu.make_async_copy(ys_ref.at[pl.ds(0, TM)], b_scr, sem.at[1]).wait()
    meta = meta_ref[...]
    o_ref[...] = x_ref[...] + (meta[:, 4:5] * a_scr[...] + meta[:, 5:6] * b_scr[...])


def _combine(dst, x, meta, ys):
    t, d = x.shape
    tile = pl.BlockSpec((TM, d), lambda i: (i, 0))
    return pl.pallas_call(
        _combine_kernel,
        grid=(t // TM,),
        in_specs=[pl.BlockSpec((TOP_K * TM,), lambda i: (i,), memory_space=pltpu.SMEM), tile,
                  pl.BlockSpec((TM, LANES), lambda i: (i, 0)), pl.BlockSpec(memory_space=pl.ANY)],
        out_specs=tile,
        out_shape=jax.ShapeDtypeStruct((t, d), F32),
        scratch_shapes=[pltpu.VMEM((TM, d), F32), pltpu.VMEM((TM, d), F32), pltpu.SemaphoreType.DMA((2,))],
        compiler_params=_params("arbitrary"),
        name="combine",
    )(dst, x, meta, ys)


def _moe_ffn(x, g, router, w1, w3, w2, layer):
    t, d = x.shape
    ne = router.shape[1]
    xn, meta, cnt = _router(x, g, router)
    n = cnt[0, :ne].astype(jnp.int32)
    tiles = (n + MOE_TM - 1) // MOE_TM
    ends = jnp.cumsum(tiles)
    n_tiles = (t * TOP_K) // MOE_TM + ne
    off = (ends - tiles) * MOE_TM
    e = meta[:, :TOP_K].astype(jnp.int32)
    dst = (jnp.take(off, e) + meta[:, TOP_K:2 * TOP_K].astype(jnp.int32)).reshape(-1)
    tid = jnp.minimum(jnp.arange(n_tiles, dtype=jnp.int32), ends[-1] - 1)
    te = jnp.sum((tid[:, None] >= ends[None, :]).astype(jnp.int32), axis=1)
    first = (tid == jnp.take(ends - tiles, te)) & (jnp.arange(n_tiles) < ends[-1])
    xs = _dispatch(dst, xn, n_tiles)
    ys = _moe(te, first.astype(jnp.int32), ends[-1:].astype(jnp.int32), xs, w1, w3, w2, layer)
    return _combine(dst, x, meta, ys)


def _final_kernel(x_ref, g_ref, op_ref, os_ref, *, npt):
    i = pl.program_id(0)
    y = _rmsnorm(x_ref[...], g_ref[...])

    @pl.when(i < npt)
    def _():
        op_ref[...] = y

    @pl.when(i == npt)
    def _():
        os_ref[...] = y


def _final_norm(x, g, *, npt):
    t, d = x.shape
    return pl.pallas_call(
        functools.partial(_final_kernel, npt=npt),
        grid=(npt + 1,),
        in_specs=[pl.BlockSpec((TM, d), lambda i: (i, 0)), _const((1, d))],
        out_specs=[pl.BlockSpec((TM, d), lambda i: (jnp.minimum(i, npt - 1), 0)),
                   pl.BlockSpec((TM, d), lambda i: (0, 0))],
        out_shape=[jax.ShapeDtypeStruct((npt * TM, d), F32), jax.ShapeDtypeStruct((TM, d), F32)],
        compiler_params=_params("arbitrary"),
        name="final_norm",
    )(x, g)


def kernel(x_prompt, x_sample, cache_mem_k, cache_mem_v, state_conv, state_pool, mem_prompt, norm_mix, norm_xa, norm_ffn, norm_mem, norm_final, a_w_in, a_ln_g, a_ln_b, a_w_s, a_b_s, a_w_out, b_w_in, b_conv, b_w_out, c_w_in, c_w_grp, c_scale, c_w_out, xa_wq, xa_wk, xa_wv, xa_wo, f_w1, f_w3, f_w2, m_router, m_w1, m_w3, m_w2):
    nb, seq, d = x_prompt.shape
    bs, ls, _ = x_sample.shape
    depth = norm_mix.shape[0]
    nm = mem_prompt.shape[1]
    assert seq % TM == 0 and TM == bs * ls and (nb * nm) % TM == 0
    tps = seq // TM
    npt = nb * tps
    pos0 = PAST_LEN
    dims = dict(npt=npt, bs=bs, ls=ls)
    bf = lambda w: w.astype(BF16)
    row = lambda v: v.reshape(1, d)
    to_pm = lambda a: a.transpose(1, 0, 2).reshape(-1, d)
    from_pm = lambda a, r: a.reshape(r, bs, d).transpose(1, 0, 2)

    mem_k, mem_v, mem_k5, mem_v5 = _mem_kv(mem_prompt.reshape(nb * nm, d), norm_mem[:, None, :],
                                           bf(xa_wk), bf(xa_wv), nm)
    kp = mem_k.reshape(depth, nb, nm, d)
    vp = mem_v.reshape(depth, nb, nm, d)
    ks, vs = cache_mem_k, cache_mem_v

    x = jnp.concatenate([x_prompt.reshape(nb * seq, d), to_pm(x_sample)], axis=0)
    conv_p, conv_s, pool_p, pool_s, chunk_v = [], [], [], [], []
    for i in range(depth):
        kind, j = i % 3, i // 3
        g = row(norm_mix[i])
        if kind == 0:
            x, v_s = _mixer_a(x, g, bf(a_w_in[j]), row(a_ln_g[j]), row(a_ln_b[j]), a_w_s[j], a_b_s[j],
                              bf(a_w_out[j]), **dims)
            chunk_v.append(from_pm(v_s, ls))
        elif kind == 1:
            x, tail, z_s = _mixer_b(x, g, bf(b_w_in[j]), b_conv[j], to_pm(state_conv[j]), bf(b_w_out[j]),
                                    npt=npt, tps=tps, bs=bs)
            conv_p.append(tail[:, 6:, :])
            conv_s.append(from_pm(z_s, 2))
        else:
            x, tail, h_s = _mixer_c(x, g, bf(c_w_in[j]), to_pm(state_pool[j]), bf(c_w_grp[j]),
                                    row(c_scale[j]), bf(c_w_out[j]), tps=tps, pos0=pos0, **dims)
            pool_p.append(tail[:, 1:, :])
            pool_s.append(jnp.concatenate([state_pool[j], from_pm(h_s, ls)], axis=1)[:, -POOL_CTX:, :])
        x = _xattn(x, row(norm_xa[i]), bf(xa_wq[i]), bf(xa_wo[i]), kp, vp, ks, vs, i, tps=tps, **dims)
        g = row(norm_ffn[i])
        if i % 2 == 0:
            x = _ffn(x, g, bf(f_w1[i // 2]), bf(f_w3[i // 2]), bf(f_w2[i // 2]))
        else:
            x = _moe_ffn(x, g, m_router[i // 2], m_w1, m_w3, m_w2, i // 2)
    y_p, y_s = _final_norm(x, row(norm_final), npt=npt)

    return (y_p.reshape(nb, seq, d), from_pm(y_s, ls), mem_k5, mem_v5,
            jnp.stack(conv_p), jnp.stack(pool_p), jnp.stack(conv_s), jnp.stack(pool_s), jnp.stack(chunk_v))
```

```python
import functools

import jax
import jax.numpy as jnp
from jax import lax
from jax.experimental import pallas as pl
from jax.experimental.pallas import tpu as pltpu

F32 = jnp.float32
BF16 = jnp.bfloat16

EPS = 1e-6
TM = 512
CHUNK = 128
A_GROUPS = 8
POOL_WINDOWS = (2, 4, 8, 16)
POOL_CTX = max(POOL_WINDOWS) - 1
X_HEADS = 4
TOP_K = 2
PAST_LEN = 16384
LANES = 128
SUBLANES = 8
DC_TM = 1536
MOE_TM = 512
MOE_TF = 512
MOE_CHUNK = 1024
VMEM_LIMIT = 60 * 1024 * 1024


def _params(*sem):
    return pltpu.CompilerParams(dimension_semantics=sem, vmem_limit_bytes=VMEM_LIMIT)


def _const(shape):
    nd = len(shape)
    return pl.BlockSpec(shape, lambda *_: (0,) * nd, pipeline_mode=pl.Buffered(1))


def _rmsnorm(x, g):
    r = lax.rsqrt(jnp.mean(x * x, axis=-1, keepdims=True) + EPS)
    return (x * r) * g


def _bdot(a, w):
    return jnp.dot(a.astype(BF16), w, preferred_element_type=F32)


def _mem_kv_kernel(m_ref, g_ref, wk_ref, wv_ref, k_ref, v_ref, k5_ref, v5_ref):
    mn = _rmsnorm(m_ref[...], g_ref[...]).astype(BF16)
    nbt, nm, nh, hd = k5_ref.shape
    for w_ref, o_ref, o5_ref in ((wk_ref, k_ref, k5_ref), (wv_ref, v_ref, v5_ref)):
        y = jnp.dot(mn, w_ref[...], preferred_element_type=F32)
        o_ref[...] = y.astype(BF16)
        for b in range(nbt):
            for h in range(nh):
                o5_ref[b, :, h, :] = y[b * nm:(b + 1) * nm, h * hd:(h + 1) * hd]


def _mem_kv(mem, g, wk, wv, nm):
    rows, d = mem.shape
    depth = g.shape[0]
    hd = d // X_HEADS
    out = jax.ShapeDtypeStruct((depth, rows, d), BF16)
    out5 = jax.ShapeDtypeStruct((depth, rows // nm, nm, X_HEADS, hd), F32)
    wspec = pl.BlockSpec((None, d, d), lambda l, r: (l, 0, 0))
    ospec = pl.BlockSpec((None, TM, d), lambda l, r: (l, r, 0))
    o5spec = pl.BlockSpec((None, TM // nm, nm, X_HEADS, hd), lambda l, r: (l, r, 0, 0, 0))
    return pl.pallas_call(
        _mem_kv_kernel,
        grid=(depth, rows // TM),
        in_specs=[pl.BlockSpec((TM, d), lambda l, r: (r, 0)),
                  pl.BlockSpec((None, 1, d), lambda l, r: (l, 0, 0)),
                  wspec, wspec],
        out_specs=[ospec, ospec, o5spec, o5spec],
        out_shape=[out, out, out5, out5],
        compiler_params=_params("arbitrary", "arbitrary"),
        name="mem_kv",
    )(mem, g, wk, wv)


def _mixer_a_kernel(x_ref, xt_ref, g_ref, win_ref, lng_ref, lnb_ref, ws_ref, bs_ref, sc_ref, sb_ref, wout_ref,
                    o_ref, vs_ref, u_scr, v_scr, s_scr, *, npt, bs, ls):
    i = pl.program_id(0)
    d = x_ref.shape[1]
    x = jnp.where(i == npt, xt_ref[...], x_ref[...])
    z = jax.nn.gelu(_bdot(_rmsnorm(x, g_ref[...]), win_ref[...]))
    u_scr[...] = z[:, :d]
    v = z[:, d:]
    mu = jnp.mean(v, axis=-1, keepdims=True)
    var = jnp.mean(jnp.square(v - mu), axis=-1, keepdims=True)
    v_scr[...] = ((v - mu) * lax.rsqrt(var + EPS)) * lng_ref[...] + lnb_ref[...]
    gd = d // A_GROUPS

    @pl.when(i < npt)
    def _():
        rows = lax.broadcasted_iota(jnp.int32, (CHUNK, CHUNK), 0)
        cols = lax.broadcasted_iota(jnp.int32, (CHUNK, CHUNK), 1)
        for g in range(A_GROUPS):
            wg = jnp.where(cols <= rows, ws_ref[g], 0.0).astype(BF16)
            bg = bs_ref[g]
            for c in range(TM // CHUNK):
                vc = v_scr[c * CHUNK:(c + 1) * CHUNK, g * gd:(g + 1) * gd]
                s_scr[c * CHUNK:(c + 1) * CHUNK, g * gd:(g + 1) * gd] = (
                    jnp.dot(wg, vc.astype(BF16), preferred_element_type=F32) + bg)

    @pl.when(i == npt)
    def _():
        for t in range(ls):
            acc = jnp.broadcast_to(sb_ref[t:t + 1, :], (bs, d))
            for s in range(t + 1):
                acc = acc + sc_ref[t * ls + s:t * ls + s + 1, :] * v_scr[s * bs:(s + 1) * bs, :]
            s_scr[t * bs:(t + 1) * bs, :] = acc
        vs_ref[...] = v_scr[...]

    o_ref[...] = x + _bdot(u_scr[...] * s_scr[...], wout_ref[...])


def _mixer_a(x, x_tail, tail_blk, g, w_in, ln_g, ln_b, w_s, b_s, w_out, *, npt, bs, ls):
    d = x.shape[1]
    t = (npt + 1) * TM
    assert TM == bs * ls and TM % CHUNK == 0 and ls <= CHUNK
    gd = d // A_GROUPS
    sc = jnp.repeat(w_s[:, :ls, :ls].transpose(1, 2, 0).reshape(ls * ls, A_GROUPS), gd, axis=1)
    sb = jnp.repeat(b_s[:, :ls].T, gd, axis=1)
    tile = pl.BlockSpec((TM, d), lambda i: (i, 0))
    return pl.pallas_call(
        functools.partial(_mixer_a_kernel, npt=npt, bs=bs, ls=ls),
        grid=(npt + 1,),
        in_specs=[pl.BlockSpec((TM, d), lambda i: (jnp.minimum(i, npt - 1), 0)),
                  pl.BlockSpec((TM, d), lambda i: (tail_blk, 0), pipeline_mode=pl.Buffered(1)),
                  _const((1, d)), _const(w_in.shape), _const((1, d)), _const((1, d)),
                  _const(w_s.shape), _const((A_GROUPS, CHUNK, 1)), _const(sc.shape), _const(sb.shape),
                  _const(w_out.shape)],
        out_specs=[tile, pl.BlockSpec((TM, d), lambda i: (0, 0))],
        out_shape=[jax.ShapeDtypeStruct((t, d), F32), jax.ShapeDtypeStruct((TM, d), F32)],
        scratch_shapes=[pltpu.VMEM((TM, d), F32)] * 3,
        compiler_params=_params("arbitrary"),
        name="mixer_a",
    )(x, x_tail, g, w_in, ln_g, ln_b, w_s, b_s[:, :, None], sc, sb, w_out)


def _mixer_b_kernel(x_ref, g_ref, win_ref, cw_ref, pre_ref, wout_ref,
                    o_ref, tail_ref, zs_ref, conv_scr, carry_scr, *, npt, tps, bs):
    i = pl.program_id(0)
    d = x_ref.shape[1]
    x = x_ref[...]
    p = _bdot(_rmsnorm(x, g_ref[...]), win_ref[...])
    gate_b = p[:, :d]
    zc = p[:, d:2 * d] * p[:, 2 * d:]
    w0, w1, w2 = cw_ref[0:1, :], cw_ref[1:2, :], cw_ref[2:3, :]

    @pl.when(i == 0)
    def _():
        carry_scr[...] = jnp.zeros_like(carry_scr)

    @pl.when(i < npt)
    def _():
        keep = (i % tps) != 0
        c6 = jnp.where(keep, carry_scr[6:7, :], 0.0)
        c7 = jnp.where(keep, carry_scr[7:8, :], 0.0)
        rows = lax.broadcasted_iota(jnp.int32, (TM, 1), 0)
        sh1 = jnp.where(rows == 0, c7, pltpu.roll(zc, 1, 0))
        sh2 = jnp.where(rows == 0, c6, jnp.where(rows == 1, c7, pltpu.roll(zc, 2, 0)))
        conv_scr[...] = w0 * sh2 + w1 * sh1 + w2 * zc
        carry_scr[...] = zc[TM - 8:, :]
        tail_ref[...] = zc[TM - 8:, :]

    @pl.when(i == npt)
    def _():
        p0, p1 = pre_ref[0:bs, :], pre_ref[bs:2 * bs, :]
        sh1 = jnp.concatenate([p1, zc[:TM - bs]], axis=0)
        sh2 = jnp.concatenate([p0, p1, zc[:TM - 2 * bs]], axis=0)
        conv_scr[...] = w0 * sh2 + w1 * sh1 + w2 * zc
        zs_ref[...] = zc[TM - 2 * bs:, :]

    o_ref[...] = x + _bdot(gate_b * conv_scr[...], wout_ref[...])


def _mixer_b(x, g, w_in, conv_w, prefix, w_out, *, npt, tps, bs):
    t, d = x.shape
    nb = npt // tps
    tile = pl.BlockSpec((TM, d), lambda i: (i, 0))
    return pl.pallas_call(
        functools.partial(_mixer_b_kernel, npt=npt, tps=tps, bs=bs),
        grid=(npt + 1,),
        in_specs=[tile, _const((1, d)), _const(w_in.shape), _const(conv_w.shape), _const(prefix.shape),
                  _const(w_out.shape)],
        out_specs=[tile,
                   pl.BlockSpec((None, 8, d), lambda i: (jnp.minimum(i // tps, nb - 1), 0, 0)),
                   pl.BlockSpec((2 * bs, d), lambda i: (0, 0))],
        out_shape=[jax.ShapeDtypeStruct((t, d), F32), jax.ShapeDtypeStruct((nb, 8, d), F32),
                   jax.ShapeDtypeStruct((2 * bs, d), F32)],
        scratch_shapes=[pltpu.VMEM((TM, d), F32), pltpu.VMEM((8, d), F32)],
        compiler_params=_params("arbitrary"),
        name="mixer_b",
    )(x, g, w_in, conv_w, prefix, w_out)


def _mixer_c_kernel(x_ref, g_ref, win_ref, pre_ref, wgrp_ref, scale_ref, wout_ref,
                    o_ref, tail_ref, hs_ref, d_scr, hist_scr, *, npt, tps, bs, ls, pos0):
    i = pl.program_id(0)
    d = x_ref.shape[1]
    gd = d // len(POOL_WINDOWS)
    hp = POOL_CTX + 1
    x = x_ref[...]
    h = _bdot(_rmsnorm(x, g_ref[...]), win_ref[...])

    @pl.when(i == 0)
    def _():
        hist_scr[...] = jnp.zeros_like(hist_scr)

    @pl.when(i < npt)
    def _():
        keep = (i % tps) != 0
        ext = jnp.concatenate([jnp.where(keep, hist_scr[...], 0.0), h], axis=0)
        pos = (i % tps) * TM + lax.broadcasted_iota(jnp.int32, (TM, 1), 0)
        sums = ext
        width = 1
        for g, w in enumerate(POOL_WINDOWS):
            while width < w:
                sums = sums + pltpu.roll(sums, width, 0)
                width *= 2
            cnt = jnp.minimum(w, pos + 1).astype(F32)
            sl = slice(g * gd, (g + 1) * gd)
            d_scr[:, sl] = sums[hp:, sl] / cnt - h[:, sl]
        hist_scr[...] = h[TM - hp:, :]
        tail_ref[...] = h[TM - hp:, :]

    @pl.when(i == npt)
    def _():
        for g, w in enumerate(POOL_WINDOWS):
            sl = slice(g * gd, (g + 1) * gd)
            run = jnp.zeros((bs, gd), F32)
            tails = [run]
            for m in range(1, w):
                run = run + pre_ref[(POOL_CTX - m) * bs:(POOL_CTX - m + 1) * bs, sl]
                tails.append(run)
            for t in range(ls):
                n = min(w, t + 1)
                acc = tails[w - n]
                for j in range(n):
                    acc = acc + h[(t - j) * bs:(t - j + 1) * bs, sl]
                cnt = float(min(w, pos0 + t + 1))
                d_scr[t * bs:(t + 1) * bs, sl] = acc / cnt - h[t * bs:(t + 1) * bs, sl]
        hs_ref[...] = h

    m = jnp.concatenate(
        [_bdot(d_scr[:, g * gd:(g + 1) * gd], wgrp_ref[g]) for g in range(len(POOL_WINDOWS))], axis=1)
    o_ref[...] = x + _bdot(m * scale_ref[...], wout_ref[...])


def _mixer_c(x, g, w_in, prefix, w_grp, scale, w_out, *, npt, tps, bs, ls, pos0):
    t, d = x.shape
    nb = npt // tps
    hp = POOL_CTX + 1
    tile = pl.BlockSpec((TM, d), lambda i: (i, 0))
    return pl.pallas_call(
        functools.partial(_mixer_c_kernel, npt=npt, tps=tps, bs=bs, ls=ls, pos0=pos0),
        grid=(npt + 1,),
        in_specs=[tile, _const((1, d)), _const(w_in.shape), _const(prefix.shape), _const(w_grp.shape),
                  _const((1, d)), _const(w_out.shape)],
        out_specs=[tile,
                   pl.BlockSpec((None, hp, d), lambda i: (jnp.minimum(i // tps, nb - 1), 0, 0)),
                   pl.BlockSpec((TM, d), lambda i: (0, 0))],
        out_shape=[jax.ShapeDtypeStruct((t, d), F32), jax.ShapeDtypeStruct((nb, hp, d), F32),
                   jax.ShapeDtypeStruct((TM, d), F32)],
        scratch_shapes=[pltpu.VMEM((TM, d), F32), pltpu.VMEM((hp, d), F32)],
        compiler_params=_params("arbitrary"),
        name="mixer_c",
    )(x, g, w_in, prefix, w_grp, scale, w_out)


def _softmax_rows(s):
    e = jnp.exp(s - jnp.max(s, axis=-1, keepdims=True))
    return e / jnp.sum(e, axis=-1, keepdims=True)


def _xattn_kernel(x_ref, xs_ref, g_ref, wq_ref, wo_ref, kp_ref, vp_ref, ks_ref, vs_ref,
                  o_ref, q_scr, o_scr, *, npt, bs, ls, spb):
    i = pl.program_id(0)
    d = x_ref.shape[1]
    hd = d // X_HEADS
    qscale = float(hd) ** -0.5
    nt = (((1,), (1,)), ((), ()))

    @pl.when(i == 0)
    def _():
        q_scr[...] = _bdot(_rmsnorm(xs_ref[...], g_ref[...]), wq_ref[...]) * qscale

    @pl.when(i < npt)
    def _():
        x = x_ref[...]
        q = (_bdot(_rmsnorm(x, g_ref[...]), wq_ref[...]) * qscale).astype(BF16)
        k = kp_ref[...]
        v = vp_ref[...]
        heads = []
        for h in range(X_HEADS):
            sl = slice(h * hd, (h + 1) * hd)
            s = lax.dot_general(q[:, sl], k[:, sl], nt, preferred_element_type=F32)
            heads.append(_bdot(_softmax_rows(s), v[:, sl]))
        o_ref[...] = x + _bdot(jnp.concatenate(heads, axis=1), wo_ref[...])

        nm = ks_ref.shape[1]
        rows = lax.broadcasted_iota(jnp.int32, (X_HEADS * ls, nm * X_HEADS), 0)
        cols = lax.broadcasted_iota(jnp.int32, (X_HEADS * ls, nm * X_HEADS), 1)
        own = (rows // ls) == (cols % X_HEADS)
        for bb in range(spb):
            b = i * spb + bb
            qb = jnp.concatenate([q_scr[pl.ds(t * bs + b, 1), :] for t in range(ls)], axis=0)
            qh = jnp.concatenate([qb[:, h * hd:(h + 1) * hd] for h in range(X_HEADS)], axis=0)
            kr = ks_ref[bb].reshape(nm * X_HEADS, hd).astype(BF16)
            vr = vs_ref[bb].reshape(nm * X_HEADS, hd).astype(BF16)
            s = lax.dot_general(qh.astype(BF16), kr, nt, preferred_element_type=F32)
            of = _bdot(_softmax_rows(jnp.where(own, s, -jnp.inf)), vr)
            ob = jnp.concatenate([of[h * ls:(h + 1) * ls] for h in range(X_HEADS)], axis=1)
            for t in range(ls):
                o_scr[pl.ds(t * bs + b, 1), :] = ob[t:t + 1]

    @pl.when(i == npt)
    def _():
        o_ref[...] = x_ref[...] + _bdot(o_scr[...], wo_ref[...])


def _xattn(x, g, wq, wo, kp, vp, ks, vs, layer, *, npt, tps, bs, ls):
    t, d = x.shape
    nb, nm = kp.shape[1], kp.shape[2]
    assert bs % npt == 0
    spb = bs // npt
    tile = pl.BlockSpec((TM, d), lambda i: (i, 0))
    pmem = pl.BlockSpec((None, None, nm, d), lambda i: (layer, jnp.minimum(i // tps, nb - 1), 0, 0))
    smem = pl.BlockSpec((None, spb, nm, X_HEADS, d // X_HEADS),
                        lambda i: (layer, jnp.minimum(i, npt - 1), 0, 0, 0))
    return pl.pallas_call(
        functools.partial(_xattn_kernel, npt=npt, bs=bs, ls=ls, spb=spb),
        grid=(npt + 1,),
        in_specs=[tile, pl.BlockSpec((TM, d), lambda i: (npt, 0), pipeline_mode=pl.Buffered(1)),
                  _const((1, d)), _const(wq.shape), _const(wo.shape), pmem, pmem, smem, smem],
        out_specs=tile,
        out_shape=jax.ShapeDtypeStruct((t, d), F32),
        scratch_shapes=[pltpu.VMEM((TM, d), F32), pltpu.VMEM((TM, d), F32)],
        compiler_params=_params("arbitrary"),
        name="xattn",
    )(x, x, g, wq, wo, kp, vp, ks, vs)


def _ffn_kernel(x_ref, g_ref, w1_ref, w3_ref, w2_ref, o_ref, *, chunks):
    x = x_ref[...]
    xn = _rmsnorm(x, g_ref[...]).astype(BF16)
    acc = x
    for lo, hi in chunks:
        h1 = jnp.dot(xn, w1_ref[:, lo:hi], preferred_element_type=F32)
        h3 = jnp.dot(xn, w3_ref[:, lo:hi], preferred_element_type=F32)
        acc = acc + _bdot(jax.nn.silu(h1) * h3, w2_ref[lo:hi, :])
    o_ref[...] = acc


def _ffn(x, g, w1, w3, w2):
    t, d = x.shape
    dff = w1.shape[1]
    step = 1024
    chunks = tuple((lo, min(lo + step, dff)) for lo in range(0, dff, step))
    tile = pl.BlockSpec((TM, d), lambda i: (i, 0))
    return pl.pallas_call(
        functools.partial(_ffn_kernel, chunks=chunks),
        grid=(t // TM,),
        in_specs=[tile, _const((1, d)), _const(w1.shape), _const(w3.shape), _const(w2.shape)],
        out_specs=tile,
        out_shape=jax.ShapeDtypeStruct((t, d), F32),
        compiler_params=_params("arbitrary"),
        name="ffn",
    )(x, g, w1, w3, w2)


def _router_kernel(x_ref, g_ref, r_ref, tri_ref, xn_ref, meta_ref, cnt_ref, carry_scr, *, n_experts):
    i = pl.program_id(0)

    @pl.when(i == 0)
    def _():
        carry_scr[...] = jnp.zeros_like(carry_scr)

    xn = _rmsnorm(x_ref[...], g_ref[...])
    xn_ref[...] = xn
    xh = xn.astype(BF16)
    xl = (xn - xh.astype(F32)).astype(BF16)
    r = r_ref[...]
    rh = r.astype(BF16)
    rl = (r - rh.astype(F32)).astype(BF16)
    logits = (jnp.dot(xh, rh, preferred_element_type=F32) + jnp.dot(xl, rh, preferred_element_type=F32)
              + jnp.dot(xh, rl, preferred_element_type=F32))
    lane = lax.broadcasted_iota(jnp.int32, logits.shape, 1).astype(F32)
    neg = jnp.float32(-jnp.inf)
    logits = jnp.where(lane < n_experts, logits, neg)
    v1 = jnp.max(logits, axis=-1, keepdims=True)
    i1 = jnp.min(jnp.where(logits == v1, lane, float(LANES)), axis=-1, keepdims=True)
    rest = jnp.where(lane == i1, neg, logits)
    v2 = jnp.max(rest, axis=-1, keepdims=True)
    i2 = jnp.min(jnp.where(rest == v2, lane, float(LANES)), axis=-1, keepdims=True)
    e2 = jnp.exp(v2 - v1)
    den = 1.0 + e2
    hit = jnp.where(lane == i1, 1.0, jnp.where(lane == i2, 1.0, 0.0))
    before = jnp.dot(tri_ref[...], hit.astype(BF16), preferred_element_type=F32) + carry_scr[0:1, :]
    r1 = jnp.sum(jnp.where(lane == i1, before, 0.0), axis=-1, keepdims=True)
    r2 = jnp.sum(jnp.where(lane == i2, before, 0.0), axis=-1, keepdims=True)
    cols = (i1, i2, r1, r2, 1.0 / den, e2 / den)
    meta = jnp.zeros_like(logits)
    for c, val in enumerate(cols):
        meta = jnp.where(lane == c, val, meta)
    meta_ref[...] = meta
    carry_scr[...] = carry_scr[...] + jnp.sum(hit, axis=0, keepdims=True)
    cnt_ref[...] = carry_scr[...]


def _router(x, g, router):
    t, d = x.shape
    ne = router.shape[1]
    assert TOP_K == 2 and ne <= LANES
    rpad = jnp.pad(router, ((0, 0), (0, LANES - ne)))
    tri = jnp.tril(jnp.ones((TM, TM), BF16), -1)
    return pl.pallas_call(
        functools.partial(_router_kernel, n_experts=ne),
        grid=(t // TM,),
        in_specs=[pl.BlockSpec((TM, d), lambda i: (i, 0)), _const((1, d)), _const(rpad.shape),
                  _const(tri.shape)],
        out_specs=[pl.BlockSpec((TM, d), lambda i: (i, 0)), pl.BlockSpec((TM, LANES), lambda i: (i, 0)),
                   pl.BlockSpec((8, LANES), lambda i: (0, 0))],
        out_shape=[jax.ShapeDtypeStruct((t, d), F32), jax.ShapeDtypeStruct((t, LANES), F32),
                   jax.ShapeDtypeStruct((8, LANES), F32)],
        scratch_shapes=[pltpu.VMEM((8, LANES), F32)],
        compiler_params=_params("arbitrary"),
        name="router",
    )(x, g, rpad, tri)


def _dispatch_kernel(dst_ref, fill_ref, nt_ref, xn_ref, xs_ref, z_scr, zsem, sem, *, min_tiles):
    i = pl.program_id(0)
    rows = xn_ref.shape[0]
    last = xs_ref.shape[0] // MOE_TM - 1

    @pl.when(i == 0)
    def _():
        z_scr[...] = jnp.zeros_like(z_scr)
        fills = [pltpu.make_async_copy(
            z_scr, xs_ref.at[pl.ds(pl.multiple_of(fill_ref[e], SUBLANES), MOE_TM)], zsem)
            for e in range(fill_ref.shape[0])]
        for c in fills:
            c.start()
        for c in fills:
            c.wait()
        for j in range(last + 1 - min_tiles):
            tile = nt_ref[0] + j

            @pl.when(tile <= last)
            def _():
                c = pltpu.make_async_copy(
                    z_scr, xs_ref.at[pl.ds(pl.multiple_of(tile * MOE_TM, MOE_TM), MOE_TM)], zsem)
                c.start()
                c.wait()

    def body(r, carry):
        src = xn_ref.at[pl.ds(r, 1)]
        for k in range(TOP_K):
            pltpu.make_async_copy(src, xs_ref.at[pl.ds(dst_ref[TOP_K * r + k], 1)], sem).start(priority=k)
        return carry

    lax.fori_loop(0, rows, body, 0, unroll=8)
    for _ in range(TOP_K):
        pltpu.make_async_copy(xn_ref, xs_ref.at[pl.ds(0, rows)], sem).wait()


def _dispatch(dst, fill, nt, xn, n_slot_tiles):
    t, d = xn.shape
    return pl.pallas_call(
        functools.partial(_dispatch_kernel, min_tiles=(t * TOP_K) // MOE_TM),
        grid=(t // DC_TM,),
        in_specs=[pl.BlockSpec((TOP_K * DC_TM,), lambda i: (i,), memory_space=pltpu.SMEM),
                  pl.BlockSpec(memory_space=pltpu.SMEM), pl.BlockSpec(memory_space=pltpu.SMEM),
                  pl.BlockSpec((DC_TM, d), lambda i: (i, 0))],
        out_specs=pl.BlockSpec(memory_space=pl.ANY),
        out_shape=jax.ShapeDtypeStruct(((n_slot_tiles + 1) * MOE_TM, d), F32),
        scratch_shapes=[pltpu.VMEM((MOE_TM, d), F32), pltpu.SemaphoreType.DMA(()),
                        pltpu.SemaphoreType.DMA(())],
        compiler_params=_params("arbitrary"),
        name="dispatch",
    )(dst, fill, nt, xn)


def _moe_kernel(te_ref, tf_ref, nt_ref, xs_ref, w1_hbm, w3_hbm, w2_hbm, ys_ref,
                s1, s3, s2, w1_scr, w3_scr, w2_scr, sem, *, layer, n_tiles):
    i = pl.program_id(0)
    dff = w1_scr.shape[1]
    nf = dff // MOE_TF
    valid = i < nt_ref[0]
    first = valid & (tf_ref[i] == 1)
    nxt = jnp.minimum(i + 1, n_tiles - 1)
    next_first = (i + 1 < nt_ref[0]) & (tf_ref[nxt] == 1)

    def chunk(e, f, slot):
        cols = pl.ds(f * MOE_TF, MOE_TF)
        return (pltpu.make_async_copy(w1_hbm.at[layer, e, :, cols], s1.at[slot], sem.at[slot]),
                pltpu.make_async_copy(w3_hbm.at[layer, e, :, cols], s3.at[slot], sem.at[slot]),
                pltpu.make_async_copy(w2_hbm.at[layer, e, cols, :], s2.at[slot], sem.at[slot]))

    def start(e, f, slot):
        for c in chunk(e, f, slot):
            c.start()

    def swiglu(xb, lo, hi):
        h1 = jnp.dot(xb, w1_scr[:, lo:hi], preferred_element_type=F32)
        h3 = jnp.dot(xb, w3_scr[:, lo:hi], preferred_element_type=F32)
        return _bdot(jax.nn.silu(h1) * h3, w2_scr[lo:hi, :])

    @pl.when(first & (i == 0))
    def _():
        start(te_ref[i], 0, 0)

    @pl.when(first)
    def _():
        xb = xs_ref[...].astype(BF16)
        acc = None
        for f in range(nf):
            slot = f % 2
            for c in chunk(te_ref[i], f, slot):
                c.wait()
            if f + 1 < nf:
                start(te_ref[i], f + 1, 1 - slot)
            lo, hi = f * MOE_TF, (f + 1) * MOE_TF
            w1_scr[:, lo:hi] = s1[slot].astype(BF16)
            w3_scr[:, lo:hi] = s3[slot].astype(BF16)
            w2_scr[lo:hi, :] = s2[slot].astype(BF16)
            y = swiglu(xb, lo, hi)
            acc = y if acc is None else acc + y
        ys_ref[...] = acc

        @pl.when(next_first)
        def _():
            start(te_ref[nxt], 0, 0)

    @pl.when(valid & jnp.logical_not(first))
    def _():
        @pl.when(next_first)
        def _():
            start(te_ref[nxt], 0, 0)

        xb = xs_ref[...].astype(BF16)
        acc = None
        for lo in range(0, dff, MOE_CHUNK):
            y = swiglu(xb, lo, min(lo + MOE_CHUNK, dff))
            acc = y if acc is None else acc + y
        ys_ref[...] = acc

    @pl.when(jnp.logical_not(valid))
    def _():
        ys_ref[...] = jnp.zeros_like(ys_ref)


def _moe(te, tf, nt, xs, w1, w3, w2, layer, n_tiles):
    d = xs.shape[1]
    dff = w1.shape[3]
    tile = pl.BlockSpec((MOE_TM, d), lambda i, te, tf, nt: (i, 0))
    used = pl.BlockSpec((MOE_TM, d), lambda i, te, tf, nt: (jnp.minimum(i, nt[0] - 1), 0))
    hbm = pl.BlockSpec(memory_space=pl.ANY)
    return pl.pallas_call(
        functools.partial(_moe_kernel, layer=layer, n_tiles=n_tiles),
        grid_spec=pltpu.PrefetchScalarGridSpec(
            num_scalar_prefetch=3,
            grid=(n_tiles,),
            in_specs=[used, hbm, hbm, hbm],
            out_specs=tile,
            scratch_shapes=[pltpu.VMEM((2, d, MOE_TF), F32), pltpu.VMEM((2, d, MOE_TF), F32),
                            pltpu.VMEM((2, MOE_TF, d), F32),
                            pltpu.VMEM((d, dff), BF16), pltpu.VMEM((d, dff), BF16),
                            pltpu.VMEM((dff, d), BF16), pltpu.SemaphoreType.DMA((2,))]),
        out_shape=jax.ShapeDtypeStruct((n_tiles * MOE_TM, d), F32),
        compiler_params=_params("arbitrary"),
        name="moe",
    )(te, tf, nt, xs, w1, w3, w2)


def _combine_kernel(dst_ref, x_ref, meta_ref, ys_ref, o_ref, a_scr, b_scr, sem):
    def body(r, carry):
        pltpu.make_async_copy(ys_ref.at[pl.ds(dst_ref[TOP_K * r], 1)], a_scr.at[pl.ds(r, 1)],
                              sem.at[0]).start(priority=0)
        pltpu.make_async_copy(ys_ref.at[pl.ds(dst_ref[TOP_K * r + 1], 1)], b_scr.at[pl.ds(r, 1)],
                              sem.at[1]).start(priority=1)
        return carry

    rows = x_ref.shape[0]
    lax.fori_loop(0, rows, body, 0, unroll=8)
    pltpu.make_async_copy(ys_ref.at[pl.ds(0, rows)], a_scr, sem.at[0]).wait()
    pltpu.make_async_copy(ys_ref.at[pl.ds(0, rows)], b_scr, sem.at[1]).wait()
    meta = meta_ref[...]
    o_ref[...] = x_ref[...] + (meta[:, 4:5] * a_scr[...] + meta[:, 5:6] * b_scr[...])


def _combine(dst, x, meta, ys):
    t, d = x.shape
    tile = pl.BlockSpec((DC_TM, d), lambda i: (i, 0))
    return pl.pallas_call(
        _combine_kernel,
        grid=(t // DC_TM,),
        in_specs=[pl.BlockSpec((TOP_K * DC_TM,), lambda i: (i,), memory_space=pltpu.SMEM), tile,
                  pl.BlockSpec((DC_TM, LANES), lambda i: (i, 0)), pl.BlockSpec(memory_space=pl.ANY)],
        out_specs=tile,
        out_shape=jax.ShapeDtypeStruct((t, d), F32),
        scratch_shapes=[pltpu.VMEM((DC_TM, d), F32), pltpu.VMEM((DC_TM, d), F32),
                        pltpu.SemaphoreType.DMA((2,))],
        compiler_params=_params("arbitrary"),
        name="combine",
    )(dst, x, meta, ys)


def _moe_ffn(x, g, router, w1, w3, w2, layer):
    t, d = x.shape
    ne = router.shape[1]
    assert t % DC_TM == 0 and (TOP_K * DC_TM) % 1024 == 0
    xn, meta, cnt = _router(x, g, router)
    n = cnt[0, :ne].astype(jnp.int32)
    tiles = (n + MOE_TM - 1) // MOE_TM
    ends = jnp.cumsum(tiles)
    n_tiles = (t * TOP_K) // MOE_TM + ne
    off = (ends - tiles) * MOE_TM
    e = meta[:, :TOP_K].astype(jnp.int32)
    dst = (jnp.take(off, e) + meta[:, TOP_K:2 * TOP_K].astype(jnp.int32)).reshape(-1)
    tid = jnp.minimum(jnp.arange(n_tiles, dtype=jnp.int32), ends[-1] - 1)
    te = jnp.sum((tid[:, None] >= ends[None, :]).astype(jnp.int32), axis=1)
    first = (tid == jnp.take(ends - tiles, te)) & (jnp.arange(n_tiles) < ends[-1])
    fill = (off + n) // SUBLANES * SUBLANES
    nt = ends[-1:].astype(jnp.int32)
    xs = _dispatch(dst, fill, nt, xn, n_tiles)
    ys = _moe(te, first.astype(jnp.int32), nt, xs, w1, w3, w2, layer, n_tiles)
    return _combine(dst, x, meta, ys)


def _final_kernel(x_ref, g_ref, op_ref, os_ref, *, npt):
    i = pl.program_id(0)
    y = _rmsnorm(x_ref[...], g_ref[...])

    @pl.when(i < npt)
    def _():
        op_ref[...] = y

    @pl.when(i == npt)
    def _():
        os_ref[...] = y


def _final_norm(x, g, *, npt):
    t, d = x.shape
    return pl.pallas_call(
        functools.partial(_final_kernel, npt=npt),
        grid=(npt + 1,),
        in_specs=[pl.BlockSpec((TM, d), lambda i: (i, 0)), _const((1, d))],
        out_specs=[pl.BlockSpec((TM, d), lambda i: (jnp.minimum(i, npt - 1), 0)),
                   pl.BlockSpec((TM, d), lambda i: (0, 0))],
        out_shape=[jax.ShapeDtypeStruct((npt * TM, d), F32), jax.ShapeDtypeStruct((TM, d), F32)],
        compiler_params=_params("arbitrary"),
        name="final_norm",
    )(x, g)


def kernel(x_prompt, x_sample, cache_mem_k, cache_mem_v, state_conv, state_pool, mem_prompt, norm_mix, norm_xa, norm_ffn, norm_mem, norm_final, a_w_in, a_ln_g, a_ln_b, a_w_s, a_b_s, a_w_out, b_w_in, b_conv, b_w_out, c_w_in, c_w_grp, c_scale, c_w_out, xa_wq, xa_wk, xa_wv, xa_wo, f_w1, f_w3, f_w2, m_router, m_w1, m_w3, m_w2):
    nb, seq, d = x_prompt.shape
    bs, ls, _ = x_sample.shape
    depth = norm_mix.shape[0]
    nm = mem_prompt.shape[1]
    assert seq % TM == 0 and TM == bs * ls and (nb * nm) % TM == 0
    tps = seq // TM
    npt = nb * tps
    pos0 = PAST_LEN
    dims = dict(npt=npt, bs=bs, ls=ls)
    bf = lambda w: w.astype(BF16)
    row = lambda v: v.reshape(1, d)
    to_pm = lambda a: a.transpose(1, 0, 2).reshape(-1, d)
    from_pm = lambda a, r: a.reshape(r, bs, d).transpose(1, 0, 2)

    mem_k, mem_v, mem_k5, mem_v5 = _mem_kv(mem_prompt.reshape(nb * nm, d), norm_mem[:, None, :],
                                           bf(xa_wk), bf(xa_wv), nm)
    kp = mem_k.reshape(depth, nb, nm, d)
    vp = mem_v.reshape(depth, nb, nm, d)
    ks, vs = cache_mem_k, cache_mem_v

    x = None
    conv_p, conv_s, pool_p, pool_s, chunk_v = [], [], [], [], []
    for i in range(depth):
        kind, j = i % 3, i // 3
        g = row(norm_mix[i])
        if kind == 0:
            srcs = (x_prompt.reshape(nb * seq, d), to_pm(x_sample), 0) if x is None else (x, x, npt)
            x, v_s = _mixer_a(*srcs, g, bf(a_w_in[j]), row(a_ln_g[j]), row(a_ln_b[j]), a_w_s[j], a_b_s[j],
                              bf(a_w_out[j]), **dims)
            chunk_v.append(from_pm(v_s, ls))
        elif kind == 1:
            x, tail, z_s = _mixer_b(x, g, bf(b_w_in[j]), b_conv[j], to_pm(state_conv[j]), bf(b_w_out[j]),
                                    npt=npt, tps=tps, bs=bs)
            conv_p.append(tail[:, 6:, :])
            conv_s.append(from_pm(z_s, 2))
        else:
            x, tail, h_s = _mixer_c(x, g, bf(c_w_in[j]), to_pm(state_pool[j]), bf(c_w_grp[j]),
                                    row(c_scale[j]), bf(c_w_out[j]), tps=tps, pos0=pos0, **dims)
            pool_p.append(tail[:, 1:, :])
            pool_s.append(jnp.concatenate([state_pool[j], from_pm(h_s, ls)], axis=1)[:, -POOL_CTX:, :])
        x = _xattn(x, row(norm_xa[i]), bf(xa_wq[i]), bf(xa_wo[i]), kp, vp, ks, vs, i, tps=tps, **dims)
        g = row(norm_ffn[i])
        if i % 2 == 0:
            x = _ffn(x, g, bf(f_w1[i // 2]), bf(f_w3[i // 2]), bf(f_w2[i // 2]))
        else:
            x = _moe_ffn(x, g, m_router[i // 2], m_w1, m_w3, m_w2, i // 2)
    y_p, y_s = _final_norm(x, row(norm_final), npt=npt)

    return (y_p.reshape(nb, seq, d), from_pm(y_s, ls), mem_k5, mem_v5,
            jnp.stack(conv_p), jnp.stack(pool_p), jnp.stack(conv_s), jnp.stack(pool_s), jnp.stack(chunk_v))
```

```python
import functools

import jax
import jax.numpy as jnp
from jax import lax
from jax.experimental import pallas as pl
from jax.experimental.pallas import tpu as pltpu

F32 = jnp.float32
BF16 = jnp.bfloat16

EPS = 1e-6
TM = 512
CHUNK = 128
A_GROUPS = 8
POOL_WINDOWS = (2, 4, 8, 16)
POOL_CTX = max(POOL_WINDOWS) - 1
X_HEADS = 4
TOP_K = 2
PAST_LEN = 16384
LANES = 128
DC_TM = 1536
MOE_TM = 512
MOE_TF = 512
MOE_CHUNK = 1024
VMEM_LIMIT = 60 * 1024 * 1024


def _params(*sem):
    return pltpu.CompilerParams(dimension_semantics=sem, vmem_limit_bytes=VMEM_LIMIT)


def _const(shape):
    nd = len(shape)
    return pl.BlockSpec(shape, lambda *_: (0,) * nd, pipeline_mode=pl.Buffered(1))


def _rmsnorm(x, g):
    r = lax.rsqrt(jnp.mean(x * x, axis=-1, keepdims=True) + EPS)
    return (x * r) * g


def _bdot(a, w):
    return jnp.dot(a.astype(BF16), w, preferred_element_type=F32)


def _mem_kv_kernel(m_ref, g_ref, wk_ref, wv_ref, k_ref, v_ref, k5_ref, v5_ref):
    mn = _rmsnorm(m_ref[...], g_ref[...]).astype(BF16)
    nbt, nm, nh, hd = k5_ref.shape
    for w_ref, o_ref, o5_ref in ((wk_ref, k_ref, k5_ref), (wv_ref, v_ref, v5_ref)):
        y = jnp.dot(mn, w_ref[...], preferred_element_type=F32)
        o_ref[...] = y.astype(BF16)
        for b in range(nbt):
            for h in range(nh):
                o5_ref[b, :, h, :] = y[b * nm:(b + 1) * nm, h * hd:(h + 1) * hd]


def _mem_kv(mem, g, wk, wv, nm):
    rows, d = mem.shape
    depth = g.shape[0]
    hd = d // X_HEADS
    out = jax.ShapeDtypeStruct((depth, rows, d), BF16)
    out5 = jax.ShapeDtypeStruct((depth, rows // nm, nm, X_HEADS, hd), F32)
    wspec = pl.BlockSpec((None, d, d), lambda l, r: (l, 0, 0))
    ospec = pl.BlockSpec((None, TM, d), lambda l, r: (l, r, 0))
    o5spec = pl.BlockSpec((None, TM // nm, nm, X_HEADS, hd), lambda l, r: (l, r, 0, 0, 0))
    return pl.pallas_call(
        _mem_kv_kernel,
        grid=(depth, rows // TM),
        in_specs=[pl.BlockSpec((TM, d), lambda l, r: (r, 0)),
                  pl.BlockSpec((None, 1, d), lambda l, r: (l, 0, 0)),
                  wspec, wspec],
        out_specs=[ospec, ospec, o5spec, o5spec],
        out_shape=[out, out, out5, out5],
        compiler_params=_params("arbitrary", "arbitrary"),
        name="mem_kv",
    )(mem, g, wk, wv)


def _mixer_a_kernel(x_ref, xt_ref, g_ref, win_ref, lng_ref, lnb_ref, ws_ref, bs_ref, sc_ref, sb_ref, wout_ref,
                    o_ref, vs_ref, u_scr, v_scr, s_scr, *, npt, bs, ls):
    i = pl.program_id(0)
    d = x_ref.shape[1]
    x = jnp.where(i == npt, xt_ref[...], x_ref[...])
    z = jax.nn.gelu(_bdot(_rmsnorm(x, g_ref[...]), win_ref[...]))
    u_scr[...] = z[:, :d]
    v = z[:, d:]
    mu = jnp.mean(v, axis=-1, keepdims=True)
    var = jnp.mean(jnp.square(v - mu), axis=-1, keepdims=True)
    v_scr[...] = ((v - mu) * lax.rsqrt(var + EPS)) * lng_ref[...] + lnb_ref[...]
    gd = d // A_GROUPS

    @pl.when(i < npt)
    def _():
        rows = lax.broadcasted_iota(jnp.int32, (CHUNK, CHUNK), 0)
        cols = lax.broadcasted_iota(jnp.int32, (CHUNK, CHUNK), 1)
        for g in range(A_GROUPS):
            wg = jnp.where(cols <= rows, ws_ref[g], 0.0).astype(BF16)
            bg = bs_ref[g]
            for c in range(TM // CHUNK):
                vc = v_scr[c * CHUNK:(c + 1) * CHUNK, g * gd:(g + 1) * gd]
                s_scr[c * CHUNK:(c + 1) * CHUNK, g * gd:(g + 1) * gd] = (
                    jnp.dot(wg, vc.astype(BF16), preferred_element_type=F32) + bg)

    @pl.when(i == npt)
    def _():
        for t in range(ls):
            acc = jnp.broadcast_to(sb_ref[t:t + 1, :], (bs, d))
            for s in range(t + 1):
                acc = acc + sc_ref[t * ls + s:t * ls + s + 1, :] * v_scr[s * bs:(s + 1) * bs, :]
            s_scr[t * bs:(t + 1) * bs, :] = acc
        vs_ref[...] = v_scr[...]

    o_ref[...] = x + _bdot(u_scr[...] * s_scr[...], wout_ref[...])


def _mixer_a(x, x_tail, tail_blk, g, w_in, ln_g, ln_b, w_s, b_s, w_out, *, npt, bs, ls):
    d = x.shape[1]
    t = (npt + 1) * TM
    assert TM == bs * ls and TM % CHUNK == 0 and ls <= CHUNK
    gd = d // A_GROUPS
    sc = jnp.repeat(w_s[:, :ls, :ls].transpose(1, 2, 0).reshape(ls * ls, A_GROUPS), gd, axis=1)
    sb = jnp.repeat(b_s[:, :ls].T, gd, axis=1)
    tile = pl.BlockSpec((TM, d), lambda i: (i, 0))
    return pl.pallas_call(
        functools.partial(_mixer_a_kernel, npt=npt, bs=bs, ls=ls),
        grid=(npt + 1,),
        in_specs=[pl.BlockSpec((TM, d), lambda i: (jnp.minimum(i, npt - 1), 0)),
                  pl.BlockSpec((TM, d), lambda i: (tail_blk, 0), pipeline_mode=pl.Buffered(1)),
                  _const((1, d)), _const(w_in.shape), _const((1, d)), _const((1, d)),
                  _const(w_s.shape), _const((A_GROUPS, CHUNK, 1)), _const(sc.shape), _const(sb.shape),
                  _const(w_out.shape)],
        out_specs=[tile, pl.BlockSpec((TM, d), lambda i: (0, 0))],
        out_shape=[jax.ShapeDtypeStruct((t, d), F32), jax.ShapeDtypeStruct((TM, d), F32)],
        scratch_shapes=[pltpu.VMEM((TM, d), F32)] * 3,
        compiler_params=_params("arbitrary"),
        name="mixer_a",
    )(x, x_tail, g, w_in, ln_g, ln_b, w_s, b_s[:, :, None], sc, sb, w_out)


def _mixer_b_kernel(x_ref, g_ref, win_ref, cw_ref, pre_ref, wout_ref,
                    o_ref, tail_ref, zs_ref, conv_scr, carry_scr, *, npt, tps, bs):
    i = pl.program_id(0)
    d = x_ref.shape[1]
    x = x_ref[...]
    p = _bdot(_rmsnorm(x, g_ref[...]), win_ref[...])
    gate_b = p[:, :d]
    zc = p[:, d:2 * d] * p[:, 2 * d:]
    w0, w1, w2 = cw_ref[0:1, :], cw_ref[1:2, :], cw_ref[2:3, :]

    @pl.when(i == 0)
    def _():
        carry_scr[...] = jnp.zeros_like(carry_scr)

    @pl.when(i < npt)
    def _():
        keep = (i % tps) != 0
        c6 = jnp.where(keep, carry_scr[6:7, :], 0.0)
        c7 = jnp.where(keep, carry_scr[7:8, :], 0.0)
        rows = lax.broadcasted_iota(jnp.int32, (TM, 1), 0)
        sh1 = jnp.where(rows == 0, c7, pltpu.roll(zc, 1, 0))
        sh2 = jnp.where(rows == 0, c6, jnp.where(rows == 1, c7, pltpu.roll(zc, 2, 0)))
        conv_scr[...] = w0 * sh2 + w1 * sh1 + w2 * zc
        carry_scr[...] = zc[TM - 8:, :]
        tail_ref[...] = zc[TM - 8:, :]

    @pl.when(i == npt)
    def _():
        p0, p1 = pre_ref[0:bs, :], pre_ref[bs:2 * bs, :]
        sh1 = jnp.concatenate([p1, zc[:TM - bs]], axis=0)
        sh2 = jnp.concatenate([p0, p1, zc[:TM - 2 * bs]], axis=0)
        conv_scr[...] = w0 * sh2 + w1 * sh1 + w2 * zc
        zs_ref[...] = zc[TM - 2 * bs:, :]

    o_ref[...] = x + _bdot(gate_b * conv_scr[...], wout_ref[...])


def _mixer_b(x, g, w_in, conv_w, prefix, w_out, *, npt, tps, bs):
    t, d = x.shape
    nb = npt // tps
    tile = pl.BlockSpec((TM, d), lambda i: (i, 0))
    return pl.pallas_call(
        functools.partial(_mixer_b_kernel, npt=npt, tps=tps, bs=bs),
        grid=(npt + 1,),
        in_specs=[tile, _const((1, d)), _const(w_in.shape), _const(conv_w.shape), _const(prefix.shape),
                  _const(w_out.shape)],
        out_specs=[tile,
                   pl.BlockSpec((None, 8, d), lambda i: (jnp.minimum(i // tps, nb - 1), 0, 0)),
                   pl.BlockSpec((2 * bs, d), lambda i: (0, 0))],
        out_shape=[jax.ShapeDtypeStruct((t, d), F32), jax.ShapeDtypeStruct((nb, 8, d), F32),
                   jax.ShapeDtypeStruct((2 * bs, d), F32)],
        scratch_shapes=[pltpu.VMEM((TM, d), F32), pltpu.VMEM((8, d), F32)],
        compiler_params=_params("arbitrary"),
        name="mixer_b",
    )(x, g, w_in, conv_w, prefix, w_out)


def _mixer_c_kernel(x_ref, g_ref, win_ref, pre_ref, wgrp_ref, scale_ref, wout_ref,
                    o_ref, tail_ref, hs_ref, d_scr, hist_scr, *, npt, tps, bs, ls, pos0):
    i = pl.program_id(0)
    d = x_ref.shape[1]
    gd = d // len(POOL_WINDOWS)
    hp = POOL_CTX + 1
    x = x_ref[...]
    h = _bdot(_rmsnorm(x, g_ref[...]), win_ref[...])

    @pl.when(i == 0)
    def _():
        hist_scr[...] = jnp.zeros_like(hist_scr)

    @pl.when(i < npt)
    def _():
        keep = (i % tps) != 0
        ext = jnp.concatenate([jnp.where(keep, hist_scr[...], 0.0), h], axis=0)
        pos = (i % tps) * TM + lax.broadcasted_iota(jnp.int32, (TM, 1), 0)
        sums = ext
        width = 1
        for g, w in enumerate(POOL_WINDOWS):
            while width < w:
                sums = sums + pltpu.roll(sums, width, 0)
                width *= 2
            cnt = jnp.minimum(w, pos + 1).astype(F32)
            sl = slice(g * gd, (g + 1) * gd)
            d_scr[:, sl] = sums[hp:, sl] / cnt - h[:, sl]
        hist_scr[...] = h[TM - hp:, :]
        tail_ref[...] = h[TM - hp:, :]

    @pl.when(i == npt)
    def _():
        for g, w in enumerate(POOL_WINDOWS):
            sl = slice(g * gd, (g + 1) * gd)
            run = jnp.zeros((bs, gd), F32)
            tails = [run]
            for m in range(1, w):
                run = run + pre_ref[(POOL_CTX - m) * bs:(POOL_CTX - m + 1) * bs, sl]
                tails.append(run)
            for t in range(ls):
                n = min(w, t + 1)
                acc = tails[w - n]
                for j in range(n):
                    acc = acc + h[(t - j) * bs:(t - j + 1) * bs, sl]
                cnt = float(min(w, pos0 + t + 1))
                d_scr[t * bs:(t + 1) * bs, sl] = acc / cnt - h[t * bs:(t + 1) * bs, sl]
        hs_ref[...] = h

    m = jnp.concatenate(
        [_bdot(d_scr[:, g * gd:(g + 1) * gd], wgrp_ref[g]) for g in range(len(POOL_WINDOWS))], axis=1)
    o_ref[...] = x + _bdot(m * scale_ref[...], wout_ref[...])


def _mixer_c(x, g, w_in, prefix, w_grp, scale, w_out, *, npt, tps, bs, ls, pos0):
    t, d = x.shape
    nb = npt // tps
    hp = POOL_CTX + 1
    tile = pl.BlockSpec((TM, d), lambda i: (i, 0))
    return pl.pallas_call(
        functools.partial(_mixer_c_kernel, npt=npt, tps=tps, bs=bs, ls=ls, pos0=pos0),
        grid=(npt + 1,),
        in_specs=[tile, _const((1, d)), _const(w_in.shape), _const(prefix.shape), _const(w_grp.shape),
                  _const((1, d)), _const(w_out.shape)],
        out_specs=[tile,
                   pl.BlockSpec((None, hp, d), lambda i: (jnp.minimum(i // tps, nb - 1), 0, 0)),
                   pl.BlockSpec((TM, d), lambda i: (0, 0))],
        out_shape=[jax.ShapeDtypeStruct((t, d), F32), jax.ShapeDtypeStruct((nb, hp, d), F32),
                   jax.ShapeDtypeStruct((TM, d), F32)],
        scratch_shapes=[pltpu.VMEM((TM, d), F32), pltpu.VMEM((hp, d), F32)],
        compiler_params=_params("arbitrary"),
        name="mixer_c",
    )(x, g, w_in, prefix, w_grp, scale, w_out)


def _softmax_rows(s):
    e = jnp.exp(s - jnp.max(s, axis=-1, keepdims=True))
    return e / jnp.sum(e, axis=-1, keepdims=True)


def _xattn_kernel(x_ref, xs_ref, g_ref, wq_ref, wo_ref, kp_ref, vp_ref, ks_ref, vs_ref,
                  o_ref, q_scr, o_scr, *, npt, bs, ls, spb):
    i = pl.program_id(0)
    d = x_ref.shape[1]
    hd = d // X_HEADS
    qscale = float(hd) ** -0.5
    nt = (((1,), (1,)), ((), ()))

    @pl.when(i == 0)
    def _():
        q_scr[...] = _bdot(_rmsnorm(xs_ref[...], g_ref[...]), wq_ref[...]) * qscale

    @pl.when(i < npt)
    def _():
        x = x_ref[...]
        q = (_bdot(_rmsnorm(x, g_ref[...]), wq_ref[...]) * qscale).astype(BF16)
        k = kp_ref[...]
        v = vp_ref[...]
        heads = []
        for h in range(X_HEADS):
            sl = slice(h * hd, (h + 1) * hd)
            s = lax.dot_general(q[:, sl], k[:, sl], nt, preferred_element_type=F32)
            heads.append(_bdot(_softmax_rows(s), v[:, sl]))
        o_ref[...] = x + _bdot(jnp.concatenate(heads, axis=1), wo_ref[...])

        nm = ks_ref.shape[1]
        rows = lax.broadcasted_iota(jnp.int32, (X_HEADS * ls, nm * X_HEADS), 0)
        cols = lax.broadcasted_iota(jnp.int32, (X_HEADS * ls, nm * X_HEADS), 1)
        own = (rows // ls) == (cols % X_HEADS)
        for bb in range(spb):
            b = i * spb + bb
            qb = jnp.concatenate([q_scr[pl.ds(t * bs + b, 1), :] for t in range(ls)], axis=0)
            qh = jnp.concatenate([qb[:, h * hd:(h + 1) * hd] for h in range(X_HEADS)], axis=0)
            kr = ks_ref[bb].reshape(nm * X_HEADS, hd).astype(BF16)
            vr = vs_ref[bb].reshape(nm * X_HEADS, hd).astype(BF16)
            s = lax.dot_general(qh.astype(BF16), kr, nt, preferred_element_type=F32)
            of = _bdot(_softmax_rows(jnp.where(own, s, -jnp.inf)), vr)
            ob = jnp.concatenate([of[h * ls:(h + 1) * ls] for h in range(X_HEADS)], axis=1)
            for t in range(ls):
                o_scr[pl.ds(t * bs + b, 1), :] = ob[t:t + 1]

    @pl.when(i == npt)
    def _():
        o_ref[...] = x_ref[...] + _bdot(o_scr[...], wo_ref[...])


def _xattn(x, g, wq, wo, kp, vp, ks, vs, layer, *, npt, tps, bs, ls):
    t, d = x.shape
    nb, nm = kp.shape[1], kp.shape[2]
    assert bs % npt == 0
    spb = bs // npt
    tile = pl.BlockSpec((TM, d), lambda i: (i, 0))
    pmem = pl.BlockSpec((None, None, nm, d), lambda i: (layer, jnp.minimum(i // tps, nb - 1), 0, 0))
    smem = pl.BlockSpec((None, spb, nm, X_HEADS, d // X_HEADS),
                        lambda i: (layer, jnp.minimum(i, npt - 1), 0, 0, 0))
    return pl.pallas_call(
        functools.partial(_xattn_kernel, npt=npt, bs=bs, ls=ls, spb=spb),
        grid=(npt + 1,),
        in_specs=[tile, pl.BlockSpec((TM, d), lambda i: (npt, 0), pipeline_mode=pl.Buffered(1)),
                  _const((1, d)), _const(wq.shape), _const(wo.shape), pmem, pmem, smem, smem],
        out_specs=tile,
        out_shape=jax.ShapeDtypeStruct((t, d), F32),
        scratch_shapes=[pltpu.VMEM((TM, d), F32), pltpu.VMEM((TM, d), F32)],
        compiler_params=_params("arbitrary"),
        name="xattn",
    )(x, x, g, wq, wo, kp, vp, ks, vs)


def _ffn_kernel(x_ref, g_ref, w1_ref, w3_ref, w2_ref, o_ref, *, chunks):
    x = x_ref[...]
    xn = _rmsnorm(x, g_ref[...]).astype(BF16)
    acc = x
    for lo, hi in chunks:
        h1 = jnp.dot(xn, w1_ref[:, lo:hi], preferred_element_type=F32)
        h3 = jnp.dot(xn, w3_ref[:, lo:hi], preferred_element_type=F32)
        acc = acc + _bdot(jax.nn.silu(h1) * h3, w2_ref[lo:hi, :])
    o_ref[...] = acc


def _ffn(x, g, w1, w3, w2):
    t, d = x.shape
    dff = w1.shape[1]
    step = 1024
    chunks = tuple((lo, min(lo + step, dff)) for lo in range(0, dff, step))
    tile = pl.BlockSpec((TM, d), lambda i: (i, 0))
    return pl.pallas_call(
        functools.partial(_ffn_kernel, chunks=chunks),
        grid=(t // TM,),
        in_specs=[tile, _const((1, d)), _const(w1.shape), _const(w3.shape), _const(w2.shape)],
        out_specs=tile,
        out_shape=jax.ShapeDtypeStruct((t, d), F32),
        compiler_params=_params("arbitrary"),
        name="ffn",
    )(x, g, w1, w3, w2)


def _router_kernel(x_ref, g_ref, r_ref, tri_ref, xn_ref, meta_ref, cnt_ref, carry_scr, *, n_experts):
    i = pl.program_id(0)

    @pl.when(i == 0)
    def _():
        carry_scr[...] = jnp.zeros_like(carry_scr)

    xn = _rmsnorm(x_ref[...], g_ref[...])
    xn_ref[...] = xn
    xh = xn.astype(BF16)
    xl = (xn - xh.astype(F32)).astype(BF16)
    r = r_ref[...]
    rh = r.astype(BF16)
    rl = (r - rh.astype(F32)).astype(BF16)
    logits = (jnp.dot(xh, rh, preferred_element_type=F32) + jnp.dot(xl, rh, preferred_element_type=F32)
              + jnp.dot(xh, rl, preferred_element_type=F32))
    lane = lax.broadcasted_iota(jnp.int32, logits.shape, 1).astype(F32)
    neg = jnp.float32(-jnp.inf)
    logits = jnp.where(lane < n_experts, logits, neg)
    v1 = jnp.max(logits, axis=-1, keepdims=True)
    i1 = jnp.min(jnp.where(logits == v1, lane, float(LANES)), axis=-1, keepdims=True)
    rest = jnp.where(lane == i1, neg, logits)
    v2 = jnp.max(rest, axis=-1, keepdims=True)
    i2 = jnp.min(jnp.where(rest == v2, lane, float(LANES)), axis=-1, keepdims=True)
    e2 = jnp.exp(v2 - v1)
    den = 1.0 + e2
    hit = jnp.where(lane == i1, 1.0, jnp.where(lane == i2, 1.0, 0.0))
    before = jnp.dot(tri_ref[...], hit.astype(BF16), preferred_element_type=F32) + carry_scr[0:1, :]
    r1 = jnp.sum(jnp.where(lane == i1, before, 0.0), axis=-1, keepdims=True)
    r2 = jnp.sum(jnp.where(lane == i2, before, 0.0), axis=-1, keepdims=True)
    cols = (i1, i2, r1, r2, 1.0 / den, e2 / den)
    meta = jnp.zeros_like(logits)
    for c, val in enumerate(cols):
        meta = jnp.where(lane == c, val, meta)
    meta_ref[...] = meta
    carry_scr[...] = carry_scr[...] + jnp.sum(hit, axis=0, keepdims=True)
    cnt_ref[...] = carry_scr[...]


def _router(x, g, router):
    t, d = x.shape
    ne = router.shape[1]
    assert TOP_K == 2 and ne <= LANES
    rpad = jnp.pad(router, ((0, 0), (0, LANES - ne)))
    tri = jnp.tril(jnp.ones((TM, TM), BF16), -1)
    return pl.pallas_call(
        functools.partial(_router_kernel, n_experts=ne),
        grid=(t // TM,),
        in_specs=[pl.BlockSpec((TM, d), lambda i: (i, 0)), _const((1, d)), _const(rpad.shape),
                  _const(tri.shape)],
        out_specs=[pl.BlockSpec((TM, d), lambda i: (i, 0)), pl.BlockSpec((TM, LANES), lambda i: (i, 0)),
                   pl.BlockSpec((8, LANES), lambda i: (0, 0))],
        out_shape=[jax.ShapeDtypeStruct((t, d), F32), jax.ShapeDtypeStruct((t, LANES), F32),
                   jax.ShapeDtypeStruct((8, LANES), F32)],
        scratch_shapes=[pltpu.VMEM((8, LANES), F32)],
        compiler_params=_params("arbitrary"),
        name="router",
    )(x, g, rpad, tri)


def _invert_kernel(dst_ref, inv_ref):
    i = pl.program_id(0)
    n = dst_ref.shape[0]

    @pl.when(i == 0)
    def _():
        def clear(j, carry):
            inv_ref[j] = 0
            return carry

        lax.fori_loop(0, inv_ref.shape[0], clear, 0, unroll=8)

    def body(j, carry):
        inv_ref[dst_ref[j]] = i * n + j
        return carry

    lax.fori_loop(0, n, body, 0, unroll=8)


def _invert(dst, n_slots):
    blk = TOP_K * DC_TM
    return pl.pallas_call(
        _invert_kernel,
        grid=(dst.shape[0] // blk,),
        in_specs=[pl.BlockSpec((blk,), lambda i: (i,), memory_space=pltpu.SMEM)],
        out_specs=pl.BlockSpec(memory_space=pltpu.SMEM),
        out_shape=jax.ShapeDtypeStruct((n_slots,), jnp.int32),
        compiler_params=_params("arbitrary"),
        name="invert",
    )(dst)


def _moe_kernel(te_ref, tf_ref, nt_ref, inv_ref, xn_hbm, w1_hbm, w3_hbm, w2_hbm, ys_ref,
                xg, s1, s3, s2, w1_scr, w3_scr, w2_scr, gsem, sem, *, layer, n_tiles):
    i = pl.program_id(0)
    dff = w1_scr.shape[1]
    nf = dff // MOE_TF
    valid = i < nt_ref[0]
    first = valid & (tf_ref[i] == 1)
    nxt = jnp.minimum(i + 1, n_tiles - 1)
    next_first = (i + 1 < nt_ref[0]) & (tf_ref[nxt] == 1)
    buf = i % 2
    next_off = ((i + 1) % 2) * MOE_TM

    def gather_row(off, r, dst_buf):
        tok = inv_ref[off + r] // TOP_K
        pltpu.make_async_copy(xn_hbm.at[pl.ds(tok, 1)], xg.at[dst_buf, pl.ds(r, 1)], gsem.at[dst_buf]).start()

    def gather_loop(off, dst_buf):
        def body(r, carry):
            gather_row(off, r, dst_buf)
            return carry

        lax.fori_loop(0, MOE_TM, body, 0, unroll=8)

    @pl.when(i == 0)
    def _():
        gather_loop(0, 0)

    @pl.when(i <= nt_ref[0])
    def _():
        pltpu.make_async_copy(xn_hbm.at[pl.ds(0, MOE_TM)], xg.at[buf], gsem.at[buf]).wait()

    def chunk(e, f, slot):
        cols = pl.ds(f * MOE_TF, MOE_TF)
        return (pltpu.make_async_copy(w1_hbm.at[layer, e, :, cols], s1.at[slot], sem.at[slot]),
                pltpu.make_async_copy(w3_hbm.at[layer, e, :, cols], s3.at[slot], sem.at[slot]),
                pltpu.make_async_copy(w2_hbm.at[layer, e, cols, :], s2.at[slot], sem.at[slot]))

    def start(e, f, slot):
        for c in chunk(e, f, slot):
            c.start()

    def swiglu(xb, lo, hi):
        h1 = jnp.dot(xb, w1_scr[:, lo:hi], preferred_element_type=F32)
        h3 = jnp.dot(xb, w3_scr[:, lo:hi], preferred_element_type=F32)
        return _bdot(jax.nn.silu(h1) * h3, w2_scr[lo:hi, :])

    @pl.when(first & (i == 0))
    def _():
        start(te_ref[i], 0, 0)

    @pl.when(first)
    def _():
        xb = xg[buf].astype(BF16)
        gather_loop(next_off, 1 - buf)
        acc = None
        for f in range(nf):
            slot = f % 2
            for c in chunk(te_ref[i], f, slot):
                c.wait()
            if f + 1 < nf:
                start(te_ref[i], f + 1, 1 - slot)
            lo, hi = f * MOE_TF, (f + 1) * MOE_TF
            w1_scr[:, lo:hi] = s1[slot].astype(BF16)
            w3_scr[:, lo:hi] = s3[slot].astype(BF16)
            w2_scr[lo:hi, :] = s2[slot].astype(BF16)
            y = swiglu(xb, lo, hi)
            acc = y if acc is None else acc + y
        ys_ref[...] = acc

        @pl.when(next_first)
        def _():
            start(te_ref[nxt], 0, 0)

    @pl.when(valid & jnp.logical_not(first))
    def _():
        @pl.when(next_first)
        def _():
            start(te_ref[nxt], 0, 0)

        xb = xg[buf].astype(BF16)
        for r in range(MOE_TM):
            gather_row(next_off, r, 1 - buf)
        acc = None
        for lo in range(0, dff, MOE_CHUNK):
            y = swiglu(xb, lo, min(lo + MOE_CHUNK, dff))
            acc = y if acc is None else acc + y
        ys_ref[...] = acc

    @pl.when(jnp.logical_not(valid))
    def _():
        ys_ref[...] = jnp.zeros_like(ys_ref)


def _moe(te, tf, nt, inv, xn, w1, w3, w2, layer, n_tiles):
    d = xn.shape[1]
    dff = w1.shape[3]
    tile = pl.BlockSpec((MOE_TM, d), lambda i, te, tf, nt: (i, 0))
    pair = pl.BlockSpec((2 * MOE_TM,), lambda i, te, tf, nt: (jnp.minimum(i + 1, n_tiles - 1) // 2,),
                        memory_space=pltpu.SMEM)
    hbm = pl.BlockSpec(memory_space=pl.ANY)
    return pl.pallas_call(
        functools.partial(_moe_kernel, layer=layer, n_tiles=n_tiles),
        grid_spec=pltpu.PrefetchScalarGridSpec(
            num_scalar_prefetch=3,
            grid=(n_tiles,),
            in_specs=[pair, hbm, hbm, hbm, hbm],
            out_specs=tile,
            scratch_shapes=[pltpu.VMEM((2, MOE_TM, d), F32),
                            pltpu.VMEM((2, d, MOE_TF), F32), pltpu.VMEM((2, d, MOE_TF), F32),
                            pltpu.VMEM((2, MOE_TF, d), F32),
                            pltpu.VMEM((d, dff), BF16), pltpu.VMEM((d, dff), BF16),
                            pltpu.VMEM((dff, d), BF16), pltpu.SemaphoreType.DMA((2,)),
                            pltpu.SemaphoreType.DMA((2,))]),
        out_shape=jax.ShapeDtypeStruct((n_tiles * MOE_TM, d), F32),
        compiler_params=_params("arbitrary"),
        name="moe",
    )(te, tf, nt, inv, xn, w1, w3, w2)


def _combine_kernel(dst_ref, x_ref, meta_ref, ys_ref, o_ref, a_scr, b_scr, sem):
    def body(r, carry):
        pltpu.make_async_copy(ys_ref.at[pl.ds(dst_ref[TOP_K * r], 1)], a_scr.at[pl.ds(r, 1)],
                              sem.at[0]).start(priority=0)
        pltpu.make_async_copy(ys_ref.at[pl.ds(dst_ref[TOP_K * r + 1], 1)], b_scr.at[pl.ds(r, 1)],
                              sem.at[1]).start(priority=1)
        return carry

    rows = x_ref.shape[0]
    lax.fori_loop(0, rows, body, 0, unroll=8)
    pltpu.make_async_copy(ys_ref.at[pl.ds(0, rows)], a_scr, sem.at[0]).wait()
    pltpu.make_async_copy(ys_ref.at[pl.ds(0, rows)], b_scr, sem.at[1]).wait()
    meta = meta_ref[...]
    o_ref[...] = x_ref[...] + (meta[:, 4:5] * a_scr[...] + meta[:, 5:6] * b_scr[...])


def _combine(dst, x, meta, ys):
    t, d = x.shape
    tile = pl.BlockSpec((DC_TM, d), lambda i: (i, 0))
    return pl.pallas_call(
        _combine_kernel,
        grid=(t // DC_TM,),
        in_specs=[pl.BlockSpec((TOP_K * DC_TM,), lambda i: (i,), memory_space=pltpu.SMEM), tile,
                  pl.BlockSpec((DC_TM, LANES), lambda i: (i, 0)), pl.BlockSpec(memory_space=pl.ANY)],
        out_specs=tile,
        out_shape=jax.ShapeDtypeStruct((t, d), F32),
        scratch_shapes=[pltpu.VMEM((DC_TM, d), F32), pltpu.VMEM((DC_TM, d), F32),
                        pltpu.SemaphoreType.DMA((2,))],
        compiler_params=_params("arbitrary"),
        name="combine",
    )(dst, x, meta, ys)


def _moe_ffn(x, g, router, w1, w3, w2, layer):
    t, d = x.shape
    ne = router.shape[1]
    assert t % DC_TM == 0 and (TOP_K * DC_TM) % 1024 == 0
    xn, meta, cnt = _router(x, g, router)
    n = cnt[0, :ne].astype(jnp.int32)
    tiles = (n + MOE_TM - 1) // MOE_TM
    ends = jnp.cumsum(tiles)
    n_tiles = (t * TOP_K) // MOE_TM + ne + 1
    off = (ends - tiles) * MOE_TM
    e = meta[:, :TOP_K].astype(jnp.int32)
    dst = (jnp.take(off, e) + meta[:, TOP_K:2 * TOP_K].astype(jnp.int32)).reshape(-1)
    tid = jnp.minimum(jnp.arange(n_tiles, dtype=jnp.int32), ends[-1] - 1)
    te = jnp.sum((tid[:, None] >= ends[None, :]).astype(jnp.int32), axis=1)
    first = (tid == jnp.take(ends - tiles, te)) & (jnp.arange(n_tiles) < ends[-1])
    inv = _invert(dst, (n_tiles + 1) // 2 * 2 * MOE_TM)
    ys = _moe(te, first.astype(jnp.int32), ends[-1:].astype(jnp.int32), inv, xn, w1, w3, w2, layer, n_tiles)
    return _combine(dst, x, meta, ys)


def _final_kernel(x_ref, g_ref, op_ref, os_ref, *, npt):
    i = pl.program_id(0)
    y = _rmsnorm(x_ref[...], g_ref[...])

    @pl.when(i < npt)
    def _():
        op_ref[...] = y

    @pl.when(i == npt)
    def _():
        os_ref[...] = y


def _final_norm(x, g, *, npt):
    t, d = x.shape
    return pl.pallas_call(
        functools.partial(_final_kernel, npt=npt),
        grid=(npt + 1,),
        in_specs=[pl.BlockSpec((TM, d), lambda i: (i, 0)), _const((1, d))],
        out_specs=[pl.BlockSpec((TM, d), lambda i: (jnp.minimum(i, npt - 1), 0)),
                   pl.BlockSpec((TM, d), lambda i: (0, 0))],
        out_shape=[jax.ShapeDtypeStruct((npt * TM, d), F32), jax.ShapeDtypeStruct((TM, d), F32)],
        compiler_params=_params("arbitrary"),
        name="final_norm",
    )(x, g)


def kernel(x_prompt, x_sample, cache_mem_k, cache_mem_v, state_conv, state_pool, mem_prompt, norm_mix, norm_xa, norm_ffn, norm_mem, norm_final, a_w_in, a_ln_g, a_ln_b, a_w_s, a_b_s, a_w_out, b_w_in, b_conv, b_w_out, c_w_in, c_w_grp, c_scale, c_w_out, xa_wq, xa_wk, xa_wv, xa_wo, f_w1, f_w3, f_w2, m_router, m_w1, m_w3, m_w2):
    nb, seq, d = x_prompt.shape
    bs, ls, _ = x_sample.shape
    depth = norm_mix.shape[0]
    nm = mem_prompt.shape[1]
    assert seq % TM == 0 and TM == bs * ls and (nb * nm) % TM == 0
    tps = seq // TM
    npt = nb * tps
    pos0 = PAST_LEN
    dims = dict(npt=npt, bs=bs, ls=ls)
    bf = lambda w: w.astype(BF16)
    row = lambda v: v.reshape(1, d)
    to_pm = lambda a: a.transpose(1, 0, 2).reshape(-1, d)
    from_pm = lambda a, r: a.reshape(r, bs, d).transpose(1, 0, 2)

    mem_k, mem_v, mem_k5, mem_v5 = _mem_kv(mem_prompt.reshape(nb * nm, d), norm_mem[:, None, :],
                                           bf(xa_wk), bf(xa_wv), nm)
    kp = mem_k.reshape(depth, nb, nm, d)
    vp = mem_v.reshape(depth, nb, nm, d)
    ks, vs = cache_mem_k, cache_mem_v

    x = None
    conv_p, conv_s, pool_p, pool_s, chunk_v = [], [], [], [], []
    for i in range(depth):
        kind, j = i % 3, i // 3
        g = row(norm_mix[i])
        if kind == 0:
            srcs = (x_prompt.reshape(nb * seq, d), to_pm(x_sample), 0) if x is None else (x, x, npt)
            x, v_s = _mixer_a(*srcs, g, bf(a_w_in[j]), row(a_ln_g[j]), row(a_ln_b[j]), a_w_s[j], a_b_s[j],
                              bf(a_w_out[j]), **dims)
            chunk_v.append(from_pm(v_s, ls))
        elif kind == 1:
            x, tail, z_s = _mixer_b(x, g, bf(b_w_in[j]), b_conv[j], to_pm(state_conv[j]), bf(b_w_out[j]),
                                    npt=npt, tps=tps, bs=bs)
            conv_p.append(tail[:, 6:, :])
            conv_s.append(from_pm(z_s, 2))
        else:
            x, tail, h_s = _mixer_c(x, g, bf(c_w_in[j]), to_pm(state_pool[j]), bf(c_w_grp[j]),
                                    row(c_scale[j]), bf(c_w_out[j]), tps=tps, pos0=pos0, **dims)
            pool_p.append(tail[:, 1:, :])
            pool_s.append(jnp.concatenate([state_pool[j], from_pm(h_s, ls)], axis=1)[:, -POOL_CTX:, :])
        x = _xattn(x, row(norm_xa[i]), bf(xa_wq[i]), bf(xa_wo[i]), kp, vp, ks, vs, i, tps=tps, **dims)
        g = row(norm_ffn[i])
        if i % 2 == 0:
            x = _ffn(x, g, bf(f_w1[i // 2]), bf(f_w3[i // 2]), bf(f_w2[i // 2]))
        else:
            x = _moe_ffn(x, g, m_router[i // 2], m_w1, m_w3, m_w2, i // 2)
    y_p, y_s = _final_norm(x, row(norm_final), npt=npt)

    return (y_p.reshape(nb, seq, d), from_pm(y_s, ls), mem_k5, mem_v5,
            jnp.stack(conv_p), jnp.stack(pool_p), jnp.stack(conv_s), jnp.stack(pool_s), jnp.stack(chunk_v))
```

```python
import functools

import jax
import jax.numpy as jnp
from jax import lax
from jax.experimental import pallas as pl
from jax.experimental.pallas import tpu as pltpu

F32 = jnp.float32
BF16 = jnp.bfloat16

EPS = 1e-6
TM = 512
CHUNK = 128
A_GROUPS = 8
POOL_WINDOWS = (2, 4, 8, 16)
POOL_CTX = max(POOL_WINDOWS) - 1
X_HEADS = 4
TOP_K = 2
PAST_LEN = 16384
LANES = 128
SUBLANES = 8
DC_TM = 1536
MOE_TM = 512
MOE_TF = 512
MOE_CHUNK = 1024
VMEM_LIMIT = 60 * 1024 * 1024


def _params(*sem):
    return pltpu.CompilerParams(dimension_semantics=sem, vmem_limit_bytes=VMEM_LIMIT)


def _const(shape):
    nd = len(shape)
    return pl.BlockSpec(shape, lambda *_: (0,) * nd, pipeline_mode=pl.Buffered(1))


def _rmsnorm(x, g):
    r = lax.rsqrt(jnp.mean(x * x, axis=-1, keepdims=True) + EPS)
    return (x * r) * g


def _bdot(a, w):
    return jnp.dot(a.astype(BF16), w, preferred_element_type=F32)


def _mem_kv_kernel(m_ref, g_ref, wk_ref, wv_ref, k_ref, v_ref, k5_ref, v5_ref):
    mn = _rmsnorm(m_ref[...], g_ref[...]).astype(BF16)
    nbt, nm, nh, hd = k5_ref.shape
    for w_ref, o_ref, o5_ref in ((wk_ref, k_ref, k5_ref), (wv_ref, v_ref, v5_ref)):
        y = jnp.dot(mn, w_ref[...], preferred_element_type=F32)
        o_ref[...] = y.astype(BF16)
        for b in range(nbt):
            for h in range(nh):
                o5_ref[b, :, h, :] = y[b * nm:(b + 1) * nm, h * hd:(h + 1) * hd]


def _mem_kv(mem, g, wk, wv, nm):
    rows, d = mem.shape
    depth = g.shape[0]
    hd = d // X_HEADS
    out = jax.ShapeDtypeStruct((depth, rows, d), BF16)
    out5 = jax.ShapeDtypeStruct((depth, rows // nm, nm, X_HEADS, hd), F32)
    wspec = pl.BlockSpec((None, d, d), lambda l, r: (l, 0, 0))
    ospec = pl.BlockSpec((None, TM, d), lambda l, r: (l, r, 0))
    o5spec = pl.BlockSpec((None, TM // nm, nm, X_HEADS, hd), lambda l, r: (l, r, 0, 0, 0))
    return pl.pallas_call(
        _mem_kv_kernel,
        grid=(depth, rows // TM),
        in_specs=[pl.BlockSpec((TM, d), lambda l, r: (r, 0)),
                  pl.BlockSpec((None, 1, d), lambda l, r: (l, 0, 0)),
                  wspec, wspec],
        out_specs=[ospec, ospec, o5spec, o5spec],
        out_shape=[out, out, out5, out5],
        compiler_params=_params("arbitrary", "arbitrary"),
        name="mem_kv",
    )(mem, g, wk, wv)


def _mixer_a_kernel(x_ref, xt_ref, g_ref, win_ref, lng_ref, lnb_ref, ws_ref, bs_ref, sc_ref, sb_ref, wout_ref,
                    o_ref, vs_ref, u_scr, v_scr, s_scr, *, npt, bs, ls):
    i = pl.program_id(0)
    d = x_ref.shape[1]
    x = jnp.where(i == npt, xt_ref[...], x_ref[...])
    z = jax.nn.gelu(_bdot(_rmsnorm(x, g_ref[...]), win_ref[...]))
    u_scr[...] = z[:, :d]
    v = z[:, d:]
    mu = jnp.mean(v, axis=-1, keepdims=True)
    var = jnp.mean(jnp.square(v - mu), axis=-1, keepdims=True)
    v_scr[...] = ((v - mu) * lax.rsqrt(var + EPS)) * lng_ref[...] + lnb_ref[...]
    gd = d // A_GROUPS

    @pl.when(i < npt)
    def _():
        rows = lax.broadcasted_iota(jnp.int32, (CHUNK, CHUNK), 0)
        cols = lax.broadcasted_iota(jnp.int32, (CHUNK, CHUNK), 1)
        for g in range(A_GROUPS):
            wg = jnp.where(cols <= rows, ws_ref[g], 0.0).astype(BF16)
            bg = bs_ref[g]
            for c in range(TM // CHUNK):
                vc = v_scr[c * CHUNK:(c + 1) * CHUNK, g * gd:(g + 1) * gd]
                s_scr[c * CHUNK:(c + 1) * CHUNK, g * gd:(g + 1) * gd] = (
                    jnp.dot(wg, vc.astype(BF16), preferred_element_type=F32) + bg)

    @pl.when(i == npt)
    def _():
        for t in range(ls):
            acc = jnp.broadcast_to(sb_ref[t:t + 1, :], (bs, d))
            for s in range(t + 1):
                acc = acc + sc_ref[t * ls + s:t * ls + s + 1, :] * v_scr[s * bs:(s + 1) * bs, :]
            s_scr[t * bs:(t + 1) * bs, :] = acc
        vs_ref[...] = v_scr[...]

    o_ref[...] = x + _bdot(u_scr[...] * s_scr[...], wout_ref[...])


def _mixer_a(x, x_tail, tail_blk, g, w_in, ln_g, ln_b, w_s, b_s, w_out, *, npt, bs, ls):
    d = x.shape[1]
    t = (npt + 1) * TM
    assert TM == bs * ls and TM % CHUNK == 0 and ls <= CHUNK
    gd = d // A_GROUPS
    sc = jnp.repeat(w_s[:, :ls, :ls].transpose(1, 2, 0).reshape(ls * ls, A_GROUPS), gd, axis=1)
    sb = jnp.repeat(b_s[:, :ls].T, gd, axis=1)
    tile = pl.BlockSpec((TM, d), lambda i: (i, 0))
    return pl.pallas_call(
        functools.partial(_mixer_a_kernel, npt=npt, bs=bs, ls=ls),
        grid=(npt + 1,),
        in_specs=[pl.BlockSpec((TM, d), lambda i: (jnp.minimum(i, npt - 1), 0)),
                  pl.BlockSpec((TM, d), lambda i: (tail_blk, 0), pipeline_mode=pl.Buffered(1)),
                  _const((1, d)), _const(w_in.shape), _const((1, d)), _const((1, d)),
                  _const(w_s.shape), _const((A_GROUPS, CHUNK, 1)), _const(sc.shape), _const(sb.shape),
                  _const(w_out.shape)],
        out_specs=[tile, pl.BlockSpec((TM, d), lambda i: (0, 0))],
        out_shape=[jax.ShapeDtypeStruct((t, d), F32), jax.ShapeDtypeStruct((TM, d), F32)],
        scratch_shapes=[pltpu.VMEM((TM, d), F32)] * 3,
        compiler_params=_params("arbitrary"),
        name="mixer_a",
    )(x, x_tail, g, w_in, ln_g, ln_b, w_s, b_s[:, :, None], sc, sb, w_out)


def _mixer_b_kernel(x_ref, g_ref, win_ref, cw_ref, pre_ref, wout_ref,
                    o_ref, tail_ref, zs_ref, conv_scr, carry_scr, *, npt, tps, bs):
    i = pl.program_id(0)
    d = x_ref.shape[1]
    x = x_ref[...]
    p = _bdot(_rmsnorm(x, g_ref[...]), win_ref[...])
    gate_b = p[:, :d]
    zc = p[:, d:2 * d] * p[:, 2 * d:]
    w0, w1, w2 = cw_ref[0:1, :], cw_ref[1:2, :], cw_ref[2:3, :]

    @pl.when(i == 0)
    def _():
        carry_scr[...] = jnp.zeros_like(carry_scr)

    @pl.when(i < npt)
    def _():
        keep = (i % tps) != 0
        c6 = jnp.where(keep, carry_scr[6:7, :], 0.0)
        c7 = jnp.where(keep, carry_scr[7:8, :], 0.0)
        rows = lax.broadcasted_iota(jnp.int32, (TM, 1), 0)
        sh1 = jnp.where(rows == 0, c7, pltpu.roll(zc, 1, 0))
        sh2 = jnp.where(rows == 0, c6, jnp.where(rows == 1, c7, pltpu.roll(zc, 2, 0)))
        conv_scr[...] = w0 * sh2 + w1 * sh1 + w2 * zc
        carry_scr[...] = zc[TM - 8:, :]
        tail_ref[...] = zc[TM - 8:, :]

    @pl.when(i == npt)
    def _():
        p0, p1 = pre_ref[0:bs, :], pre_ref[bs:2 * bs, :]
        sh1 = jnp.concatenate([p1, zc[:TM - bs]], axis=0)
        sh2 = jnp.concatenate([p0, p1, zc[:TM - 2 * bs]], axis=0)
        conv_scr[...] = w0 * sh2 + w1 * sh1 + w2 * zc
        zs_ref[...] = zc[TM - 2 * bs:, :]

    o_ref[...] = x + _bdot(gate_b * conv_scr[...], wout_ref[...])


def _mixer_b(x, g, w_in, conv_w, prefix, w_out, *, npt, tps, bs):
    t, d = x.shape
    nb = npt // tps
    tile = pl.BlockSpec((TM, d), lambda i: (i, 0))
    return pl.pallas_call(
        functools.partial(_mixer_b_kernel, npt=npt, tps=tps, bs=bs),
        grid=(npt + 1,),
        in_specs=[tile, _const((1, d)), _const(w_in.shape), _const(conv_w.shape), _const(prefix.shape),
                  _const(w_out.shape)],
        out_specs=[tile,
                   pl.BlockSpec((None, 8, d), lambda i: (jnp.minimum(i // tps, nb - 1), 0, 0)),
                   pl.BlockSpec((2 * bs, d), lambda i: (0, 0))],
        out_shape=[jax.ShapeDtypeStruct((t, d), F32), jax.ShapeDtypeStruct((nb, 8, d), F32),
                   jax.ShapeDtypeStruct((2 * bs, d), F32)],
        scratch_shapes=[pltpu.VMEM((TM, d), F32), pltpu.VMEM((8, d), F32)],
        compiler_params=_params("arbitrary"),
        name="mixer_b",
    )(x, g, w_in, conv_w, prefix, w_out)


def _mixer_c_kernel(x_ref, g_ref, win_ref, pre_ref, wgrp_ref, scale_ref, wout_ref,
                    o_ref, tail_ref, hs_ref, d_scr, hist_scr, *, npt, tps, bs, ls, pos0):
    i = pl.program_id(0)
    d = x_ref.shape[1]
    gd = d // len(POOL_WINDOWS)
    hp = POOL_CTX + 1
    x = x_ref[...]
    h = _bdot(_rmsnorm(x, g_ref[...]), win_ref[...])

    @pl.when(i == 0)
    def _():
        hist_scr[...] = jnp.zeros_like(hist_scr)

    @pl.when(i < npt)
    def _():
        keep = (i % tps) != 0
        ext = jnp.concatenate([jnp.where(keep, hist_scr[...], 0.0), h], axis=0)
        pos = (i % tps) * TM + lax.broadcasted_iota(jnp.int32, (TM, 1), 0)
        sums = ext
        width = 1
        for g, w in enumerate(POOL_WINDOWS):
            while width < w:
                sums = sums + pltpu.roll(sums, width, 0)
                width *= 2
            cnt = jnp.minimum(w, pos + 1).astype(F32)
            sl = slice(g * gd, (g + 1) * gd)
            d_scr[:, sl] = sums[hp:, sl] / cnt - h[:, sl]
        hist_scr[...] = h[TM - hp:, :]
        tail_ref[...] = h[TM - hp:, :]

    @pl.when(i == npt)
    def _():
        for g, w in enumerate(POOL_WINDOWS):
            sl = slice(g * gd, (g + 1) * gd)
            run = jnp.zeros((bs, gd), F32)
            tails = [run]
            for m in range(1, w):
                run = run + pre_ref[(POOL_CTX - m) * bs:(POOL_CTX - m + 1) * bs, sl]
                tails.append(run)
            for t in range(ls):
                n = min(w, t + 1)
                acc = tails[w - n]
                for j in range(n):
                    acc = acc + h[(t - j) * bs:(t - j + 1) * bs, sl]
                cnt = float(min(w, pos0 + t + 1))
                d_scr[t * bs:(t + 1) * bs, sl] = acc / cnt - h[t * bs:(t + 1) * bs, sl]
        hs_ref[...] = h

    m = jnp.concatenate(
        [_bdot(d_scr[:, g * gd:(g + 1) * gd], wgrp_ref[g]) for g in range(len(POOL_WINDOWS))], axis=1)
    o_ref[...] = x + _bdot(m * scale_ref[...], wout_ref[...])


def _mixer_c(x, g, w_in, prefix, w_grp, scale, w_out, *, npt, tps, bs, ls, pos0):
    t, d = x.shape
    nb = npt // tps
    hp = POOL_CTX + 1
    tile = pl.BlockSpec((TM, d), lambda i: (i, 0))
    return pl.pallas_call(
        functools.partial(_mixer_c_kernel, npt=npt, tps=tps, bs=bs, ls=ls, pos0=pos0),
        grid=(npt + 1,),
        in_specs=[tile, _const((1, d)), _const(w_in.shape), _const(prefix.shape), _const(w_grp.shape),
                  _const((1, d)), _const(w_out.shape)],
        out_specs=[tile,
                   pl.BlockSpec((None, hp, d), lambda i: (jnp.minimum(i // tps, nb - 1), 0, 0)),
                   pl.BlockSpec((TM, d), lambda i: (0, 0))],
        out_shape=[jax.ShapeDtypeStruct((t, d), F32), jax.ShapeDtypeStruct((nb, hp, d), F32),
                   jax.ShapeDtypeStruct((TM, d), F32)],
        scratch_shapes=[pltpu.VMEM((TM, d), F32), pltpu.VMEM((hp, d), F32)],
        compiler_params=_params("arbitrary"),
        name="mixer_c",
    )(x, g, w_in, prefix, w_grp, scale, w_out)


def _softmax_rows(s):
    e = jnp.exp(s - jnp.max(s, axis=-1, keepdims=True))
    return e / jnp.sum(e, axis=-1, keepdims=True)


def _xattn_kernel(x_ref, xs_ref, g_ref, wq_ref, wo_ref, kp_ref, vp_ref, ks_ref, vs_ref,
                  o_ref, q_scr, o_scr, *, npt, bs, ls, spb):
    i = pl.program_id(0)
    d = x_ref.shape[1]
    hd = d // X_HEADS
    qscale = float(hd) ** -0.5
    nt = (((1,), (1,)), ((), ()))

    @pl.when(i == 0)
    def _():
        q_scr[...] = _bdot(_rmsnorm(xs_ref[...], g_ref[...]), wq_ref[...]) * qscale

    @pl.when(i < npt)
    def _():
        x = x_ref[...]
        q = (_bdot(_rmsnorm(x, g_ref[...]), wq_ref[...]) * qscale).astype(BF16)
        k = kp_ref[...]
        v = vp_ref[...]
        heads = []
        for h in range(X_HEADS):
            sl = slice(h * hd, (h + 1) * hd)
            s = lax.dot_general(q[:, sl], k[:, sl], nt, preferred_element_type=F32)
            heads.append(_bdot(_softmax_rows(s), v[:, sl]))
        o_ref[...] = x + _bdot(jnp.concatenate(heads, axis=1), wo_ref[...])

        nm = ks_ref.shape[1]
        rows = lax.broadcasted_iota(jnp.int32, (X_HEADS * ls, nm * X_HEADS), 0)
        cols = lax.broadcasted_iota(jnp.int32, (X_HEADS * ls, nm * X_HEADS), 1)
        own = (rows // ls) == (cols % X_HEADS)
        for bb in range(spb):
            b = i * spb + bb
            qb = jnp.concatenate([q_scr[pl.ds(t * bs + b, 1), :] for t in range(ls)], axis=0)
            qh = jnp.concatenate([qb[:, h * hd:(h + 1) * hd] for h in range(X_HEADS)], axis=0)
            kr = ks_ref[bb].reshape(nm * X_HEADS, hd).astype(BF16)
            vr = vs_ref[bb].reshape(nm * X_HEADS, hd).astype(BF16)
            s = lax.dot_general(qh.astype(BF16), kr, nt, preferred_element_type=F32)
            of = _bdot(_softmax_rows(jnp.where(own, s, -jnp.inf)), vr)
            ob = jnp.concatenate([of[h * ls:(h + 1) * ls] for h in range(X_HEADS)], axis=1)
            for t in range(ls):
                o_scr[pl.ds(t * bs + b, 1), :] = ob[t:t + 1]

    @pl.when(i == npt)
    def _():
        o_ref[...] = x_ref[...] + _bdot(o_scr[...], wo_ref[...])


def _xattn(x, g, wq, wo, kp, vp, ks, vs, layer, *, npt, tps, bs, ls):
    t, d = x.shape
    nb, nm = kp.shape[1], kp.shape[2]
    assert bs % npt == 0
    spb = bs // npt
    tile = pl.BlockSpec((TM, d), lambda i: (i, 0))
    pmem = pl.BlockSpec((None, None, nm, d), lambda i: (layer, jnp.minimum(i // tps, nb - 1), 0, 0))
    smem = pl.BlockSpec((None, spb, nm, X_HEADS, d // X_HEADS),
                        lambda i: (layer, jnp.minimum(i, npt - 1), 0, 0, 0))
    return pl.pallas_call(
        functools.partial(_xattn_kernel, npt=npt, bs=bs, ls=ls, spb=spb),
        grid=(npt + 1,),
        in_specs=[tile, pl.BlockSpec((TM, d), lambda i: (npt, 0), pipeline_mode=pl.Buffered(1)),
                  _const((1, d)), _const(wq.shape), _const(wo.shape), pmem, pmem, smem, smem],
        out_specs=tile,
        out_shape=jax.ShapeDtypeStruct((t, d), F32),
        scratch_shapes=[pltpu.VMEM((TM, d), F32), pltpu.VMEM((TM, d), F32)],
        compiler_params=_params("arbitrary"),
        name="xattn",
    )(x, x, g, wq, wo, kp, vp, ks, vs)


def _ffn_kernel(x_ref, g_ref, w1_ref, w3_ref, w2_ref, o_ref, *, chunks):
    x = x_ref[...]
    xn = _rmsnorm(x, g_ref[...]).astype(BF16)
    acc = x
    for lo, hi in chunks:
        h1 = jnp.dot(xn, w1_ref[:, lo:hi], preferred_element_type=F32)
        h3 = jnp.dot(xn, w3_ref[:, lo:hi], preferred_element_type=F32)
        acc = acc + _bdot(jax.nn.silu(h1) * h3, w2_ref[lo:hi, :])
    o_ref[...] = acc


def _ffn(x, g, w1, w3, w2):
    t, d = x.shape
    dff = w1.shape[1]
    step = 1024
    chunks = tuple((lo, min(lo + step, dff)) for lo in range(0, dff, step))
    tile = pl.BlockSpec((TM, d), lambda i: (i, 0))
    return pl.pallas_call(
        functools.partial(_ffn_kernel, chunks=chunks),
        grid=(t // TM,),
        in_specs=[tile, _const((1, d)), _const(w1.shape), _const(w3.shape), _const(w2.shape)],
        out_specs=tile,
        out_shape=jax.ShapeDtypeStruct((t, d), F32),
        compiler_params=_params("arbitrary"),
        name="ffn",
    )(x, g, w1, w3, w2)


def _router_kernel(x_ref, g_ref, r_ref, tri_ref, xn_ref, meta_ref, cnt_ref, carry_scr, *, n_experts):
    i = pl.program_id(0)

    @pl.when(i == 0)
    def _():
        carry_scr[...] = jnp.zeros_like(carry_scr)

    xn = _rmsnorm(x_ref[...], g_ref[...])
    xn_ref[...] = xn
    xh = xn.astype(BF16)
    xl = (xn - xh.astype(F32)).astype(BF16)
    r = r_ref[...]
    rh = r.astype(BF16)
    rl = (r - rh.astype(F32)).astype(BF16)
    logits = (jnp.dot(xh, rh, preferred_element_type=F32) + jnp.dot(xl, rh, preferred_element_type=F32)
              + jnp.dot(xh, rl, preferred_element_type=F32))
    lane = lax.broadcasted_iota(jnp.int32, logits.shape, 1).astype(F32)
    neg = jnp.float32(-jnp.inf)
    logits = jnp.where(lane < n_experts, logits, neg)
    v1 = jnp.max(logits, axis=-1, keepdims=True)
    i1 = jnp.min(jnp.where(logits == v1, lane, float(LANES)), axis=-1, keepdims=True)
    rest = jnp.where(lane == i1, neg, logits)
    v2 = jnp.max(rest, axis=-1, keepdims=True)
    i2 = jnp.min(jnp.where(rest == v2, lane, float(LANES)), axis=-1, keepdims=True)
    e2 = jnp.exp(v2 - v1)
    den = 1.0 + e2
    hit = jnp.where(lane == i1, 1.0, jnp.where(lane == i2, 1.0, 0.0))
    before = jnp.dot(tri_ref[...], hit.astype(BF16), preferred_element_type=F32) + carry_scr[0:1, :]
    r1 = jnp.sum(jnp.where(lane == i1, before, 0.0), axis=-1, keepdims=True)
    r2 = jnp.sum(jnp.where(lane == i2, before, 0.0), axis=-1, keepdims=True)
    cols = (i1, i2, r1, r2, 1.0 / den, e2 / den)
    meta = jnp.zeros_like(logits)
    for c, val in enumerate(cols):
        meta = jnp.where(lane == c, val, meta)
    meta_ref[...] = meta
    carry_scr[...] = carry_scr[...] + jnp.sum(hit, axis=0, keepdims=True)
    cnt_ref[...] = carry_scr[...]


def _router(x, g, router):
    t, d = x.shape
    ne = router.shape[1]
    assert TOP_K == 2 and ne <= LANES
    rpad = jnp.pad(router, ((0, 0), (0, LANES - ne)))
    tri = jnp.tril(jnp.ones((TM, TM), BF16), -1)
    return pl.pallas_call(
        functools.partial(_router_kernel, n_experts=ne),
        grid=(t // TM,),
        in_specs=[pl.BlockSpec((TM, d), lambda i: (i, 0)), _const((1, d)), _const(rpad.shape),
                  _const(tri.shape)],
        out_specs=[pl.BlockSpec((TM, d), lambda i: (i, 0)), pl.BlockSpec((TM, LANES), lambda i: (i, 0)),
                   pl.BlockSpec((8, LANES), lambda i: (0, 0))],
        out_shape=[jax.ShapeDtypeStruct((t, d), F32), jax.ShapeDtypeStruct((t, LANES), F32),
                   jax.ShapeDtypeStruct((8, LANES), F32)],
        scratch_shapes=[pltpu.VMEM((8, LANES), F32)],
        compiler_params=_params("arbitrary"),
        name="router",
    )(x, g, rpad, tri)


def _dispatch_kernel(dst_ref, fill_ref, nt_ref, xn_ref, xs_ref, z_scr, zsem, sem, *, min_tiles):
    i = pl.program_id(0)
    rows = xn_ref.shape[0]
    last = xs_ref.shape[0] // MOE_TM - 1

    @pl.when(i == 0)
    def _():
        z_scr[...] = jnp.zeros_like(z_scr)
        fills = [pltpu.make_async_copy(
            z_scr, xs_ref.at[pl.ds(pl.multiple_of(fill_ref[e], SUBLANES), MOE_TM)], zsem)
            for e in range(fill_ref.shape[0])]
        for c in fills:
            c.start()
        for c in fills:
            c.wait()
        for j in range(last + 1 - min_tiles):
            tile = nt_ref[0] + j

            @pl.when(tile <= last)
            def _():
                c = pltpu.make_async_copy(
                    z_scr, xs_ref.at[pl.ds(pl.multiple_of(tile * MOE_TM, MOE_TM), MOE_TM)], zsem)
                c.start()
                c.wait()

    def body(r, carry):
        src = xn_ref.at[pl.ds(r, 1)]
        for k in range(TOP_K):
            pltpu.make_async_copy(src, xs_ref.at[pl.ds(dst_ref[TOP_K * r + k], 1)], sem).start(priority=k)
        return carry

    lax.fori_loop(0, rows, body, 0, unroll=8)
    for _ in range(TOP_K):
        pltpu.make_async_copy(xn_ref, xs_ref.at[pl.ds(0, rows)], sem).wait()


def _dispatch(dst, fill, nt, xn, n_slot_tiles):
    t, d = xn.shape
    return pl.pallas_call(
        functools.partial(_dispatch_kernel, min_tiles=(t * TOP_K) // MOE_TM),
        grid=(t // DC_TM,),
        in_specs=[pl.BlockSpec((TOP_K * DC_TM,), lambda i: (i,), memory_space=pltpu.SMEM),
                  pl.BlockSpec(memory_space=pltpu.SMEM), pl.BlockSpec(memory_space=pltpu.SMEM),
                  pl.BlockSpec((DC_TM, d), lambda i: (i, 0))],
        out_specs=pl.BlockSpec(memory_space=pl.ANY),
        out_shape=jax.ShapeDtypeStruct(((n_slot_tiles + 1) * MOE_TM, d), F32),
        scratch_shapes=[pltpu.VMEM((MOE_TM, d), F32), pltpu.SemaphoreType.DMA(()),
                        pltpu.SemaphoreType.DMA(())],
        compiler_params=_params("arbitrary"),
        name="dispatch",
    )(dst, fill, nt, xn)


def _moe_kernel(te_ref, tf_ref, nt_ref, xs_ref, w1_hbm, w3_hbm, w2_hbm, ys_ref,
                s1, s3, s2, w1_scr, w3_scr, w2_scr, sem, *, layer, n_tiles):
    i = pl.program_id(0)
    dff = w1_scr.shape[1]
    nf = dff // MOE_TF
    valid = i < nt_ref[0]
    first = valid & (tf_ref[i] == 1)
    nxt = jnp.minimum(i + 1, n_tiles - 1)
    next_first = (i + 1 < nt_ref[0]) & (tf_ref[nxt] == 1)

    def chunk(e, f, slot):
        cols = pl.ds(f * MOE_TF, MOE_TF)
        return (pltpu.make_async_copy(w1_hbm.at[layer, e, :, cols], s1.at[slot], sem.at[slot]),
                pltpu.make_async_copy(w3_hbm.at[layer, e, :, cols], s3.at[slot], sem.at[slot]),
                pltpu.make_async_copy(w2_hbm.at[layer, e, cols, :], s2.at[slot], sem.at[slot]))

    def start(e, f, slot):
        for c in chunk(e, f, slot):
            c.start()

    def swiglu(xb, lo, hi):
        h1 = jnp.dot(xb, w1_scr[:, lo:hi], preferred_element_type=F32)
        h3 = jnp.dot(xb, w3_scr[:, lo:hi], preferred_element_type=F32)
        return _bdot(jax.nn.silu(h1) * h3, w2_scr[lo:hi, :])

    @pl.when(first & (i == 0))
    def _():
        start(te_ref[i], 0, 0)

    @pl.when(first)
    def _():
        xb = xs_ref[...].astype(BF16)
        acc = None
        for f in range(nf):
            slot = f % 2
            for c in chunk(te_ref[i], f, slot):
                c.wait()
            if f + 1 < nf:
                start(te_ref[i], f + 1, 1 - slot)
            lo, hi = f * MOE_TF, (f + 1) * MOE_TF
            w1_scr[:, lo:hi] = s1[slot].astype(BF16)
            w3_scr[:, lo:hi] = s3[slot].astype(BF16)
            w2_scr[lo:hi, :] = s2[slot].astype(BF16)
            y = swiglu(xb, lo, hi)
            acc = y if acc is None else acc + y
        ys_ref[...] = acc

        @pl.when(next_first)
        def _():
            start(te_ref[nxt], 0, 0)

    @pl.when(valid & jnp.logical_not(first))
    def _():
        @pl.when(next_first)
        def _():
            start(te_ref[nxt], 0, 0)

        xb = xs_ref[...].astype(BF16)
        acc = None
        for lo in range(0, dff, MOE_CHUNK):
            y = swiglu(xb, lo, min(lo + MOE_CHUNK, dff))
            acc = y if acc is None else acc + y
        ys_ref[...] = acc

    @pl.when(jnp.logical_not(valid))
    def _():
        ys_ref[...] = jnp.zeros_like(ys_ref)


def _moe(te, tf, nt, xs, w1, w3, w2, layer, n_tiles):
    d = xs.shape[1]
    dff = w1.shape[3]
    tile = pl.BlockSpec((MOE_TM, d), lambda i, te, tf, nt: (i, 0))
    used = pl.BlockSpec((MOE_TM, d), lambda i, te, tf, nt: (jnp.minimum(i, nt[0] - 1), 0))
    hbm = pl.BlockSpec(memory_space=pl.ANY)
    return pl.pallas_call(
        functools.partial(_moe_kernel, layer=layer, n_tiles=n_tiles),
        grid_spec=pltpu.PrefetchScalarGridSpec(
            num_scalar_prefetch=3,
            grid=(n_tiles,),
            in_specs=[used, hbm, hbm, hbm],
            out_specs=tile,
            scratch_shapes=[pltpu.VMEM((2, d, MOE_TF), F32), pltpu.VMEM((2, d, MOE_TF), F32),
                            pltpu.VMEM((2, MOE_TF, d), F32),
                            pltpu.VMEM((d, dff), BF16), pltpu.VMEM((d, dff), BF16),
                            pltpu.VMEM((dff, d), BF16), pltpu.SemaphoreType.DMA((2,))]),
        out_shape=jax.ShapeDtypeStruct((n_tiles * MOE_TM, d), F32),
        compiler_params=_params("arbitrary"),
        name="moe",
    )(te, tf, nt, xs, w1, w3, w2)


def _combined(dst_ref, x_ref, meta_ref, ys_ref, a_scr, b_scr, sem):
    def body(r, carry):
        pltpu.make_async_copy(ys_ref.at[pl.ds(dst_ref[TOP_K * r], 1)], a_scr.at[pl.ds(r, 1)],
                              sem.at[0]).start(priority=0)
        pltpu.make_async_copy(ys_ref.at[pl.ds(dst_ref[TOP_K * r + 1], 1)], b_scr.at[pl.ds(r, 1)],
                              sem.at[1]).start(priority=1)
        return carry

    rows = x_ref.shape[0]
    lax.fori_loop(0, rows, body, 0, unroll=8)
    pltpu.make_async_copy(ys_ref.at[pl.ds(0, rows)], a_scr, sem.at[0]).wait()
    pltpu.make_async_copy(ys_ref.at[pl.ds(0, rows)], b_scr, sem.at[1]).wait()
    meta = meta_ref[...]
    return x_ref[...] + (meta[:, 4:5] * a_scr[...] + meta[:, 5:6] * b_scr[...])


def _combine_kernel(dst_ref, x_ref, meta_ref, ys_ref, o_ref, a_scr, b_scr, sem):
    o_ref[...] = _combined(dst_ref, x_ref, meta_ref, ys_ref, a_scr, b_scr, sem)


def _combine_final_kernel(dst_ref, x_ref, meta_ref, ys_ref, g_ref, op_ref, os_ref, a_scr, b_scr, sem, *, npt):
    i = pl.program_id(0)
    y = _rmsnorm(_combined(dst_ref, x_ref, meta_ref, ys_ref, a_scr, b_scr, sem), g_ref[...])

    @pl.when(i < npt)
    def _():
        op_ref[...] = y

    @pl.when(i == npt)
    def _():
        os_ref[...] = y


def _combine(dst, x, meta, ys):
    t, d = x.shape
    tile = pl.BlockSpec((DC_TM, d), lambda i: (i, 0))
    return pl.pallas_call(
        _combine_kernel,
        grid=(t // DC_TM,),
        in_specs=[pl.BlockSpec((TOP_K * DC_TM,), lambda i: (i,), memory_space=pltpu.SMEM), tile,
                  pl.BlockSpec((DC_TM, LANES), lambda i: (i, 0)), pl.BlockSpec(memory_space=pl.ANY)],
        out_specs=tile,
        out_shape=jax.ShapeDtypeStruct((t, d), F32),
        scratch_shapes=[pltpu.VMEM((DC_TM, d), F32), pltpu.VMEM((DC_TM, d), F32),
                        pltpu.SemaphoreType.DMA((2,))],
        compiler_params=_params("arbitrary"),
        name="combine",
    )(dst, x, meta, ys)


def _combine_final(dst, x, meta, ys, g, npt):
    t, d = x.shape
    return pl.pallas_call(
        functools.partial(_combine_final_kernel, npt=npt),
        grid=(npt + 1,),
        in_specs=[pl.BlockSpec((TOP_K * TM,), lambda i: (i,), memory_space=pltpu.SMEM),
                  pl.BlockSpec((TM, d), lambda i: (i, 0)), pl.BlockSpec((TM, LANES), lambda i: (i, 0)),
                  pl.BlockSpec(memory_space=pl.ANY), _const((1, d))],
        out_specs=[pl.BlockSpec((TM, d), lambda i: (jnp.minimum(i, npt - 1), 0)),
                   pl.BlockSpec((TM, d), lambda i: (0, 0))],
        out_shape=[jax.ShapeDtypeStruct((npt * TM, d), F32), jax.ShapeDtypeStruct((TM, d), F32)],
        scratch_shapes=[pltpu.VMEM((TM, d), F32), pltpu.VMEM((TM, d), F32), pltpu.SemaphoreType.DMA((2,))],
        compiler_params=_params("arbitrary"),
        name="combine_final",
    )(dst, x, meta, ys, g)


def _moe_ffn(x, g, router, w1, w3, w2, layer, final_g=None, npt=None):
    t, d = x.shape
    ne = router.shape[1]
    assert t % DC_TM == 0 and (TOP_K * DC_TM) % 1024 == 0
    xn, meta, cnt = _router(x, g, router)
    n = cnt[0, :ne].astype(jnp.int32)
    tiles = (n + MOE_TM - 1) // MOE_TM
    ends = jnp.cumsum(tiles)
    n_tiles = (t * TOP_K) // MOE_TM + ne
    off = (ends - tiles) * MOE_TM
    e = meta[:, :TOP_K].astype(jnp.int32)
    dst = (jnp.take(off, e) + meta[:, TOP_K:2 * TOP_K].astype(jnp.int32)).reshape(-1)
    tid = jnp.minimum(jnp.arange(n_tiles, dtype=jnp.int32), ends[-1] - 1)
    te = jnp.sum((tid[:, None] >= ends[None, :]).astype(jnp.int32), axis=1)
    first = (tid == jnp.take(ends - tiles, te)) & (jnp.arange(n_tiles) < ends[-1])
    fill = (off + n) // SUBLANES * SUBLANES
    nt = ends[-1:].astype(jnp.int32)
    xs = _dispatch(dst, fill, nt, xn, n_tiles)
    ys = _moe(te, first.astype(jnp.int32), nt, xs, w1, w3, w2, layer, n_tiles)
    if final_g is not None:
        return _combine_final(dst, x, meta, ys, final_g, npt)
    return _combine(dst, x, meta, ys)


def _final_kernel(x_ref, g_ref, op_ref, os_ref, *, npt):
    i = pl.program_id(0)
    y = _rmsnorm(x_ref[...], g_ref[...])

    @pl.when(i < npt)
    def _():
        op_ref[...] = y

    @pl.when(i == npt)
    def _():
        os_ref[...] = y


def _final_norm(x, g, *, npt):
    t, d = x.shape
    return pl.pallas_call(
        functools.partial(_final_kernel, npt=npt),
        grid=(npt + 1,),
        in_specs=[pl.BlockSpec((TM, d), lambda i: (i, 0)), _const((1, d))],
        out_specs=[pl.BlockSpec((TM, d), lambda i: (jnp.minimum(i, npt - 1), 0)),
                   pl.BlockSpec((TM, d), lambda i: (0, 0))],
        out_shape=[jax.ShapeDtypeStruct((npt * TM, d), F32), jax.ShapeDtypeStruct((TM, d), F32)],
        compiler_params=_params("arbitrary"),
        name="final_norm",
    )(x, g)


def kernel(x_prompt, x_sample, cache_mem_k, cache_mem_v, state_conv, state_pool, mem_prompt, norm_mix, norm_xa, norm_ffn, norm_mem, norm_final, a_w_in, a_ln_g, a_ln_b, a_w_s, a_b_s, a_w_out, b_w_in, b_conv, b_w_out, c_w_in, c_w_grp, c_scale, c_w_out, xa_wq, xa_wk, xa_wv, xa_wo, f_w1, f_w3, f_w2, m_router, m_w1, m_w3, m_w2):
    nb, seq, d = x_prompt.shape
    bs, ls, _ = x_sample.shape
    depth = norm_mix.shape[0]
    nm = mem_prompt.shape[1]
    assert seq % TM == 0 and TM == bs * ls and (nb * nm) % TM == 0
    tps = seq // TM
    npt = nb * tps
    pos0 = PAST_LEN
    dims = dict(npt=npt, bs=bs, ls=ls)
    bf = lambda w: w.astype(BF16)
    row = lambda v: v.reshape(1, d)
    to_pm = lambda a: a.transpose(1, 0, 2).reshape(-1, d)
    from_pm = lambda a, r: a.reshape(r, bs, d).transpose(1, 0, 2)

    mem_k, mem_v, mem_k5, mem_v5 = _mem_kv(mem_prompt.reshape(nb * nm, d), norm_mem[:, None, :],
                                           bf(xa_wk), bf(xa_wv), nm)
    kp = mem_k.reshape(depth, nb, nm, d)
    vp = mem_v.reshape(depth, nb, nm, d)
    ks, vs = cache_mem_k, cache_mem_v

    x = None
    conv_p, conv_s, pool_p, pool_s, chunk_v = [], [], [], [], []
    for i in range(depth):
        kind, j = i % 3, i // 3
        g = row(norm_mix[i])
        if kind == 0:
            srcs = (x_prompt.reshape(nb * seq, d), to_pm(x_sample), 0) if x is None else (x, x, npt)
            x, v_s = _mixer_a(*srcs, g, bf(a_w_in[j]), row(a_ln_g[j]), row(a_ln_b[j]), a_w_s[j], a_b_s[j],
                              bf(a_w_out[j]), **dims)
            chunk_v.append(from_pm(v_s, ls))
        elif kind == 1:
            x, tail, z_s = _mixer_b(x, g, bf(b_w_in[j]), b_conv[j], to_pm(state_conv[j]), bf(b_w_out[j]),
                                    npt=npt, tps=tps, bs=bs)
            conv_p.append(tail[:, 6:, :])
            conv_s.append(from_pm(z_s, 2))
        else:
            x, tail, h_s = _mixer_c(x, g, bf(c_w_in[j]), to_pm(state_pool[j]), bf(c_w_grp[j]),
                                    row(c_scale[j]), bf(c_w_out[j]), tps=tps, pos0=pos0, **dims)
            pool_p.append(tail[:, 1:, :])
            pool_s.append(jnp.concatenate([state_pool[j], from_pm(h_s, ls)], axis=1)[:, -POOL_CTX:, :])
        x = _xattn(x, row(norm_xa[i]), bf(xa_wq[i]), bf(xa_wo[i]), kp, vp, ks, vs, i, tps=tps, **dims)
        g = row(norm_ffn[i])
        if i % 2 == 0:
            x = _ffn(x, g, bf(f_w1[i // 2]), bf(f_w3[i // 2]), bf(f_w2[i // 2]))
        elif i < depth - 1:
            x = _moe_ffn(x, g, m_router[i // 2], m_w1, m_w3, m_w2, i // 2)
        else:
            x = _moe_ffn(x, g, m_router[i // 2], m_w1, m_w3, m_w2, i // 2, row(norm_final), npt)
    y_p, y_s = x if isinstance(x, (tuple, list)) else _final_norm(x, row(norm_final), npt=npt)

    return (y_p.reshape(nb, seq, d), from_pm(y_s, ls), mem_k5, mem_v5,
            jnp.stack(conv_p), jnp.stack(pool_p), jnp.stack(conv_s), jnp.stack(pool_s), jnp.stack(chunk_v))
```

```python
import functools

import jax
import jax.numpy as jnp
from jax import lax
from jax.experimental import pallas as pl
from jax.experimental.pallas import tpu as pltpu

F32 = jnp.float32
BF16 = jnp.bfloat16

EPS = 1e-6
TM = 512
CHUNK = 128
A_GROUPS = 8
POOL_WINDOWS = (2, 4, 8, 16)
POOL_CTX = max(POOL_WINDOWS) - 1
X_HEADS = 4
TOP_K = 2
PAST_LEN = 16384
LANES = 128
SUBLANES = 8
DC_TM = 1536
MOE_TM = 512
MOE_TF = 512
MOE_CHUNK = 1024
VMEM_LIMIT = 60 * 1024 * 1024


def _params(*sem):
    return pltpu.CompilerParams(dimension_semantics=sem, vmem_limit_bytes=VMEM_LIMIT)


def _const(shape):
    nd = len(shape)
    return pl.BlockSpec(shape, lambda *_: (0,) * nd, pipeline_mode=pl.Buffered(1))


def _rmsnorm(x, g):
    r = lax.rsqrt(jnp.mean(x * x, axis=-1, keepdims=True) + EPS)
    return (x * r) * g


def _bdot(a, w):
    return jnp.dot(a.astype(BF16), w, preferred_element_type=F32)


def _mem_kv_kernel(m_ref, g_ref, wk_ref, wv_ref, k_ref, v_ref, k5_ref, v5_ref):
    mn = _rmsnorm(m_ref[...], g_ref[...]).astype(BF16)
    nbt, nm, nh, hd = k5_ref.shape
    for w_ref, o_ref, o5_ref in ((wk_ref, k_ref, k5_ref), (wv_ref, v_ref, v5_ref)):
        y = jnp.dot(mn, w_ref[...], preferred_element_type=F32)
        o_ref[...] = y.astype(BF16)
        for b in range(nbt):
            for h in range(nh):
                o5_ref[b, :, h, :] = y[b * nm:(b + 1) * nm, h * hd:(h + 1) * hd]


def _mem_kv(mem, g, wk, wv, nm):
    rows, d = mem.shape
    depth = g.shape[0]
    hd = d // X_HEADS
    out = jax.ShapeDtypeStruct((depth, rows, d), BF16)
    out5 = jax.ShapeDtypeStruct((depth, rows // nm, nm, X_HEADS, hd), F32)
    wspec = pl.BlockSpec((None, d, d), lambda l, r: (l, 0, 0))
    ospec = pl.BlockSpec((None, TM, d), lambda l, r: (l, r, 0))
    o5spec = pl.BlockSpec((None, TM // nm, nm, X_HEADS, hd), lambda l, r: (l, r, 0, 0, 0))
    return pl.pallas_call(
        _mem_kv_kernel,
        grid=(depth, rows // TM),
        in_specs=[pl.BlockSpec((TM, d), lambda l, r: (r, 0)),
                  pl.BlockSpec((None, 1, d), lambda l, r: (l, 0, 0)),
                  wspec, wspec],
        out_specs=[ospec, ospec, o5spec, o5spec],
        out_shape=[out, out, out5, out5],
        compiler_params=_params("arbitrary", "arbitrary"),
        name="mem_kv",
    )(mem, g, wk, wv)


def _mixer_a_kernel(x_ref, xt_ref, g_ref, win_ref, lng_ref, lnb_ref, ws_ref, bs_ref, sc_ref, sb_ref, wout_ref,
                    o_ref, vs_ref, u_scr, v_scr, s_scr, *, npt, bs, ls):
    i = pl.program_id(0)
    d = x_ref.shape[1]
    x = jnp.where(i == npt, xt_ref[...], x_ref[...])
    z = jax.nn.gelu(_bdot(_rmsnorm(x, g_ref[...]), win_ref[...]))
    u_scr[...] = z[:, :d]
    v = z[:, d:]
    mu = jnp.mean(v, axis=-1, keepdims=True)
    var = jnp.mean(jnp.square(v - mu), axis=-1, keepdims=True)
    v_scr[...] = ((v - mu) * lax.rsqrt(var + EPS)) * lng_ref[...] + lnb_ref[...]
    gd = d // A_GROUPS

    @pl.when(i < npt)
    def _():
        rows = lax.broadcasted_iota(jnp.int32, (CHUNK, CHUNK), 0)
        cols = lax.broadcasted_iota(jnp.int32, (CHUNK, CHUNK), 1)
        for g in range(A_GROUPS):
            wg = jnp.where(cols <= rows, ws_ref[g], 0.0).astype(BF16)
            bg = bs_ref[g]
            for c in range(TM // CHUNK):
                vc = v_scr[c * CHUNK:(c + 1) * CHUNK, g * gd:(g + 1) * gd]
                s_scr[c * CHUNK:(c + 1) * CHUNK, g * gd:(g + 1) * gd] = (
                    jnp.dot(wg, vc.astype(BF16), preferred_element_type=F32) + bg)

    @pl.when(i == npt)
    def _():
        for t in range(ls):
            acc = jnp.broadcast_to(sb_ref[t:t + 1, :], (bs, d))
            for s in range(t + 1):
                acc = acc + sc_ref[t * ls + s:t * ls + s + 1, :] * v_scr[s * bs:(s + 1) * bs, :]
            s_scr[t * bs:(t + 1) * bs, :] = acc
        vs_ref[...] = v_scr[...]

    o_ref[...] = x + _bdot(u_scr[...] * s_scr[...], wout_ref[...])


def _mixer_a(x, x_tail, tail_blk, g, w_in, ln_g, ln_b, w_s, b_s, w_out, *, npt, bs, ls):
    d = x.shape[1]
    t = (npt + 1) * TM
    assert TM == bs * ls and TM % CHUNK == 0 and ls <= CHUNK
    gd = d // A_GROUPS
    sc = jnp.repeat(w_s[:, :ls, :ls].transpose(1, 2, 0).reshape(ls * ls, A_GROUPS), gd, axis=1)
    sb = jnp.repeat(b_s[:, :ls].T, gd, axis=1)
    tile = pl.BlockSpec((TM, d), lambda i: (i, 0))
    return pl.pallas_call(
        functools.partial(_mixer_a_kernel, npt=npt, bs=bs, ls=ls),
        grid=(npt + 1,),
        in_specs=[pl.BlockSpec((TM, d), lambda i: (jnp.minimum(i, npt - 1), 0)),
                  pl.BlockSpec((TM, d), lambda i: (tail_blk, 0), pipeline_mode=pl.Buffered(1)),
                  _const((1, d)), _const(w_in.shape), _const((1, d)), _const((1, d)),
                  _const(w_s.shape), _const((A_GROUPS, CHUNK, 1)), _const(sc.shape), _const(sb.shape),
                  _const(w_out.shape)],
        out_specs=[tile, pl.BlockSpec((TM, d), lambda i: (0, 0))],
        out_shape=[jax.ShapeDtypeStruct((t, d), F32), jax.ShapeDtypeStruct((TM, d), F32)],
        scratch_shapes=[pltpu.VMEM((TM, d), F32)] * 3,
        compiler_params=_params("arbitrary"),
        name="mixer_a",
    )(x, x_tail, g, w_in, ln_g, ln_b, w_s, b_s[:, :, None], sc, sb, w_out)


def _mixer_b_kernel(x_ref, g_ref, win_ref, cw_ref, pre_ref, wout_ref,
                    o_ref, tail_ref, zs_ref, conv_scr, carry_scr, *, npt, tps, bs):
    i = pl.program_id(0)
    d = x_ref.shape[1]
    x = x_ref[...]
    p = _bdot(_rmsnorm(x, g_ref[...]), win_ref[...])
    gate_b = p[:, :d]
    zc = p[:, d:2 * d] * p[:, 2 * d:]
    w0, w1, w2 = cw_ref[0:1, :], cw_ref[1:2, :], cw_ref[2:3, :]

    @pl.when(i == 0)
    def _():
        carry_scr[...] = jnp.zeros_like(carry_scr)

    @pl.when(i < npt)
    def _():
        keep = (i % tps) != 0
        c6 = jnp.where(keep, carry_scr[6:7, :], 0.0)
        c7 = jnp.where(keep, carry_scr[7:8, :], 0.0)
        rows = lax.broadcasted_iota(jnp.int32, (TM, 1), 0)
        sh1 = jnp.where(rows == 0, c7, pltpu.roll(zc, 1, 0))
        sh2 = jnp.where(rows == 0, c6, jnp.where(rows == 1, c7, pltpu.roll(zc, 2, 0)))
        conv_scr[...] = w0 * sh2 + w1 * sh1 + w2 * zc
        carry_scr[...] = zc[TM - 8:, :]
        tail_ref[...] = zc[TM - 8:, :]

    @pl.when(i == npt)
    def _():
        p0, p1 = pre_ref[0:bs, :], pre_ref[bs:2 * bs, :]
        sh1 = jnp.concatenate([p1, zc[:TM - bs]], axis=0)
        sh2 = jnp.concatenate([p0, p1, zc[:TM - 2 * bs]], axis=0)
        conv_scr[...] = w0 * sh2 + w1 * sh1 + w2 * zc
        zs_ref[...] = zc[TM - 2 * bs:, :]

    o_ref[...] = x + _bdot(gate_b * conv_scr[...], wout_ref[...])


def _mixer_b(x, g, w_in, conv_w, prefix, w_out, *, npt, tps, bs):
    t, d = x.shape
    nb = npt // tps
    tile = pl.BlockSpec((TM, d), lambda i: (i, 0))
    return pl.pallas_call(
        functools.partial(_mixer_b_kernel, npt=npt, tps=tps, bs=bs),
        grid=(npt + 1,),
        in_specs=[tile, _const((1, d)), _const(w_in.shape), _const(conv_w.shape), _const(prefix.shape),
                  _const(w_out.shape)],
        out_specs=[tile,
                   pl.BlockSpec((None, 8, d), lambda i: (jnp.minimum(i // tps, nb - 1), 0, 0)),
                   pl.BlockSpec((2 * bs, d), lambda i: (0, 0))],
        out_shape=[jax.ShapeDtypeStruct((t, d), F32), jax.ShapeDtypeStruct((nb, 8, d), F32),
                   jax.ShapeDtypeStruct((2 * bs, d), F32)],
        scratch_shapes=[pltpu.VMEM((TM, d), F32), pltpu.VMEM((8, d), F32)],
        compiler_params=_params("arbitrary"),
        name="mixer_b",
    )(x, g, w_in, conv_w, prefix, w_out)


def _mixer_c_kernel(x_ref, g_ref, win_ref, pre_ref, wgrp_ref, scale_ref, wout_ref,
                    o_ref, tail_ref, hs_ref, d_scr, hist_scr, *, npt, tps, bs, ls, pos0):
    i = pl.program_id(0)
    d = x_ref.shape[1]
    gd = d // len(POOL_WINDOWS)
    hp = POOL_CTX + 1
    x = x_ref[...]
    h = _bdot(_rmsnorm(x, g_ref[...]), win_ref[...])

    @pl.when(i == 0)
    def _():
        hist_scr[...] = jnp.zeros_like(hist_scr)

    @pl.when(i < npt)
    def _():
        keep = (i % tps) != 0
        ext = jnp.concatenate([jnp.where(keep, hist_scr[...], 0.0), h], axis=0)
        pos = (i % tps) * TM + lax.broadcasted_iota(jnp.int32, (TM, 1), 0)
        sums = ext
        width = 1
        for g, w in enumerate(POOL_WINDOWS):
            while width < w:
                sums = sums + pltpu.roll(sums, width, 0)
                width *= 2
            cnt = jnp.minimum(w, pos + 1).astype(F32)
            sl = slice(g * gd, (g + 1) * gd)
            d_scr[:, sl] = sums[hp:, sl] / cnt - h[:, sl]
        hist_scr[...] = h[TM - hp:, :]
        tail_ref[...] = h[TM - hp:, :]

    @pl.when(i == npt)
    def _():
        for g, w in enumerate(POOL_WINDOWS):
            sl = slice(g * gd, (g + 1) * gd)
            run = jnp.zeros((bs, gd), F32)
            tails = [run]
            for m in range(1, w):
                run = run + pre_ref[(POOL_CTX - m) * bs:(POOL_CTX - m + 1) * bs, sl]
                tails.append(run)
            for t in range(ls):
                n = min(w, t + 1)
                acc = tails[w - n]
                for j in range(n):
                    acc = acc + h[(t - j) * bs:(t - j + 1) * bs, sl]
                cnt = float(min(w, pos0 + t + 1))
                d_scr[t * bs:(t + 1) * bs, sl] = acc / cnt - h[t * bs:(t + 1) * bs, sl]
        hs_ref[...] = h

    m = jnp.concatenate(
        [_bdot(d_scr[:, g * gd:(g + 1) * gd], wgrp_ref[g]) for g in range(len(POOL_WINDOWS))], axis=1)
    o_ref[...] = x + _bdot(m * scale_ref[...], wout_ref[...])


def _mixer_c(x, g, w_in, prefix, w_grp, scale, w_out, *, npt, tps, bs, ls, pos0):
    t, d = x.shape
    nb = npt // tps
    hp = POOL_CTX + 1
    tile = pl.BlockSpec((TM, d), lambda i: (i, 0))
    return pl.pallas_call(
        functools.partial(_mixer_c_kernel, npt=npt, tps=tps, bs=bs, ls=ls, pos0=pos0),
        grid=(npt + 1,),
        in_specs=[tile, _const((1, d)), _const(w_in.shape), _const(prefix.shape), _const(w_grp.shape),
                  _const((1, d)), _const(w_out.shape)],
        out_specs=[tile,
                   pl.BlockSpec((None, hp, d), lambda i: (jnp.minimum(i // tps, nb - 1), 0, 0)),
                   pl.BlockSpec((TM, d), lambda i: (0, 0))],
        out_shape=[jax.ShapeDtypeStruct((t, d), F32), jax.ShapeDtypeStruct((nb, hp, d), F32),
                   jax.ShapeDtypeStruct((TM, d), F32)],
        scratch_shapes=[pltpu.VMEM((TM, d), F32), pltpu.VMEM((hp, d), F32)],
        compiler_params=_params("arbitrary"),
        name="mixer_c",
    )(x, g, w_in, prefix, w_grp, scale, w_out)


def _softmax_rows(s):
    e = jnp.exp(s - jnp.max(s, axis=-1, keepdims=True))
    return e / jnp.sum(e, axis=-1, keepdims=True)


def _xattn_kernel(x_ref, xs_ref, g_ref, wq_ref, wo_ref, kp_ref, vp_ref, ks_ref, vs_ref,
                  o_ref, q_scr, o_scr, *, npt, bs, ls, spb):
    i = pl.program_id(0)
    d = x_ref.shape[1]
    hd = d // X_HEADS
    qscale = float(hd) ** -0.5
    nt = (((1,), (1,)), ((), ()))

    @pl.when(i == 0)
    def _():
        q_scr[...] = _bdot(_rmsnorm(xs_ref[...], g_ref[...]), wq_ref[...]) * qscale

    @pl.when(i < npt)
    def _():
        x = x_ref[...]
        q = (_bdot(_rmsnorm(x, g_ref[...]), wq_ref[...]) * qscale).astype(BF16)
        k = kp_ref[...]
        v = vp_ref[...]
        heads = []
        for h in range(X_HEADS):
            sl = slice(h * hd, (h + 1) * hd)
            s = lax.dot_general(q[:, sl], k[:, sl], nt, preferred_element_type=F32)
            heads.append(_bdot(_softmax_rows(s), v[:, sl]))
        o_ref[...] = x + _bdot(jnp.concatenate(heads, axis=1), wo_ref[...])

        nm = ks_ref.shape[1]
        rows = lax.broadcasted_iota(jnp.int32, (X_HEADS * ls, nm * X_HEADS), 0)
        cols = lax.broadcasted_iota(jnp.int32, (X_HEADS * ls, nm * X_HEADS), 1)
        own = (rows // ls) == (cols % X_HEADS)
        for bb in range(spb):
            b = i * spb + bb
            qb = jnp.concatenate([q_scr[pl.ds(t * bs + b, 1), :] for t in range(ls)], axis=0)
            qh = jnp.concatenate([qb[:, h * hd:(h + 1) * hd] for h in range(X_HEADS)], axis=0)
            kr = ks_ref[bb].reshape(nm * X_HEADS, hd).astype(BF16)
            vr = vs_ref[bb].reshape(nm * X_HEADS, hd).astype(BF16)
            s = lax.dot_general(qh.astype(BF16), kr, nt, preferred_element_type=F32)
            of = _bdot(_softmax_rows(jnp.where(own, s, -jnp.inf)), vr)
            ob = jnp.concatenate([of[h * ls:(h + 1) * ls] for h in range(X_HEADS)], axis=1)
            for t in range(ls):
                o_scr[pl.ds(t * bs + b, 1), :] = ob[t:t + 1]

    @pl.when(i == npt)
    def _():
        o_ref[...] = x_ref[...] + _bdot(o_scr[...], wo_ref[...])


def _xattn(x, g, wq, wo, kp, vp, ks, vs, layer, *, npt, tps, bs, ls):
    t, d = x.shape
    nb, nm = kp.shape[1], kp.shape[2]
    assert bs % npt == 0
    spb = bs // npt
    tile = pl.BlockSpec((TM, d), lambda i: (i, 0))
    pmem = pl.BlockSpec((None, None, nm, d), lambda i: (layer, jnp.minimum(i // tps, nb - 1), 0, 0))
    smem = pl.BlockSpec((None, spb, nm, X_HEADS, d // X_HEADS),
                        lambda i: (layer, jnp.minimum(i, npt - 1), 0, 0, 0))
    return pl.pallas_call(
        functools.partial(_xattn_kernel, npt=npt, bs=bs, ls=ls, spb=spb),
        grid=(npt + 1,),
        in_specs=[tile, pl.BlockSpec((TM, d), lambda i: (npt, 0), pipeline_mode=pl.Buffered(1)),
                  _const((1, d)), _const(wq.shape), _const(wo.shape), pmem, pmem, smem, smem],
        out_specs=tile,
        out_shape=jax.ShapeDtypeStruct((t, d), F32),
        scratch_shapes=[pltpu.VMEM((TM, d), F32), pltpu.VMEM((TM, d), F32)],
        compiler_params=_params("arbitrary"),
        name="xattn",
    )(x, x, g, wq, wo, kp, vp, ks, vs)


def _ffn_kernel(x_ref, g_ref, w1_ref, w3_ref, w2_ref, o_ref, *, chunks):
    x = x_ref[...]
    xn = _rmsnorm(x, g_ref[...]).astype(BF16)
    acc = x
    for lo, hi in chunks:
        h1 = jnp.dot(xn, w1_ref[:, lo:hi], preferred_element_type=F32)
        h3 = jnp.dot(xn, w3_ref[:, lo:hi], preferred_element_type=F32)
        acc = acc + _bdot(jax.nn.silu(h1) * h3, w2_ref[lo:hi, :])
    o_ref[...] = acc


def _ffn(x, g, w1, w3, w2):
    t, d = x.shape
    dff = w1.shape[1]
    step = 1024
    chunks = tuple((lo, min(lo + step, dff)) for lo in range(0, dff, step))
    tile = pl.BlockSpec((TM, d), lambda i: (i, 0))
    return pl.pallas_call(
        functools.partial(_ffn_kernel, chunks=chunks),
        grid=(t // TM,),
        in_specs=[tile, _const((1, d)), _const(w1.shape), _const(w3.shape), _const(w2.shape)],
        out_specs=tile,
        out_shape=jax.ShapeDtypeStruct((t, d), F32),
        compiler_params=_params("arbitrary"),
        name="ffn",
    )(x, g, w1, w3, w2)


def _router_kernel(x_ref, g_ref, r_ref, tri_ref, xn_ref, meta_ref, cnt_ref, carry_scr, *, n_experts):
    i = pl.program_id(0)

    @pl.when(i == 0)
    def _():
        carry_scr[...] = jnp.zeros_like(carry_scr)

    xn = _rmsnorm(x_ref[...], g_ref[...])
    xn_ref[...] = xn
    xh = xn.astype(BF16)
    xl = (xn - xh.astype(F32)).astype(BF16)
    r = r_ref[...]
    rh = r.astype(BF16)
    rl = (r - rh.astype(F32)).astype(BF16)
    logits = (jnp.dot(xh, rh, preferred_element_type=F32) + jnp.dot(xl, rh, preferred_element_type=F32)
              + jnp.dot(xh, rl, preferred_element_type=F32))
    lane = lax.broadcasted_iota(jnp.int32, logits.shape, 1).astype(F32)
    neg = jnp.float32(-jnp.inf)
    logits = jnp.where(lane < n_experts, logits, neg)
    v1 = jnp.max(logits, axis=-1, keepdims=True)
    i1 = jnp.min(jnp.where(logits == v1, lane, float(LANES)), axis=-1, keepdims=True)
    rest = jnp.where(lane == i1, neg, logits)
    v2 = jnp.max(rest, axis=-1, keepdims=True)
    i2 = jnp.min(jnp.where(rest == v2, lane, float(LANES)), axis=-1, keepdims=True)
    e2 = jnp.exp(v2 - v1)
    den = 1.0 + e2
    hit = jnp.where(lane == i1, 1.0, jnp.where(lane == i2, 1.0, 0.0))
    before = jnp.dot(tri_ref[...], hit.astype(BF16), preferred_element_type=F32) + carry_scr[0:1, :]
    r1 = jnp.sum(jnp.where(lane == i1, before, 0.0), axis=-1, keepdims=True)
    r2 = jnp.sum(jnp.where(lane == i2, before, 0.0), axis=-1, keepdims=True)
    cols = (i1, i2, r1, r2, 1.0 / den, e2 / den)
    meta = jnp.zeros_like(logits)
    for c, val in enumerate(cols):
        meta = jnp.where(lane == c, val, meta)
    meta_ref[...] = meta
    carry_scr[...] = carry_scr[...] + jnp.sum(hit, axis=0, keepdims=True)
    cnt_ref[...] = carry_scr[...]


def _router(x, g, router):
    t, d = x.shape
    ne = router.shape[1]
    assert TOP_K == 2 and ne <= LANES
    rpad = jnp.pad(router, ((0, 0), (0, LANES - ne)))
    tri = jnp.tril(jnp.ones((TM, TM), BF16), -1)
    return pl.pallas_call(
        functools.partial(_router_kernel, n_experts=ne),
        grid=(t // TM,),
        in_specs=[pl.BlockSpec((TM, d), lambda i: (i, 0)), _const((1, d)), _const(rpad.shape),
                  _const(tri.shape)],
        out_specs=[pl.BlockSpec((TM, d), lambda i: (i, 0)), pl.BlockSpec((TM, LANES), lambda i: (i, 0)),
                   pl.BlockSpec((8, LANES), lambda i: (0, 0))],
        out_shape=[jax.ShapeDtypeStruct((t, d), F32), jax.ShapeDtypeStruct((t, LANES), F32),
                   jax.ShapeDtypeStruct((8, LANES), F32)],
        scratch_shapes=[pltpu.VMEM((8, LANES), F32)],
        compiler_params=_params("arbitrary"),
        name="router",
    )(x, g, rpad, tri)


def _dispatch_kernel(dst_ref, fill_ref, nt_ref, xn_ref, xs_ref, z_scr, zsem, sem, *, min_tiles):
    i = pl.program_id(0)
    rows = xn_ref.shape[0]
    last = xs_ref.shape[0] // MOE_TM - 1

    @pl.when(i == 0)
    def _():
        z_scr[...] = jnp.zeros_like(z_scr)
        fills = [pltpu.make_async_copy(
            z_scr, xs_ref.at[pl.ds(pl.multiple_of(fill_ref[e], SUBLANES), MOE_TM)], zsem)
            for e in range(fill_ref.shape[0])]
        for c in fills:
            c.start()
        for c in fills:
            c.wait()
        for j in range(last + 1 - min_tiles):
            tile = nt_ref[0] + j

            @pl.when(tile <= last)
            def _():
                c = pltpu.make_async_copy(
                    z_scr, xs_ref.at[pl.ds(pl.multiple_of(tile * MOE_TM, MOE_TM), MOE_TM)], zsem)
                c.start()
                c.wait()

    def body(r, carry):
        src = xn_ref.at[pl.ds(r, 1)]
        for k in range(TOP_K):
            pltpu.make_async_copy(src, xs_ref.at[pl.ds(dst_ref[TOP_K * r + k], 1)], sem).start(priority=k)
        return carry

    lax.fori_loop(0, rows, body, 0, unroll=8)
    for _ in range(TOP_K):
        pltpu.make_async_copy(xn_ref, xs_ref.at[pl.ds(0, rows)], sem).wait()


def _dispatch(dst, fill, nt, xn, n_slot_tiles):
    t, d = xn.shape
    return pl.pallas_call(
        functools.partial(_dispatch_kernel, min_tiles=(t * TOP_K) // MOE_TM),
        grid=(t // DC_TM,),
        in_specs=[pl.BlockSpec((TOP_K * DC_TM,), lambda i: (i,), memory_space=pltpu.SMEM),
                  pl.BlockSpec(memory_space=pltpu.SMEM), pl.BlockSpec(memory_space=pltpu.SMEM),
                  pl.BlockSpec((DC_TM, d), lambda i: (i, 0))],
        out_specs=pl.BlockSpec(memory_space=pl.ANY),
        out_shape=jax.ShapeDtypeStruct(((n_slot_tiles + 1) * MOE_TM, d), F32),
        scratch_shapes=[pltpu.VMEM((MOE_TM, d), F32), pltpu.SemaphoreType.DMA(()),
                        pltpu.SemaphoreType.DMA(())],
        compiler_params=_params("arbitrary"),
        name="dispatch",
    )(dst, fill, nt, xn)


def _moe_kernel(te_ref, tf_ref, nt_ref, xs_ref, w1_hbm, w3_hbm, w2_hbm, ys_ref,
                s1, s3, s2, w1_scr, w3_scr, w2_scr, sem, *, layer, n_tiles):
    i = pl.program_id(0)
    dff = w1_scr.shape[1]
    nf = dff // MOE_TF
    valid = i < nt_ref[0]
    first = valid & (tf_ref[i] == 1)
    nxt = jnp.minimum(i + 1, n_tiles - 1)
    next_first = (i + 1 < nt_ref[0]) & (tf_ref[nxt] == 1)

    def chunk(e, f, slot):
        cols = pl.ds(f * MOE_TF, MOE_TF)
        return (pltpu.make_async_copy(w1_hbm.at[layer, e, :, cols], s1.at[slot], sem.at[slot]),
                pltpu.make_async_copy(w3_hbm.at[layer, e, :, cols], s3.at[slot], sem.at[slot]),
                pltpu.make_async_copy(w2_hbm.at[layer, e, cols, :], s2.at[slot], sem.at[slot]))

    def start(e, f, slot):
        for c in chunk(e, f, slot):
            c.start()

    def prefetch(e):
        start(e, 0, 0)
        start(e, 1, 1)

    def swiglu(xb, lo, hi):
        h1 = jnp.dot(xb, w1_scr[:, lo:hi], preferred_element_type=F32)
        h3 = jnp.dot(xb, w3_scr[:, lo:hi], preferred_element_type=F32)
        return _bdot(jax.nn.silu(h1) * h3, w2_scr[lo:hi, :])

    @pl.when(first & (i == 0))
    def _():
        prefetch(te_ref[i])

    @pl.when(first)
    def _():
        xb = xs_ref[...].astype(BF16)
        acc = None
        for f in range(nf):
            slot = f % 2
            for c in chunk(te_ref[i], f, slot):
                c.wait()
            lo, hi = f * MOE_TF, (f + 1) * MOE_TF
            w1_scr[:, lo:hi] = s1[slot].astype(BF16)
            w3_scr[:, lo:hi] = s3[slot].astype(BF16)
            w2_scr[lo:hi, :] = s2[slot].astype(BF16)
            if f + 2 < nf:
                start(te_ref[i], f + 2, slot)
            y = swiglu(xb, lo, hi)
            acc = y if acc is None else acc + y
        ys_ref[...] = acc

        @pl.when(next_first)
        def _():
            prefetch(te_ref[nxt])

    @pl.when(valid & jnp.logical_not(first))
    def _():
        @pl.when(next_first)
        def _():
            prefetch(te_ref[nxt])

        xb = xs_ref[...].astype(BF16)
        acc = None
        for lo in range(0, dff, MOE_CHUNK):
            y = swiglu(xb, lo, min(lo + MOE_CHUNK, dff))
            acc = y if acc is None else acc + y
        ys_ref[...] = acc

    @pl.when(jnp.logical_not(valid))
    def _():
        ys_ref[...] = jnp.zeros_like(ys_ref)


def _moe(te, tf, nt, xs, w1, w3, w2, layer, n_tiles):
    d = xs.shape[1]
    dff = w1.shape[3]
    tile = pl.BlockSpec((MOE_TM, d), lambda i, te, tf, nt: (i, 0))
    used = pl.BlockSpec((MOE_TM, d), lambda i, te, tf, nt: (jnp.minimum(i, nt[0] - 1), 0))
    hbm = pl.BlockSpec(memory_space=pl.ANY)
    return pl.pallas_call(
        functools.partial(_moe_kernel, layer=layer, n_tiles=n_tiles),
        grid_spec=pltpu.PrefetchScalarGridSpec(
            num_scalar_prefetch=3,
            grid=(n_tiles,),
            in_specs=[used, hbm, hbm, hbm],
            out_specs=tile,
            scratch_shapes=[pltpu.VMEM((2, d, MOE_TF), F32), pltpu.VMEM((2, d, MOE_TF), F32),
                            pltpu.VMEM((2, MOE_TF, d), F32),
                            pltpu.VMEM((d, dff), BF16), pltpu.VMEM((d, dff), BF16),
                            pltpu.VMEM((dff, d), BF16), pltpu.SemaphoreType.DMA((2,))]),
        out_shape=jax.ShapeDtypeStruct((n_tiles * MOE_TM, d), F32),
        compiler_params=_params("arbitrary"),
        name="moe",
    )(te, tf, nt, xs, w1, w3, w2)


def _combined(dst_ref, x_ref, meta_ref, ys_ref, a_scr, b_scr, sem):
    def body(r, carry):
        pltpu.make_async_copy(ys_ref.at[pl.ds(dst_ref[TOP_K * r], 1)], a_scr.at[pl.ds(r, 1)],
                              sem.at[0]).start(priority=0)
        pltpu.make_async_copy(ys_ref.at[pl.ds(dst_ref[TOP_K * r + 1], 1)], b_scr.at[pl.ds(r, 1)],
                              sem.at[1]).start(priority=1)
        return carry

    rows = x_ref.shape[0]
    lax.fori_loop(0, rows, body, 0, unroll=8)
    pltpu.make_async_copy(ys_ref.at[pl.ds(0, rows)], a_scr, sem.at[0]).wait()
    pltpu.make_async_copy(ys_ref.at[pl.ds(0, rows)], b_scr, sem.at[1]).wait()
    meta = meta_ref[...]
    return x_ref[...] + (meta[:, 4:5] * a_scr[...] + meta[:, 5:6] * b_scr[...])


def _combine_kernel(dst_ref, x_ref, meta_ref, ys_ref, o_ref, a_scr, b_scr, sem):
    o_ref[...] = _combined(dst_ref, x_ref, meta_ref, ys_ref, a_scr, b_scr, sem)


def _combine_final_kernel(dst_ref, x_ref, meta_ref, ys_ref, g_ref, op_ref, os_ref, a_scr, b_scr, sem, *, npt):
    i = pl.program_id(0)
    y = _rmsnorm(_combined(dst_ref, x_ref, meta_ref, ys_ref, a_scr, b_scr, sem), g_ref[...])

    @pl.when(i < npt)
    def _():
        op_ref[...] = y

    @pl.when(i == npt)
    def _():
        os_ref[...] = y


def _combine(dst, x, meta, ys):
    t, d = x.shape
    tile = pl.BlockSpec((DC_TM, d), lambda i: (i, 0))
    return pl.pallas_call(
        _combine_kernel,
        grid=(t // DC_TM,),
        in_specs=[pl.BlockSpec((TOP_K * DC_TM,), lambda i: (i,), memory_space=pltpu.SMEM), tile,
                  pl.BlockSpec((DC_TM, LANES), lambda i: (i, 0)), pl.BlockSpec(memory_space=pl.ANY)],
        out_specs=tile,
        out_shape=jax.ShapeDtypeStruct((t, d), F32),
        scratch_shapes=[pltpu.VMEM((DC_TM, d), F32), pltpu.VMEM((DC_TM, d), F32),
                        pltpu.SemaphoreType.DMA((2,))],
        compiler_params=_params("arbitrary"),
        name="combine",
    )(dst, x, meta, ys)


def _combine_final(dst, x, meta, ys, g, npt):
    t, d = x.shape
    return pl.pallas_call(
        functools.partial(_combine_final_kernel, npt=npt),
        grid=(npt + 1,),
        in_specs=[pl.BlockSpec((TOP_K * TM,), lambda i: (i,), memory_space=pltpu.SMEM),
                  pl.BlockSpec((TM, d), lambda i: (i, 0)), pl.BlockSpec((TM, LANES), lambda i: (i, 0)),
                  pl.BlockSpec(memory_space=pl.ANY), _const((1, d))],
        out_specs=[pl.BlockSpec((TM, d), lambda i: (jnp.minimum(i, npt - 1), 0)),
                   pl.BlockSpec((TM, d), lambda i: (0, 0))],
        out_shape=[jax.ShapeDtypeStruct((npt * TM, d), F32), jax.ShapeDtypeStruct((TM, d), F32)],
        scratch_shapes=[pltpu.VMEM((TM, d), F32), pltpu.VMEM((TM, d), F32), pltpu.SemaphoreType.DMA((2,))],
        compiler_params=_params("arbitrary"),
        name="combine_final",
    )(dst, x, meta, ys, g)


def _moe_ffn(x, g, router, w1, w3, w2, layer, final_g=None, npt=None):
    t, d = x.shape
    ne = router.shape[1]
    assert t % DC_TM == 0 and (TOP_K * DC_TM) % 1024 == 0
    xn, meta, cnt = _router(x, g, router)
    n = cnt[0, :ne].astype(jnp.int32)
    tiles = (n + MOE_TM - 1) // MOE_TM
    ends = jnp.cumsum(tiles)
    n_tiles = (t * TOP_K) // MOE_TM + ne
    off = (ends - tiles) * MOE_TM
    e = meta[:, :TOP_K].astype(jnp.int32)
    dst = (jnp.take(off, e) + meta[:, TOP_K:2 * TOP_K].astype(jnp.int32)).reshape(-1)
    tid = jnp.minimum(jnp.arange(n_tiles, dtype=jnp.int32), ends[-1] - 1)
    te = jnp.sum((tid[:, None] >= ends[None, :]).astype(jnp.int32), axis=1)
    first = (tid == jnp.take(ends - tiles, te)) & (jnp.arange(n_tiles) < ends[-1])
    fill = (off + n) // SUBLANES * SUBLANES
    nt = ends[-1:].astype(jnp.int32)
    xs = _dispatch(dst, fill, nt, xn, n_tiles)
    ys = _moe(te, first.astype(jnp.int32), nt, xs, w1, w3, w2, layer, n_tiles)
    if final_g is not None:
        return _combine_final(dst, x, meta, ys, final_g, npt)
    return _combine(dst, x, meta, ys)


def _final_kernel(x_ref, g_ref, op_ref, os_ref, *, npt):
    i = pl.program_id(0)
    y = _rmsnorm(x_ref[...], g_ref[...])

    @pl.when(i < npt)
    def _():
        op_ref[...] = y

    @pl.when(i == npt)
    def _():
        os_ref[...] = y


def _final_norm(x, g, *, npt):
    t, d = x.shape
    return pl.pallas_call(
        functools.partial(_final_kernel, npt=npt),
        grid=(npt + 1,),
        in_specs=[pl.BlockSpec((TM, d), lambda i: (i, 0)), _const((1, d))],
        out_specs=[pl.BlockSpec((TM, d), lambda i: (jnp.minimum(i, npt - 1), 0)),
                   pl.BlockSpec((TM, d), lambda i: (0, 0))],
        out_shape=[jax.ShapeDtypeStruct((npt * TM, d), F32), jax.ShapeDtypeStruct((TM, d), F32)],
        compiler_params=_params("arbitrary"),
        name="final_norm",
    )(x, g)


def kernel(x_prompt, x_sample, cache_mem_k, cache_mem_v, state_conv, state_pool, mem_prompt, norm_mix, norm_xa, norm_ffn, norm_mem, norm_final, a_w_in, a_ln_g, a_ln_b, a_w_s, a_b_s, a_w_out, b_w_in, b_conv, b_w_out, c_w_in, c_w_grp, c_scale, c_w_out, xa_wq, xa_wk, xa_wv, xa_wo, f_w1, f_w3, f_w2, m_router, m_w1, m_w3, m_w2):
    nb, seq, d = x_prompt.shape
    bs, ls, _ = x_sample.shape
    depth = norm_mix.shape[0]
    nm = mem_prompt.shape[1]
    assert seq % TM == 0 and TM == bs * ls and (nb * nm) % TM == 0
    tps = seq // TM
    npt = nb * tps
    pos0 = PAST_LEN
    dims = dict(npt=npt, bs=bs, ls=ls)
    bf = lambda w: w.astype(BF16)
    row = lambda v: v.reshape(1, d)
    to_pm = lambda a: a.transpose(1, 0, 2).reshape(-1, d)
    from_pm = lambda a, r: a.reshape(r, bs, d).transpose(1, 0, 2)

    mem_k, mem_v, mem_k5, mem_v5 = _mem_kv(mem_prompt.reshape(nb * nm, d), norm_mem[:, None, :],
                                           bf(xa_wk), bf(xa_wv), nm)
    kp = mem_k.reshape(depth, nb, nm, d)
    vp = mem_v.reshape(depth, nb, nm, d)
    ks, vs = cache_mem_k, cache_mem_v

    x = None
    conv_p, conv_s, pool_p, pool_s, chunk_v = [], [], [], [], []
    for i in range(depth):
        kind, j = i % 3, i // 3
        g = row(norm_mix[i])
        if kind == 0:
            srcs = (x_prompt.reshape(nb * seq, d), to_pm(x_sample), 0) if x is None else (x, x, npt)
            x, v_s = _mixer_a(*srcs, g, bf(a_w_in[j]), row(a_ln_g[j]), row(a_ln_b[j]), a_w_s[j], a_b_s[j],
                              bf(a_w_out[j]), **dims)
            chunk_v.append(from_pm(v_s, ls))
        elif kind == 1:
            x, tail, z_s = _mixer_b(x, g, bf(b_w_in[j]), b_conv[j], to_pm(state_conv[j]), bf(b_w_out[j]),
                                    npt=npt, tps=tps, bs=bs)
            conv_p.append(tail[:, 6:, :])
            conv_s.append(from_pm(z_s, 2))
        else:
            x, tail, h_s = _mixer_c(x, g, bf(c_w_in[j]), to_pm(state_pool[j]), bf(c_w_grp[j]),
                                    row(c_scale[j]), bf(c_w_out[j]), tps=tps, pos0=pos0, **dims)
            pool_p.append(tail[:, 1:, :])
            pool_s.append(jnp.concatenate([state_pool[j], from_pm(h_s, ls)], axis=1)[:, -POOL_CTX:, :])
        x = _xattn(x, row(norm_xa[i]), bf(xa_wq[i]), bf(xa_wo[i]), kp, vp, ks, vs, i, tps=tps, **dims)
        g = row(norm_ffn[i])
        if i % 2 == 0:
            x = _ffn(x, g, bf(f_w1[i // 2]), bf(f_w3[i // 2]), bf(f_w2[i // 2]))
        elif i < depth - 1:
            x = _moe_ffn(x, g, m_router[i // 2], m_w1, m_w3, m_w2, i // 2)
        else:
            x = _moe_ffn(x, g, m_router[i // 2], m_w1, m_w3, m_w2, i // 2, row(norm_final), npt)
    y_p, y_s = x if isinstance(x, (tuple, list)) else _final_norm(x, row(norm_final), npt=npt)

    return (y_p.reshape(nb, seq, d), from_pm(y_s, ls), mem_k5, mem_v5,
            jnp.stack(conv_p), jnp.stack(pool_p), jnp.stack(conv_s), jnp.stack(pool_s), jnp.stack(chunk_v))
```

```python
import functools

import jax
import jax.numpy as jnp
from jax import lax
from jax.experimental import pallas as pl
from jax.experimental.pallas import tpu as pltpu

F32 = jnp.float32
BF16 = jnp.bfloat16

EPS = 1e-6
TM = 512
CHUNK = 128
A_GROUPS = 8
POOL_WINDOWS = (2, 4, 8, 16)
POOL_CTX = max(POOL_WINDOWS) - 1
X_HEADS = 4
TOP_K = 2
PAST_LEN = 16384
LANES = 128
SUBLANES = 8
DC_TM = 1536
MOE_TM = 512
MOE_TF = 512
MOE_CHUNK = 1024
VMEM_LIMIT = 60 * 1024 * 1024


def _params(*sem):
    return pltpu.CompilerParams(dimension_semantics=sem, vmem_limit_bytes=VMEM_LIMIT)


def _const(shape):
    nd = len(shape)
    return pl.BlockSpec(shape, lambda *_: (0,) * nd, pipeline_mode=pl.Buffered(1))


def _rmsnorm(x, g):
    r = lax.rsqrt(jnp.mean(x * x, axis=-1, keepdims=True) + EPS)
    return (x * r) * g


def _bdot(a, w):
    return jnp.dot(a.astype(BF16), w, preferred_element_type=F32)


def _mem_kv_kernel(m_ref, g_ref, wk_ref, wv_ref, k_ref, v_ref, k5_ref, v5_ref):
    mn = _rmsnorm(m_ref[...], g_ref[...]).astype(BF16)
    nbt, nm, nh, hd = k5_ref.shape
    for w_ref, o_ref, o5_ref in ((wk_ref, k_ref, k5_ref), (wv_ref, v_ref, v5_ref)):
        y = jnp.dot(mn, w_ref[...], preferred_element_type=F32)
        o_ref[...] = y.astype(BF16)
        for b in range(nbt):
            for h in range(nh):
                o5_ref[b, :, h, :] = y[b * nm:(b + 1) * nm, h * hd:(h + 1) * hd]


def _mem_kv(mem, g, wk, wv, nm):
    rows, d = mem.shape
    depth = g.shape[0]
    hd = d // X_HEADS
    out = jax.ShapeDtypeStruct((depth, rows, d), BF16)
    out5 = jax.ShapeDtypeStruct((depth, rows // nm, nm, X_HEADS, hd), F32)
    wspec = pl.BlockSpec((None, d, d), lambda l, r: (l, 0, 0))
    ospec = pl.BlockSpec((None, TM, d), lambda l, r: (l, r, 0))
    o5spec = pl.BlockSpec((None, TM // nm, nm, X_HEADS, hd), lambda l, r: (l, r, 0, 0, 0))
    return pl.pallas_call(
        _mem_kv_kernel,
        grid=(depth, rows // TM),
        in_specs=[pl.BlockSpec((TM, d), lambda l, r: (r, 0)),
                  pl.BlockSpec((None, 1, d), lambda l, r: (l, 0, 0)),
                  wspec, wspec],
        out_specs=[ospec, ospec, o5spec, o5spec],
        out_shape=[out, out, out5, out5],
        compiler_params=_params("arbitrary", "arbitrary"),
        name="mem_kv",
    )(mem, g, wk, wv)


def _mixer_a_kernel(x_ref, xt_ref, g_ref, win_ref, lng_ref, lnb_ref, ws_ref, bs_ref, sc_ref, sb_ref, wout_ref,
                    o_ref, vs_ref, u_scr, v_scr, s_scr, *, npt, bs, ls):
    i = pl.program_id(0)
    d = x_ref.shape[1]
    x = jnp.where(i == npt, xt_ref[...], x_ref[...])
    z = jax.nn.gelu(_bdot(_rmsnorm(x, g_ref[...]), win_ref[...]))
    u_scr[...] = z[:, :d]
    v = z[:, d:]
    mu = jnp.mean(v, axis=-1, keepdims=True)
    var = jnp.mean(jnp.square(v - mu), axis=-1, keepdims=True)
    v_scr[...] = ((v - mu) * lax.rsqrt(var + EPS)) * lng_ref[...] + lnb_ref[...]
    gd = d // A_GROUPS

    @pl.when(i < npt)
    def _():
        rows = lax.broadcasted_iota(jnp.int32, (CHUNK, CHUNK), 0)
        cols = lax.broadcasted_iota(jnp.int32, (CHUNK, CHUNK), 1)
        for g in range(A_GROUPS):
            wg = jnp.where(cols <= rows, ws_ref[g], 0.0).astype(BF16)
            bg = bs_ref[g]
            for c in range(TM // CHUNK):
                vc = v_scr[c * CHUNK:(c + 1) * CHUNK, g * gd:(g + 1) * gd]
                s_scr[c * CHUNK:(c + 1) * CHUNK, g * gd:(g + 1) * gd] = (
                    jnp.dot(wg, vc.astype(BF16), preferred_element_type=F32) + bg)

    @pl.when(i == npt)
    def _():
        for t in range(ls):
            acc = jnp.broadcast_to(sb_ref[t:t + 1, :], (bs, d))
            for s in range(t + 1):
                acc = acc + sc_ref[t * ls + s:t * ls + s + 1, :] * v_scr[s * bs:(s + 1) * bs, :]
            s_scr[t * bs:(t + 1) * bs, :] = acc
        vs_ref[...] = v_scr[...]

    o_ref[...] = x + _bdot(u_scr[...] * s_scr[...], wout_ref[...])


def _mixer_a(x, x_tail, tail_blk, g, w_in, ln_g, ln_b, w_s, b_s, w_out, *, npt, bs, ls):
    d = x.shape[1]
    t = (npt + 1) * TM
    assert TM == bs * ls and TM % CHUNK == 0 and ls <= CHUNK
    gd = d // A_GROUPS
    sc = jnp.repeat(w_s[:, :ls, :ls].transpose(1, 2, 0).reshape(ls * ls, A_GROUPS), gd, axis=1)
    sb = jnp.repeat(b_s[:, :ls].T, gd, axis=1)
    tile = pl.BlockSpec((TM, d), lambda i: (i, 0))
    return pl.pallas_call(
        functools.partial(_mixer_a_kernel, npt=npt, bs=bs, ls=ls),
        grid=(npt + 1,),
        in_specs=[pl.BlockSpec((TM, d), lambda i: (jnp.minimum(i, npt - 1), 0)),
                  pl.BlockSpec((TM, d), lambda i: (tail_blk, 0), pipeline_mode=pl.Buffered(1)),
                  _const((1, d)), _const(w_in.shape), _const((1, d)), _const((1, d)),
                  _const(w_s.shape), _const((A_GROUPS, CHUNK, 1)), _const(sc.shape), _const(sb.shape),
                  _const(w_out.shape)],
        out_specs=[tile, pl.BlockSpec((TM, d), lambda i: (0, 0))],
        out_shape=[jax.ShapeDtypeStruct((t, d), F32), jax.ShapeDtypeStruct((TM, d), F32)],
        scratch_shapes=[pltpu.VMEM((TM, d), F32)] * 3,
        compiler_params=_params("arbitrary"),
        name="mixer_a",
    )(x, x_tail, g, w_in, ln_g, ln_b, w_s, b_s[:, :, None], sc, sb, w_out)


def _mixer_b_kernel(x_ref, g_ref, win_ref, cw_ref, pre_ref, wout_ref,
                    o_ref, tail_ref, zs_ref, conv_scr, carry_scr, *, npt, tps, bs):
    i = pl.program_id(0)
    d = x_ref.shape[1]
    x = x_ref[...]
    p = _bdot(_rmsnorm(x, g_ref[...]), win_ref[...])
    gate_b = p[:, :d]
    zc = p[:, d:2 * d] * p[:, 2 * d:]
    w0, w1, w2 = cw_ref[0:1, :], cw_ref[1:2, :], cw_ref[2:3, :]

    @pl.when(i == 0)
    def _():
        carry_scr[...] = jnp.zeros_like(carry_scr)

    @pl.when(i < npt)
    def _():
        keep = (i % tps) != 0
        c6 = jnp.where(keep, carry_scr[6:7, :], 0.0)
        c7 = jnp.where(keep, carry_scr[7:8, :], 0.0)
        rows = lax.broadcasted_iota(jnp.int32, (TM, 1), 0)
        sh1 = jnp.where(rows == 0, c7, pltpu.roll(zc, 1, 0))
        sh2 = jnp.where(rows == 0, c6, jnp.where(rows == 1, c7, pltpu.roll(zc, 2, 0)))
        conv_scr[...] = w0 * sh2 + w1 * sh1 + w2 * zc
        carry_scr[...] = zc[TM - 8:, :]
        tail_ref[...] = zc[TM - 8:, :]

    @pl.when(i == npt)
    def _():
        p0, p1 = pre_ref[0:bs, :], pre_ref[bs:2 * bs, :]
        sh1 = jnp.concatenate([p1, zc[:TM - bs]], axis=0)
        sh2 = jnp.concatenate([p0, p1, zc[:TM - 2 * bs]], axis=0)
        conv_scr[...] = w0 * sh2 + w1 * sh1 + w2 * zc
        zs_ref[...] = zc[TM - 2 * bs:, :]

    o_ref[...] = x + _bdot(gate_b * conv_scr[...], wout_ref[...])


def _mixer_b(x, g, w_in, conv_w, prefix, w_out, *, npt, tps, bs):
    t, d = x.shape
    nb = npt // tps
    tile = pl.BlockSpec((TM, d), lambda i: (i, 0))
    return pl.pallas_call(
        functools.partial(_mixer_b_kernel, npt=npt, tps=tps, bs=bs),
        grid=(npt + 1,),
        in_specs=[tile, _const((1, d)), _const(w_in.shape), _const(conv_w.shape), _const(prefix.shape),
                  _const(w_out.shape)],
        out_specs=[tile,
                   pl.BlockSpec((None, 8, d), lambda i: (jnp.minimum(i // tps, nb - 1), 0, 0)),
                   pl.BlockSpec((2 * bs, d), lambda i: (0, 0))],
        out_shape=[jax.ShapeDtypeStruct((t, d), F32), jax.ShapeDtypeStruct((nb, 8, d), F32),
                   jax.ShapeDtypeStruct((2 * bs, d), F32)],
        scratch_shapes=[pltpu.VMEM((TM, d), F32), pltpu.VMEM((8, d), F32)],
        compiler_params=_params("arbitrary"),
        name="mixer_b",
    )(x, g, w_in, conv_w, prefix, w_out)


def _mixer_c_kernel(x_ref, g_ref, win_ref, pre_ref, wgrp_ref, scale_ref, wout_ref,
                    o_ref, tail_ref, hs_ref, d_scr, hist_scr, *, npt, tps, bs, ls, pos0):
    i = pl.program_id(0)
    d = x_ref.shape[1]
    gd = d // len(POOL_WINDOWS)
    hp = POOL_CTX + 1
    x = x_ref[...]
    h = _bdot(_rmsnorm(x, g_ref[...]), win_ref[...])

    @pl.when(i == 0)
    def _():
        hist_scr[...] = jnp.zeros_like(hist_scr)

    @pl.when(i < npt)
    def _():
        keep = (i % tps) != 0
        ext = jnp.concatenate([jnp.where(keep, hist_scr[...], 0.0), h], axis=0)
        pos = (i % tps) * TM + lax.broadcasted_iota(jnp.int32, (TM, 1), 0)
        sums = ext
        width = 1
        for g, w in enumerate(POOL_WINDOWS):
            while width < w:
                sums = sums + pltpu.roll(sums, width, 0)
                width *= 2
            cnt = jnp.minimum(w, pos + 1).astype(F32)
            sl = slice(g * gd, (g + 1) * gd)
            d_scr[:, sl] = sums[hp:, sl] / cnt - h[:, sl]
        hist_scr[...] = h[TM - hp:, :]
        tail_ref[...] = h[TM - hp:, :]

    @pl.when(i == npt)
    def _():
        for g, w in enumerate(POOL_WINDOWS):
            sl = slice(g * gd, (g + 1) * gd)
            run = jnp.zeros((bs, gd), F32)
            tails = [run]
            for m in range(1, w):
                run = run + pre_ref[(POOL_CTX - m) * bs:(POOL_CTX - m + 1) * bs, sl]
                tails.append(run)
            for t in range(ls):
                n = min(w, t + 1)
                acc = tails[w - n]
                for j in range(n):
                    acc = acc + h[(t - j) * bs:(t - j + 1) * bs, sl]
                cnt = float(min(w, pos0 + t + 1))
                d_scr[t * bs:(t + 1) * bs, sl] = acc / cnt - h[t * bs:(t + 1) * bs, sl]
        hs_ref[...] = h

    m = jnp.concatenate(
        [_bdot(d_scr[:, g * gd:(g + 1) * gd], wgrp_ref[g]) for g in range(len(POOL_WINDOWS))], axis=1)
    o_ref[...] = x + _bdot(m * scale_ref[...], wout_ref[...])


def _mixer_c(x, g, w_in, prefix, w_grp, scale, w_out, *, npt, tps, bs, ls, pos0):
    t, d = x.shape
    nb = npt // tps
    hp = POOL_CTX + 1
    tile = pl.BlockSpec((TM, d), lambda i: (i, 0))
    return pl.pallas_call(
        functools.partial(_mixer_c_kernel, npt=npt, tps=tps, bs=bs, ls=ls, pos0=pos0),
        grid=(npt + 1,),
        in_specs=[tile, _const((1, d)), _const(w_in.shape), _const(prefix.shape), _const(w_grp.shape),
                  _const((1, d)), _const(w_out.shape)],
        out_specs=[tile,
                   pl.BlockSpec((None, hp, d), lambda i: (jnp.minimum(i // tps, nb - 1), 0, 0)),
                   pl.BlockSpec((TM, d), lambda i: (0, 0))],
        out_shape=[jax.ShapeDtypeStruct((t, d), F32), jax.ShapeDtypeStruct((nb, hp, d), F32),
                   jax.ShapeDtypeStruct((TM, d), F32)],
        scratch_shapes=[pltpu.VMEM((TM, d), F32), pltpu.VMEM((hp, d), F32)],
        compiler_params=_params("arbitrary"),
        name="mixer_c",
    )(x, g, w_in, prefix, w_grp, scale, w_out)


def _softmax_rows(s):
    e = jnp.exp(s - jnp.max(s, axis=-1, keepdims=True))
    return e / jnp.sum(e, axis=-1, keepdims=True)


def _xattn_kernel(x_ref, xs_ref, g_ref, wq_ref, wo_ref, kp_ref, vp_ref, ks_ref, vs_ref,
                  o_ref, q_scr, o_scr, *, npt, bs, ls, spb):
    i = pl.program_id(0)
    d = x_ref.shape[1]
    hd = d // X_HEADS
    qscale = float(hd) ** -0.5
    nt = (((1,), (1,)), ((), ()))

    @pl.when(i == 0)
    def _():
        q_scr[...] = _bdot(_rmsnorm(xs_ref[...], g_ref[...]), wq_ref[...]) * qscale

    @pl.when(i < npt)
    def _():
        x = x_ref[...]
        q = (_bdot(_rmsnorm(x, g_ref[...]), wq_ref[...]) * qscale).astype(BF16)
        k = kp_ref[...]
        v = vp_ref[...]
        heads = []
        for h in range(X_HEADS):
            sl = slice(h * hd, (h + 1) * hd)
            s = lax.dot_general(q[:, sl], k[:, sl], nt, preferred_element_type=F32)
            heads.append(_bdot(_softmax_rows(s), v[:, sl]))
        o_ref[...] = x + _bdot(jnp.concatenate(heads, axis=1), wo_ref[...])

        nm = ks_ref.shape[1]
        rows = lax.broadcasted_iota(jnp.int32, (X_HEADS * ls, nm * X_HEADS), 0)
        cols = lax.broadcasted_iota(jnp.int32, (X_HEADS * ls, nm * X_HEADS), 1)
        own = (rows // ls) == (cols % X_HEADS)
        for bb in range(spb):
            b = i * spb + bb
            qb = jnp.concatenate([q_scr[pl.ds(t * bs + b, 1), :] for t in range(ls)], axis=0)
            qh = jnp.concatenate([qb[:, h * hd:(h + 1) * hd] for h in range(X_HEADS)], axis=0)
            kr = ks_ref[bb].reshape(nm * X_HEADS, hd).astype(BF16)
            vr = vs_ref[bb].reshape(nm * X_HEADS, hd).astype(BF16)
            s = lax.dot_general(qh.astype(BF16), kr, nt, preferred_element_type=F32)
            of = _bdot(_softmax_rows(jnp.where(own, s, -jnp.inf)), vr)
            ob = jnp.concatenate([of[h * ls:(h + 1) * ls] for h in range(X_HEADS)], axis=1)
            for t in range(ls):
                o_scr[pl.ds(t * bs + b, 1), :] = ob[t:t + 1]

    @pl.when(i == npt)
    def _():
        o_ref[...] = x_ref[...] + _bdot(o_scr[...], wo_ref[...])


def _xattn(x, g, wq, wo, kp, vp, ks, vs, layer, *, npt, tps, bs, ls):
    t, d = x.shape
    nb, nm = kp.shape[1], kp.shape[2]
    assert bs % npt == 0
    spb = bs // npt
    tile = pl.BlockSpec((TM, d), lambda i: (i, 0))
    pmem = pl.BlockSpec((None, None, nm, d), lambda i: (layer, jnp.minimum(i // tps, nb - 1), 0, 0))
    smem = pl.BlockSpec((None, spb, nm, X_HEADS, d // X_HEADS),
                        lambda i: (layer, jnp.minimum(i, npt - 1), 0, 0, 0))
    return pl.pallas_call(
        functools.partial(_xattn_kernel, npt=npt, bs=bs, ls=ls, spb=spb),
        grid=(npt + 1,),
        in_specs=[tile, pl.BlockSpec((TM, d), lambda i: (npt, 0), pipeline_mode=pl.Buffered(1)),
                  _const((1, d)), _const(wq.shape), _const(wo.shape), pmem, pmem, smem, smem],
        out_specs=tile,
        out_shape=jax.ShapeDtypeStruct((t, d), F32),
        scratch_shapes=[pltpu.VMEM((TM, d), F32), pltpu.VMEM((TM, d), F32)],
        compiler_params=_params("arbitrary"),
        name="xattn",
    )(x, x, g, wq, wo, kp, vp, ks, vs)


def _ffn_kernel(x_ref, g_ref, w1_ref, w3_ref, w2_ref, o_ref, *, chunks):
    x = x_ref[...]
    xn = _rmsnorm(x, g_ref[...]).astype(BF16)
    acc = x
    for lo, hi in chunks:
        h1 = jnp.dot(xn, w1_ref[:, lo:hi], preferred_element_type=F32)
        h3 = jnp.dot(xn, w3_ref[:, lo:hi], preferred_element_type=F32)
        acc = acc + _bdot(jax.nn.silu(h1) * h3, w2_ref[lo:hi, :])
    o_ref[...] = acc


def _ffn(x, g, w1, w3, w2):
    t, d = x.shape
    dff = w1.shape[1]
    step = 1024
    chunks = tuple((lo, min(lo + step, dff)) for lo in range(0, dff, step))
    tile = pl.BlockSpec((TM, d), lambda i: (i, 0))
    return pl.pallas_call(
        functools.partial(_ffn_kernel, chunks=chunks),
        grid=(t // TM,),
        in_specs=[tile, _const((1, d)), _const(w1.shape), _const(w3.shape), _const(w2.shape)],
        out_specs=tile,
        out_shape=jax.ShapeDtypeStruct((t, d), F32),
        compiler_params=_params("arbitrary"),
        name="ffn",
    )(x, g, w1, w3, w2)


def _router_kernel(x_ref, g_ref, r_ref, tri_ref, xn_ref, meta_ref, cnt_ref, carry_scr, *, n_experts):
    i = pl.program_id(0)

    @pl.when(i == 0)
    def _():
        carry_scr[...] = jnp.zeros_like(carry_scr)

    xn = _rmsnorm(x_ref[...], g_ref[...])
    xn_ref[...] = xn
    xh = xn.astype(BF16)
    xl = (xn - xh.astype(F32)).astype(BF16)
    r = r_ref[...]
    rh = r.astype(BF16)
    rl = (r - rh.astype(F32)).astype(BF16)
    logits = (jnp.dot(xh, rh, preferred_element_type=F32) + jnp.dot(xl, rh, preferred_element_type=F32)
              + jnp.dot(xh, rl, preferred_element_type=F32))
    lane = lax.broadcasted_iota(jnp.int32, logits.shape, 1).astype(F32)
    neg = jnp.float32(-jnp.inf)
    logits = jnp.where(lane < n_experts, logits, neg)
    v1 = jnp.max(logits, axis=-1, keepdims=True)
    i1 = jnp.min(jnp.where(logits == v1, lane, float(LANES)), axis=-1, keepdims=True)
    rest = jnp.where(lane == i1, neg, logits)
    v2 = jnp.max(rest, axis=-1, keepdims=True)
    i2 = jnp.min(jnp.where(rest == v2, lane, float(LANES)), axis=-1, keepdims=True)
    e2 = jnp.exp(v2 - v1)
    den = 1.0 + e2
    hit = jnp.where(lane == i1, 1.0, jnp.where(lane == i2, 1.0, 0.0))
    before = jnp.dot(tri_ref[...], hit.astype(BF16), preferred_element_type=F32) + carry_scr[0:1, :]
    r1 = jnp.sum(jnp.where(lane == i1, before, 0.0), axis=-1, keepdims=True)
    r2 = jnp.sum(jnp.where(lane == i2, before, 0.0), axis=-1, keepdims=True)
    cols = (i1, i2, r1, r2, 1.0 / den, e2 / den)
    meta = jnp.zeros_like(logits)
    for c, val in enumerate(cols):
        meta = jnp.where(lane == c, val, meta)
    meta_ref[...] = meta
    carry_scr[...] = carry_scr[...] + jnp.sum(hit, axis=0, keepdims=True)
    cnt_ref[...] = carry_scr[...]


def _router(x, g, router):
    t, d = x.shape
    ne = router.shape[1]
    assert TOP_K == 2 and ne <= LANES
    rpad = jnp.pad(router, ((0, 0), (0, LANES - ne)))
    tri = jnp.tril(jnp.ones((TM, TM), BF16), -1)
    return pl.pallas_call(
        functools.partial(_router_kernel, n_experts=ne),
        grid=(t // TM,),
        in_specs=[pl.BlockSpec((TM, d), lambda i: (i, 0)), _const((1, d)), _const(rpad.shape),
                  _const(tri.shape)],
        out_specs=[pl.BlockSpec((TM, d), lambda i: (i, 0)), pl.BlockSpec((TM, LANES), lambda i: (i, 0)),
                   pl.BlockSpec((8, LANES), lambda i: (0, 0))],
        out_shape=[jax.ShapeDtypeStruct((t, d), F32), jax.ShapeDtypeStruct((t, LANES), F32),
                   jax.ShapeDtypeStruct((8, LANES), F32)],
        scratch_shapes=[pltpu.VMEM((8, LANES), F32)],
        compiler_params=_params("arbitrary"),
        name="router",
    )(x, g, rpad, tri)


def _dispatch_kernel(dst_ref, fill_ref, nt_ref, xn_ref, xs_ref, z_scr, zsem, sem, *, min_tiles):
    i = pl.program_id(0)
    rows = xn_ref.shape[0]
    last = xs_ref.shape[0] // MOE_TM - 1

    @pl.when(i == 0)
    def _():
        z_scr[...] = jnp.zeros_like(z_scr)
        fills = [pltpu.make_async_copy(
            z_scr, xs_ref.at[pl.ds(pl.multiple_of(fill_ref[e], SUBLANES), MOE_TM)], zsem)
            for e in range(fill_ref.shape[0])]
        for c in fills:
            c.start()
        for c in fills:
            c.wait()
        for j in range(last + 1 - min_tiles):
            tile = nt_ref[0] + j

            @pl.when(tile <= last)
            def _():
                c = pltpu.make_async_copy(
                    z_scr, xs_ref.at[pl.ds(pl.multiple_of(tile * MOE_TM, MOE_TM), MOE_TM)], zsem)
                c.start()
                c.wait()

    def body(r, carry):
        src = xn_ref.at[pl.ds(r, 1)]
        for k in range(TOP_K):
            pltpu.make_async_copy(src, xs_ref.at[pl.ds(dst_ref[TOP_K * r + k], 1)], sem).start(priority=k)
        return carry

    lax.fori_loop(0, rows, body, 0, unroll=8)
    for _ in range(TOP_K):
        pltpu.make_async_copy(xn_ref, xs_ref.at[pl.ds(0, rows)], sem).wait()


def _dispatch(dst, fill, nt, xn, n_slot_tiles):
    t, d = xn.shape
    return pl.pallas_call(
        functools.partial(_dispatch_kernel, min_tiles=(t * TOP_K) // MOE_TM),
        grid=(t // DC_TM,),
        in_specs=[pl.BlockSpec((TOP_K * DC_TM,), lambda i: (i,), memory_space=pltpu.SMEM),
                  pl.BlockSpec(memory_space=pltpu.SMEM), pl.BlockSpec(memory_space=pltpu.SMEM),
                  pl.BlockSpec((DC_TM, d), lambda i: (i, 0))],
        out_specs=pl.BlockSpec(memory_space=pl.ANY),
        out_shape=jax.ShapeDtypeStruct(((n_slot_tiles + 1) * MOE_TM, d), F32),
        scratch_shapes=[pltpu.VMEM((MOE_TM, d), F32), pltpu.SemaphoreType.DMA(()),
                        pltpu.SemaphoreType.DMA(())],
        compiler_params=_params("arbitrary"),
        name="dispatch",
    )(dst, fill, nt, xn)


def _moe_kernel(te_ref, tf_ref, nt_ref, xs_ref, w1_hbm, w3_hbm, w2_hbm, ys_ref,
                s1, s3, s2, w1_scr, w3_scr, w2_scr, sem, *, layer, n_tiles):
    i = pl.program_id(0)
    dff = w1_scr.shape[1]
    nf = dff // MOE_TF
    valid = i < nt_ref[0]
    first = valid & (tf_ref[i] == 1)
    nxt = jnp.minimum(i + 1, n_tiles - 1)
    next_first = (i + 1 < nt_ref[0]) & (tf_ref[nxt] == 1)

    def chunk(e, f, slot):
        cols = pl.ds(f * MOE_TF, MOE_TF)
        return (pltpu.make_async_copy(w1_hbm.at[layer, e, :, cols], s1.at[slot], sem.at[slot]),
                pltpu.make_async_copy(w3_hbm.at[layer, e, :, cols], s3.at[slot], sem.at[slot]),
                pltpu.make_async_copy(w2_hbm.at[layer, e, cols, :], s2.at[slot], sem.at[slot]))

    def start(e, f, slot):
        for c in chunk(e, f, slot):
            c.start()

    def prefetch(e):
        start(e, 0, 0)
        start(e, 1, 1)

    def swiglu(xb, lo, hi):
        h1 = jnp.dot(xb, w1_scr[:, lo:hi], preferred_element_type=F32)
        h3 = jnp.dot(xb, w3_scr[:, lo:hi], preferred_element_type=F32)
        return _bdot(jax.nn.silu(h1) * h3, w2_scr[lo:hi, :])

    @pl.when(first & (i == 0))
    def _():
        prefetch(te_ref[i])

    @pl.when(first)
    def _():
        xb = xs_ref[...].astype(BF16)
        acc = None
        for f in range(nf):
            slot = f % 2
            for c in chunk(te_ref[i], f, slot):
                c.wait()
            lo, hi = f * MOE_TF, (f + 1) * MOE_TF
            w1_scr[:, lo:hi] = s1[slot].astype(BF16)
            w3_scr[:, lo:hi] = s3[slot].astype(BF16)
            w2_scr[lo:hi, :] = s2[slot].astype(BF16)
            if f + 2 < nf:
                start(te_ref[i], f + 2, slot)
            y = swiglu(xb, lo, hi)
            acc = y if acc is None else acc + y
        ys_ref[...] = acc

        @pl.when(next_first)
        def _():
            prefetch(te_ref[nxt])

    @pl.when(valid & jnp.logical_not(first))
    def _():
        @pl.when(next_first)
        def _():
            prefetch(te_ref[nxt])

        xb = xs_ref[...].astype(BF16)
        acc = None
        for lo in range(0, dff, MOE_CHUNK):
            y = swiglu(xb, lo, min(lo + MOE_CHUNK, dff))
            acc = y if acc is None else acc + y
        ys_ref[...] = acc

    @pl.when(jnp.logical_not(valid))
    def _():
        ys_ref[...] = jnp.zeros_like(ys_ref)


def _moe(te, tf, nt, xs, w1, w3, w2, layer, n_tiles):
    d = xs.shape[1]
    dff = w1.shape[3]
    tile = pl.BlockSpec((MOE_TM, d), lambda i, te, tf, nt: (i, 0))
    used = pl.BlockSpec((MOE_TM, d), lambda i, te, tf, nt: (jnp.minimum(i, nt[0] - 1), 0))
    hbm = pl.BlockSpec(memory_space=pl.ANY)
    return pl.pallas_call(
        functools.partial(_moe_kernel, layer=layer, n_tiles=n_tiles),
        grid_spec=pltpu.PrefetchScalarGridSpec(
            num_scalar_prefetch=3,
            grid=(n_tiles,),
            in_specs=[used, hbm, hbm, hbm],
            out_specs=tile,
            scratch_shapes=[pltpu.VMEM((2, d, MOE_TF), F32), pltpu.VMEM((2, d, MOE_TF), F32),
                            pltpu.VMEM((2, MOE_TF, d), F32),
                            pltpu.VMEM((d, dff), BF16), pltpu.VMEM((d, dff), BF16),
                            pltpu.VMEM((dff, d), BF16), pltpu.SemaphoreType.DMA((2,))]),
        out_shape=jax.ShapeDtypeStruct((n_tiles * MOE_TM, d), F32),
        compiler_params=_params("arbitrary"),
        name="moe",
    )(te, tf, nt, xs, w1, w3, w2)


def _combined(dst_ref, x_ref, meta_ref, ys_ref, a_scr, b_scr, sem):
    def body(r, carry):
        pltpu.make_async_copy(ys_ref.at[pl.ds(dst_ref[TOP_K * r], 1)], a_scr.at[pl.ds(r, 1)],
                              sem.at[0]).start(priority=0)
        pltpu.make_async_copy(ys_ref.at[pl.ds(dst_ref[TOP_K * r + 1], 1)], b_scr.at[pl.ds(r, 1)],
                              sem.at[1]).start(priority=1)
        return carry

    rows = x_ref.shape[0]
    lax.fori_loop(0, rows, body, 0, unroll=8)
    pltpu.make_async_copy(ys_ref.at[pl.ds(0, rows)], a_scr, sem.at[0]).wait()
    pltpu.make_async_copy(ys_ref.at[pl.ds(0, rows)], b_scr, sem.at[1]).wait()
    meta = meta_ref[...]
    return x_ref[...] + (meta[:, 4:5] * a_scr[...] + meta[:, 5:6] * b_scr[...])


def _combine_kernel(dst_ref, x_ref, meta_ref, ys_ref, o_ref, a_scr, b_scr, sem):
    o_ref[...] = _combined(dst_ref, x_ref, meta_ref, ys_ref, a_scr, b_scr, sem)


def _combine_final_kernel(dst_ref, x_ref, meta_ref, ys_ref, g_ref, op_ref, os_ref, a_scr, b_scr, sem, *, npt):
    i = pl.program_id(0)
    y = _rmsnorm(_combined(dst_ref, x_ref, meta_ref, ys_ref, a_scr, b_scr, sem), g_ref[...])

    @pl.when(i < npt)
    def _():
        op_ref[...] = y

    @pl.when(i == npt)
    def _():
        os_ref[...] = y


def _combine(dst, x, meta, ys):
    t, d = x.shape
    tile = pl.BlockSpec((DC_TM, d), lambda i: (i, 0))
    return pl.pallas_call(
        _combine_kernel,
        grid=(t // DC_TM,),
        in_specs=[pl.BlockSpec((TOP_K * DC_TM,), lambda i: (i,), memory_space=pltpu.SMEM), tile,
                  pl.BlockSpec((DC_TM, LANES), lambda i: (i, 0)), pl.BlockSpec(memory_space=pl.ANY)],
        out_specs=tile,
        out_shape=jax.ShapeDtypeStruct((t, d), F32),
        scratch_shapes=[pltpu.VMEM((DC_TM, d), F32), pltpu.VMEM((DC_TM, d), F32),
                        pltpu.SemaphoreType.DMA((2,))],
        compiler_params=_params("arbitrary"),
        name="combine",
    )(dst, x, meta, ys)


def _combine_final(dst, x, meta, ys, g, npt):
    t, d = x.shape
    return pl.pallas_call(
        functools.partial(_combine_final_kernel, npt=npt),
        grid=(npt + 1,),
        in_specs=[pl.BlockSpec((TOP_K * TM,), lambda i: (i,), memory_space=pltpu.SMEM),
                  pl.BlockSpec((TM, d), lambda i: (i, 0)), pl.BlockSpec((TM, LANES), lambda i: (i, 0)),
                  pl.BlockSpec(memory_space=pl.ANY), _const((1, d))],
        out_specs=[pl.BlockSpec((TM, d), lambda i: (jnp.minimum(i, npt - 1), 0)),
                   pl.BlockSpec((TM, d), lambda i: (0, 0))],
        out_shape=[jax.ShapeDtypeStruct((npt * TM, d), F32), jax.ShapeDtypeStruct((TM, d), F32)],
        scratch_shapes=[pltpu.VMEM((TM, d), F32), pltpu.VMEM((TM, d), F32), pltpu.SemaphoreType.DMA((2,))],
        compiler_params=_params("arbitrary"),
        name="combine_final",
    )(dst, x, meta, ys, g)


def _moe_ffn(x, g, router, w1, w3, w2, layer, final_g=None, npt=None):
    t, d = x.shape
    ne = router.shape[1]
    assert t % DC_TM == 0 and (TOP_K * DC_TM) % 1024 == 0
    xn, meta, cnt = _router(x, g, router)
    n = cnt[0, :ne].astype(jnp.int32)
    tiles = (n + MOE_TM - 1) // MOE_TM
    ends = jnp.cumsum(tiles)
    n_tiles = (t * TOP_K) // MOE_TM + ne
    off = (ends - tiles) * MOE_TM
    e = meta[:, :TOP_K].astype(jnp.int32)
    off_e = jnp.sum(jnp.where(e[..., None] == jnp.arange(ne, dtype=jnp.int32), off, 0), axis=-1)
    dst = (off_e + meta[:, TOP_K:2 * TOP_K].astype(jnp.int32)).reshape(-1)
    tid = jnp.minimum(jnp.arange(n_tiles, dtype=jnp.int32), ends[-1] - 1)
    te = jnp.sum((tid[:, None] >= ends[None, :]).astype(jnp.int32), axis=1)
    first = (tid == jnp.take(ends - tiles, te)) & (jnp.arange(n_tiles) < ends[-1])
    fill = (off + n) // SUBLANES * SUBLANES
    nt = ends[-1:].astype(jnp.int32)
    xs = _dispatch(dst, fill, nt, xn, n_tiles)
    ys = _moe(te, first.astype(jnp.int32), nt, xs, w1, w3, w2, layer, n_tiles)
    if final_g is not None:
        return _combine_final(dst, x, meta, ys, final_g, npt)
    return _combine(dst, x, meta, ys)


def _final_kernel(x_ref, g_ref, op_ref, os_ref, *, npt):
    i = pl.program_id(0)
    y = _rmsnorm(x_ref[...], g_ref[...])

    @pl.when(i < npt)
    def _():
        op_ref[...] = y

    @pl.when(i == npt)
    def _():
        os_ref[...] = y


def _final_norm(x, g, *, npt):
    t, d = x.shape
    return pl.pallas_call(
        functools.partial(_final_kernel, npt=npt),
        grid=(npt + 1,),
        in_specs=[pl.BlockSpec((TM, d), lambda i: (i, 0)), _const((1, d))],
        out_specs=[pl.BlockSpec((TM, d), lambda i: (jnp.minimum(i, npt - 1), 0)),
                   pl.BlockSpec((TM, d), lambda i: (0, 0))],
        out_shape=[jax.ShapeDtypeStruct((npt * TM, d), F32), jax.ShapeDtypeStruct((TM, d), F32)],
        compiler_params=_params("arbitrary"),
        name="final_norm",
    )(x, g)


def kernel(x_prompt, x_sample, cache_mem_k, cache_mem_v, state_conv, state_pool, mem_prompt, norm_mix, norm_xa, norm_ffn, norm_mem, norm_final, a_w_in, a_ln_g, a_ln_b, a_w_s, a_b_s, a_w_out, b_w_in, b_conv, b_w_out, c_w_in, c_w_grp, c_scale, c_w_out, xa_wq, xa_wk, xa_wv, xa_wo, f_w1, f_w3, f_w2, m_router, m_w1, m_w3, m_w2):
    nb, seq, d = x_prompt.shape
    bs, ls, _ = x_sample.shape
    depth = norm_mix.shape[0]
    nm = mem_prompt.shape[1]
    assert seq % TM == 0 and TM == bs * ls and (nb * nm) % TM == 0
    tps = seq // TM
    npt = nb * tps
    pos0 = PAST_LEN
    dims = dict(npt=npt, bs=bs, ls=ls)
    bf = lambda w: w.astype(BF16)
    row = lambda v: v.reshape(1, d)
    to_pm = lambda a: a.transpose(1, 0, 2).reshape(-1, d)
    from_pm = lambda a, r: a.reshape(r, bs, d).transpose(1, 0, 2)

    mem_k, mem_v, mem_k5, mem_v5 = _mem_kv(mem_prompt.reshape(nb * nm, d), norm_mem[:, None, :],
                                           bf(xa_wk), bf(xa_wv), nm)
    kp = mem_k.reshape(depth, nb, nm, d)
    vp = mem_v.reshape(depth, nb, nm, d)
    ks, vs = cache_mem_k, cache_mem_v

    x = None
    conv_p, conv_s, pool_p, pool_s, chunk_v = [], [], [], [], []
    for i in range(depth):
        kind, j = i % 3, i // 3
        g = row(norm_mix[i])
        if kind == 0:
            srcs = (x_prompt.reshape(nb * seq, d), to_pm(x_sample), 0) if x is None else (x, x, npt)
            x, v_s = _mixer_a(*srcs, g, bf(a_w_in[j]), row(a_ln_g[j]), row(a_ln_b[j]), a_w_s[j], a_b_s[j],
                              bf(a_w_out[j]), **dims)
            chunk_v.append(from_pm(v_s, ls))
        elif kind == 1:
            x, tail, z_s = _mixer_b(x, g, bf(b_w_in[j]), b_conv[j], to_pm(state_conv[j]), bf(b_w_out[j]),
                                    npt=npt, tps=tps, bs=bs)
            conv_p.append(tail[:, 6:, :])
            conv_s.append(from_pm(z_s, 2))
        else:
            x, tail, h_s = _mixer_c(x, g, bf(c_w_in[j]), to_pm(state_pool[j]), bf(c_w_grp[j]),
                                    row(c_scale[j]), bf(c_w_out[j]), tps=tps, pos0=pos0, **dims)
            pool_p.append(tail[:, 1:, :])
            pool_s.append(jnp.concatenate([state_pool[j], from_pm(h_s, ls)], axis=1)[:, -POOL_CTX:, :])
        x = _xattn(x, row(norm_xa[i]), bf(xa_wq[i]), bf(xa_wo[i]), kp, vp, ks, vs, i, tps=tps, **dims)
        g = row(norm_ffn[i])
        if i % 2 == 0:
            x = _ffn(x, g, bf(f_w1[i // 2]), bf(f_w3[i // 2]), bf(f_w2[i // 2]))
        elif i < depth - 1:
            x = _moe_ffn(x, g, m_router[i // 2], m_w1, m_w3, m_w2, i // 2)
        else:
            x = _moe_ffn(x, g, m_router[i // 2], m_w1, m_w3, m_w2, i // 2, row(norm_final), npt)
    y_p, y_s = x if isinstance(x, (tuple, list)) else _final_norm(x, row(norm_final), npt=npt)

    return (y_p.reshape(nb, seq, d), from_pm(y_s, ls), mem_k5, mem_v5,
            jnp.stack(conv_p), jnp.stack(pool_p), jnp.stack(conv_s), jnp.stack(pool_s), jnp.stack(chunk_v))
```

```python
import functools

import jax
import jax.numpy as jnp
from jax import lax
from jax.experimental import pallas as pl
from jax.experimental.pallas import tpu as pltpu

F32 = jnp.float32
BF16 = jnp.bfloat16

EPS = 1e-6
TM = 512
CHUNK = 128
A_GROUPS = 8
POOL_WINDOWS = (2, 4, 8, 16)
POOL_CTX = max(POOL_WINDOWS) - 1
X_HEADS = 4
TOP_K = 2
PAST_LEN = 16384
LANES = 128
SUBLANES = 8
DC_TM = 1536
MOE_TM = 512
MOE_TF = 512
MOE_CHUNK = 1024
VMEM_LIMIT = 60 * 1024 * 1024


def _params(*sem):
    return pltpu.CompilerParams(dimension_semantics=sem, vmem_limit_bytes=VMEM_LIMIT)


def _const(shape):
    nd = len(shape)
    return pl.BlockSpec(shape, lambda *_: (0,) * nd, pipeline_mode=pl.Buffered(1))


def _rmsnorm(x, g):
    r = lax.rsqrt(jnp.mean(x * x, axis=-1, keepdims=True) + EPS)
    return (x * r) * g


def _bdot(a, w):
    return jnp.dot(a.astype(BF16), w, preferred_element_type=F32)


def _mem_kv_kernel(m_ref, g_ref, wk_ref, wv_ref, k_ref, v_ref, k5_ref, v5_ref):
    mn = _rmsnorm(m_ref[...], g_ref[...]).astype(BF16)
    nbt, nm, nh, hd = k5_ref.shape
    for w_ref, o_ref, o5_ref in ((wk_ref, k_ref, k5_ref), (wv_ref, v_ref, v5_ref)):
        y = jnp.dot(mn, w_ref[...], preferred_element_type=F32)
        o_ref[...] = y.astype(BF16)
        for b in range(nbt):
            for h in range(nh):
                o5_ref[b, :, h, :] = y[b * nm:(b + 1) * nm, h * hd:(h + 1) * hd]


def _mem_kv(mem, g, wk, wv, nm):
    rows, d = mem.shape
    depth = g.shape[0]
    hd = d // X_HEADS
    out = jax.ShapeDtypeStruct((depth, rows, d), BF16)
    out5 = jax.ShapeDtypeStruct((depth, rows // nm, nm, X_HEADS, hd), F32)
    wspec = pl.BlockSpec((None, d, d), lambda l, r: (l, 0, 0))
    ospec = pl.BlockSpec((None, TM, d), lambda l, r: (l, r, 0))
    o5spec = pl.BlockSpec((None, TM // nm, nm, X_HEADS, hd), lambda l, r: (l, r, 0, 0, 0))
    return pl.pallas_call(
        _mem_kv_kernel,
        grid=(depth, rows // TM),
        in_specs=[pl.BlockSpec((TM, d), lambda l, r: (r, 0)),
                  pl.BlockSpec((None, 1, d), lambda l, r: (l, 0, 0)),
                  wspec, wspec],
        out_specs=[ospec, ospec, o5spec, o5spec],
        out_shape=[out, out, out5, out5],
        compiler_params=_params("arbitrary", "arbitrary"),
        name="mem_kv",
    )(mem, g, wk, wv)


def _mixer_a_kernel(x_ref, xt_ref, g_ref, win_ref, lng_ref, lnb_ref, ws_ref, bs_ref, sc_ref, sb_ref, wout_ref,
                    o_ref, vs_ref, u_scr, v_scr, s_scr, *, npt, bs, ls):
    i = pl.program_id(0)
    d = x_ref.shape[1]
    x = jnp.where(i == npt, xt_ref[...], x_ref[...])
    z = jax.nn.gelu(_bdot(_rmsnorm(x, g_ref[...]), win_ref[...]))
    u_scr[...] = z[:, :d]
    v = z[:, d:]
    mu = jnp.mean(v, axis=-1, keepdims=True)
    var = jnp.mean(jnp.square(v - mu), axis=-1, keepdims=True)
    v_scr[...] = ((v - mu) * lax.rsqrt(var + EPS)) * lng_ref[...] + lnb_ref[...]
    gd = d // A_GROUPS

    @pl.when(i < npt)
    def _():
        rows = lax.broadcasted_iota(jnp.int32, (CHUNK, CHUNK), 0)
        cols = lax.broadcasted_iota(jnp.int32, (CHUNK, CHUNK), 1)
        for g in range(A_GROUPS):
            wg = jnp.where(cols <= rows, ws_ref[g], 0.0).astype(BF16)
            bg = bs_ref[g]
            for c in range(TM // CHUNK):
                vc = v_scr[c * CHUNK:(c + 1) * CHUNK, g * gd:(g + 1) * gd]
                s_scr[c * CHUNK:(c + 1) * CHUNK, g * gd:(g + 1) * gd] = (
                    jnp.dot(wg, vc.astype(BF16), preferred_element_type=F32) + bg)

    @pl.when(i == npt)
    def _():
        for t in range(ls):
            acc = jnp.broadcast_to(sb_ref[t:t + 1, :], (bs, d))
            for s in range(t + 1):
                acc = acc + sc_ref[t * ls + s:t * ls + s + 1, :] * v_scr[s * bs:(s + 1) * bs, :]
            s_scr[t * bs:(t + 1) * bs, :] = acc
        vs_ref[...] = v_scr[...]

    o_ref[...] = x + _bdot(u_scr[...] * s_scr[...], wout_ref[...])


def _mixer_a(x, x_tail, tail_blk, g, w_in, ln_g, ln_b, w_s, b_s, w_out, *, npt, bs, ls):
    d = x.shape[1]
    t = (npt + 1) * TM
    assert TM == bs * ls and TM % CHUNK == 0 and ls <= CHUNK
    gd = d // A_GROUPS
    sc = jnp.repeat(w_s[:, :ls, :ls].transpose(1, 2, 0).reshape(ls * ls, A_GROUPS), gd, axis=1)
    sb = jnp.repeat(b_s[:, :ls].T, gd, axis=1)
    tile = pl.BlockSpec((TM, d), lambda i: (i, 0))
    return pl.pallas_call(
        functools.partial(_mixer_a_kernel, npt=npt, bs=bs, ls=ls),
        grid=(npt + 1,),
        in_specs=[pl.BlockSpec((TM, d), lambda i: (jnp.minimum(i, npt - 1), 0)),
                  pl.BlockSpec((TM, d), lambda i: (tail_blk, 0), pipeline_mode=pl.Buffered(1)),
                  _const((1, d)), _const(w_in.shape), _const((1, d)), _const((1, d)),
                  _const(w_s.shape), _const((A_GROUPS, CHUNK, 1)), _const(sc.shape), _const(sb.shape),
                  _const(w_out.shape)],
        out_specs=[tile, pl.BlockSpec((TM, d), lambda i: (0, 0))],
        out_shape=[jax.ShapeDtypeStruct((t, d), F32), jax.ShapeDtypeStruct((TM, d), F32)],
        scratch_shapes=[pltpu.VMEM((TM, d), F32)] * 3,
        compiler_params=_params("arbitrary"),
        name="mixer_a",
    )(x, x_tail, g, w_in, ln_g, ln_b, w_s, b_s[:, :, None], sc, sb, w_out)


def _mixer_b_kernel(x_ref, g_ref, win_ref, cw_ref, pre_ref, wout_ref,
                    o_ref, tail_ref, zs_ref, conv_scr, carry_scr, *, npt, tps, bs):
    i = pl.program_id(0)
    d = x_ref.shape[1]
    x = x_ref[...]
    p = _bdot(_rmsnorm(x, g_ref[...]), win_ref[...])
    gate_b = p[:, :d]
    zc = p[:, d:2 * d] * p[:, 2 * d:]
    w0, w1, w2 = cw_ref[0:1, :], cw_ref[1:2, :], cw_ref[2:3, :]

    @pl.when(i == 0)
    def _():
        carry_scr[...] = jnp.zeros_like(carry_scr)

    @pl.when(i < npt)
    def _():
        keep = (i % tps) != 0
        c6 = jnp.where(keep, carry_scr[6:7, :], 0.0)
        c7 = jnp.where(keep, carry_scr[7:8, :], 0.0)
        rows = lax.broadcasted_iota(jnp.int32, (TM, 1), 0)
        sh1 = jnp.where(rows == 0, c7, pltpu.roll(zc, 1, 0))
        sh2 = jnp.where(rows == 0, c6, jnp.where(rows == 1, c7, pltpu.roll(zc, 2, 0)))
        conv_scr[...] = w0 * sh2 + w1 * sh1 + w2 * zc
        carry_scr[...] = zc[TM - 8:, :]
        tail_ref[...] = zc[TM - 8:, :]

    @pl.when(i == npt)
    def _():
        p0, p1 = pre_ref[0:bs, :], pre_ref[bs:2 * bs, :]
        sh1 = jnp.concatenate([p1, zc[:TM - bs]], axis=0)
        sh2 = jnp.concatenate([p0, p1, zc[:TM - 2 * bs]], axis=0)
        conv_scr[...] = w0 * sh2 + w1 * sh1 + w2 * zc
        zs_ref[...] = zc[TM - 2 * bs:, :]

    o_ref[...] = x + _bdot(gate_b * conv_scr[...], wout_ref[...])


def _mixer_b(x, g, w_in, conv_w, prefix, w_out, *, npt, tps, bs):
    t, d = x.shape
    nb = npt // tps
    tile = pl.BlockSpec((TM, d), lambda i: (i, 0))
    return pl.pallas_call(
        functools.partial(_mixer_b_kernel, npt=npt, tps=tps, bs=bs),
        grid=(npt + 1,),
        in_specs=[tile, _const((1, d)), _const(w_in.shape), _const(conv_w.shape), _const(prefix.shape),
                  _const(w_out.shape)],
        out_specs=[tile,
                   pl.BlockSpec((None, 8, d), lambda i: (jnp.minimum(i // tps, nb - 1), 0, 0)),
                   pl.BlockSpec((2 * bs, d), lambda i: (0, 0))],
        out_shape=[jax.ShapeDtypeStruct((t, d), F32), jax.ShapeDtypeStruct((nb, 8, d), F32),
                   jax.ShapeDtypeStruct((2 * bs, d), F32)],
        scratch_shapes=[pltpu.VMEM((TM, d), F32), pltpu.VMEM((8, d), F32)],
        compiler_params=_params("arbitrary"),
        name="mixer_b",
    )(x, g, w_in, conv_w, prefix, w_out)


def _mixer_c_kernel(x_ref, g_ref, win_ref, pre_ref, wgrp_ref, scale_ref, wout_ref,
                    o_ref, tail_ref, hs_ref, d_scr, hist_scr, *, npt, tps, bs, ls, pos0):
    i = pl.program_id(0)
    d = x_ref.shape[1]
    gd = d // len(POOL_WINDOWS)
    hp = POOL_CTX + 1
    x = x_ref[...]
    h = _bdot(_rmsnorm(x, g_ref[...]), win_ref[...])

    @pl.when(i == 0)
    def _():
        hist_scr[...] = jnp.zeros_like(hist_scr)

    @pl.when(i < npt)
    def _():
        keep = (i % tps) != 0
        ext = jnp.concatenate([jnp.where(keep, hist_scr[...], 0.0), h], axis=0)
        pos = (i % tps) * TM + lax.broadcasted_iota(jnp.int32, (TM, 1), 0)
        sums = ext
        width = 1
        for g, w in enumerate(POOL_WINDOWS):
            while width < w:
                sums = sums + pltpu.roll(sums, width, 0)
                width *= 2
            cnt = jnp.minimum(w, pos + 1).astype(F32)
            sl = slice(g * gd, (g + 1) * gd)
            d_scr[:, sl] = sums[hp:, sl] / cnt - h[:, sl]
        hist_scr[...] = h[TM - hp:, :]
        tail_ref[...] = h[TM - hp:, :]

    @pl.when(i == npt)
    def _():
        for g, w in enumerate(POOL_WINDOWS):
            sl = slice(g * gd, (g + 1) * gd)
            run = jnp.zeros((bs, gd), F32)
            tails = [run]
            for m in range(1, w):
                run = run + pre_ref[(POOL_CTX - m) * bs:(POOL_CTX - m + 1) * bs, sl]
                tails.append(run)
            for t in range(ls):
                n = min(w, t + 1)
                acc = tails[w - n]
                for j in range(n):
                    acc = acc + h[(t - j) * bs:(t - j + 1) * bs, sl]
                cnt = float(min(w, pos0 + t + 1))
                d_scr[t * bs:(t + 1) * bs, sl] = acc / cnt - h[t * bs:(t + 1) * bs, sl]
        hs_ref[...] = h

    m = jnp.concatenate(
        [_bdot(d_scr[:, g * gd:(g + 1) * gd], wgrp_ref[g]) for g in range(len(POOL_WINDOWS))], axis=1)
    o_ref[...] = x + _bdot(m * scale_ref[...], wout_ref[...])


def _mixer_c(x, g, w_in, prefix, w_grp, scale, w_out, *, npt, tps, bs, ls, pos0):
    t, d = x.shape
    nb = npt // tps
    hp = POOL_CTX + 1
    tile = pl.BlockSpec((TM, d), lambda i: (i, 0))
    return pl.pallas_call(
        functools.partial(_mixer_c_kernel, npt=npt, tps=tps, bs=bs, ls=ls, pos0=pos0),
        grid=(npt + 1,),
        in_specs=[tile, _const((1, d)), _const(w_in.shape), _const(prefix.shape), _const(w_grp.shape),
                  _const((1, d)), _const(w_out.shape)],
        out_specs=[tile,
                   pl.BlockSpec((None, hp, d), lambda i: (jnp.minimum(i // tps, nb - 1), 0, 0)),
                   pl.BlockSpec((TM, d), lambda i: (0, 0))],
        out_shape=[jax.ShapeDtypeStruct((t, d), F32), jax.ShapeDtypeStruct((nb, hp, d), F32),
                   jax.ShapeDtypeStruct((TM, d), F32)],
        scratch_shapes=[pltpu.VMEM((TM, d), F32), pltpu.VMEM((hp, d), F32)],
        compiler_params=_params("arbitrary"),
        name="mixer_c",
    )(x, g, w_in, prefix, w_grp, scale, w_out)


def _softmax_rows(s):
    e = jnp.exp(s - jnp.max(s, axis=-1, keepdims=True))
    return e / jnp.sum(e, axis=-1, keepdims=True)


def _xattn_kernel(x_ref, xs_ref, g_ref, wq_ref, wo_ref, kp_ref, vp_ref, ks_ref, vs_ref,
                  o_ref, q_scr, o_scr, *, npt, bs, ls, spb):
    i = pl.program_id(0)
    d = x_ref.shape[1]
    hd = d // X_HEADS
    qscale = float(hd) ** -0.5
    nt = (((1,), (1,)), ((), ()))

    @pl.when(i == 0)
    def _():
        q_scr[...] = _bdot(_rmsnorm(xs_ref[...], g_ref[...]), wq_ref[...]) * qscale

    @pl.when(i < npt)
    def _():
        x = x_ref[...]
        q = (_bdot(_rmsnorm(x, g_ref[...]), wq_ref[...]) * qscale).astype(BF16)
        k = kp_ref[...]
        v = vp_ref[...]
        heads = []
        for h in range(X_HEADS):
            sl = slice(h * hd, (h + 1) * hd)
            s = lax.dot_general(q[:, sl], k[:, sl], nt, preferred_element_type=F32)
            heads.append(_bdot(_softmax_rows(s), v[:, sl]))
        o_ref[...] = x + _bdot(jnp.concatenate(heads, axis=1), wo_ref[...])

        nm = ks_ref.shape[1]
        rows = lax.broadcasted_iota(jnp.int32, (X_HEADS * ls, nm * X_HEADS), 0)
        cols = lax.broadcasted_iota(jnp.int32, (X_HEADS * ls, nm * X_HEADS), 1)
        own = (rows // ls) == (cols % X_HEADS)
        for bb in range(spb):
            b = i * spb + bb
            qb = jnp.concatenate([q_scr[pl.ds(t * bs + b, 1), :] for t in range(ls)], axis=0)
            qh = jnp.concatenate([qb[:, h * hd:(h + 1) * hd] for h in range(X_HEADS)], axis=0)
            kr = ks_ref[bb].reshape(nm * X_HEADS, hd).astype(BF16)
            vr = vs_ref[bb].reshape(nm * X_HEADS, hd).astype(BF16)
            s = lax.dot_general(qh.astype(BF16), kr, nt, preferred_element_type=F32)
            of = _bdot(_softmax_rows(jnp.where(own, s, -jnp.inf)), vr)
            ob = jnp.concatenate([of[h * ls:(h + 1) * ls] for h in range(X_HEADS)], axis=1)
            for t in range(ls):
                o_scr[pl.ds(t * bs + b, 1), :] = ob[t:t + 1]

    @pl.when(i == npt)
    def _():
        o_ref[...] = x_ref[...] + _bdot(o_scr[...], wo_ref[...])


def _xattn(x, g, wq, wo, kp, vp, ks, vs, layer, *, npt, tps, bs, ls):
    t, d = x.shape
    nb, nm = kp.shape[1], kp.shape[2]
    assert bs % npt == 0
    spb = bs // npt
    tile = pl.BlockSpec((TM, d), lambda i: (i, 0))
    pmem = pl.BlockSpec((None, None, nm, d), lambda i: (layer, jnp.minimum(i // tps, nb - 1), 0, 0))
    smem = pl.BlockSpec((None, spb, nm, X_HEADS, d // X_HEADS),
                        lambda i: (layer, jnp.minimum(i, npt - 1), 0, 0, 0))
    return pl.pallas_call(
        functools.partial(_xattn_kernel, npt=npt, bs=bs, ls=ls, spb=spb),
        grid=(npt + 1,),
        in_specs=[tile, pl.BlockSpec((TM, d), lambda i: (npt, 0), pipeline_mode=pl.Buffered(1)),
                  _const((1, d)), _const(wq.shape), _const(wo.shape), pmem, pmem, smem, smem],
        out_specs=tile,
        out_shape=jax.ShapeDtypeStruct((t, d), F32),
        scratch_shapes=[pltpu.VMEM((TM, d), F32), pltpu.VMEM((TM, d), F32)],
        compiler_params=_params("arbitrary"),
        name="xattn",
    )(x, x, g, wq, wo, kp, vp, ks, vs)


def _ffn_kernel(x_ref, g_ref, w1_ref, w3_ref, w2_ref, o_ref, *, chunks):
    x = x_ref[...]
    xn = _rmsnorm(x, g_ref[...]).astype(BF16)
    acc = x
    for lo, hi in chunks:
        h1 = jnp.dot(xn, w1_ref[:, lo:hi], preferred_element_type=F32)
        h3 = jnp.dot(xn, w3_ref[:, lo:hi], preferred_element_type=F32)
        acc = acc + _bdot(jax.nn.silu(h1) * h3, w2_ref[lo:hi, :])
    o_ref[...] = acc


def _ffn(x, g, w1, w3, w2):
    t, d = x.shape
    dff = w1.shape[1]
    step = 1024
    chunks = tuple((lo, min(lo + step, dff)) for lo in range(0, dff, step))
    tile = pl.BlockSpec((TM, d), lambda i: (i, 0))
    return pl.pallas_call(
        functools.partial(_ffn_kernel, chunks=chunks),
        grid=(t // TM,),
        in_specs=[tile, _const((1, d)), _const(w1.shape), _const(w3.shape), _const(w2.shape)],
        out_specs=tile,
        out_shape=jax.ShapeDtypeStruct((t, d), F32),
        compiler_params=_params("arbitrary"),
        name="ffn",
    )(x, g, w1, w3, w2)


def _router_kernel(x_ref, g_ref, r_ref, tri_ref, meta_ref, cnt_ref, carry_scr, *, n_experts):
    i = pl.program_id(0)

    @pl.when(i == 0)
    def _():
        carry_scr[...] = jnp.zeros_like(carry_scr)

    xn = _rmsnorm(x_ref[...], g_ref[...])
    xh = xn.astype(BF16)
    xl = (xn - xh.astype(F32)).astype(BF16)
    r = r_ref[...]
    rh = r.astype(BF16)
    rl = (r - rh.astype(F32)).astype(BF16)
    logits = (jnp.dot(xh, rh, preferred_element_type=F32) + jnp.dot(xl, rh, preferred_element_type=F32)
              + jnp.dot(xh, rl, preferred_element_type=F32))
    lane = lax.broadcasted_iota(jnp.int32, logits.shape, 1).astype(F32)
    neg = jnp.float32(-jnp.inf)
    logits = jnp.where(lane < n_experts, logits, neg)
    v1 = jnp.max(logits, axis=-1, keepdims=True)
    i1 = jnp.min(jnp.where(logits == v1, lane, float(LANES)), axis=-1, keepdims=True)
    rest = jnp.where(lane == i1, neg, logits)
    v2 = jnp.max(rest, axis=-1, keepdims=True)
    i2 = jnp.min(jnp.where(rest == v2, lane, float(LANES)), axis=-1, keepdims=True)
    e2 = jnp.exp(v2 - v1)
    den = 1.0 + e2
    hit = jnp.where(lane == i1, 1.0, jnp.where(lane == i2, 1.0, 0.0))
    before = jnp.dot(tri_ref[...], hit.astype(BF16), preferred_element_type=F32) + carry_scr[0:1, :]
    r1 = jnp.sum(jnp.where(lane == i1, before, 0.0), axis=-1, keepdims=True)
    r2 = jnp.sum(jnp.where(lane == i2, before, 0.0), axis=-1, keepdims=True)
    cols = (i1, i2, r1, r2, 1.0 / den, e2 / den)
    meta = jnp.zeros_like(logits)
    for c, val in enumerate(cols):
        meta = jnp.where(lane == c, val, meta)
    meta_ref[...] = meta
    carry_scr[...] = carry_scr[...] + jnp.sum(hit, axis=0, keepdims=True)
    cnt_ref[...] = carry_scr[...]


def _router(x, g, router):
    t, d = x.shape
    ne = router.shape[1]
    assert TOP_K == 2 and ne <= LANES
    rpad = jnp.pad(router, ((0, 0), (0, LANES - ne)))
    tri = jnp.tril(jnp.ones((TM, TM), BF16), -1)
    return pl.pallas_call(
        functools.partial(_router_kernel, n_experts=ne),
        grid=(t // TM,),
        in_specs=[pl.BlockSpec((TM, d), lambda i: (i, 0)), _const((1, d)), _const(rpad.shape),
                  _const(tri.shape)],
        out_specs=[pl.BlockSpec((TM, LANES), lambda i: (i, 0)), pl.BlockSpec((8, LANES), lambda i: (0, 0))],
        out_shape=[jax.ShapeDtypeStruct((t, LANES), F32), jax.ShapeDtypeStruct((8, LANES), F32)],
        scratch_shapes=[pltpu.VMEM((8, LANES), F32)],
        compiler_params=_params("arbitrary"),
        name="router",
    )(x, g, rpad, tri)


def _dispatch_kernel(dst_ref, fill_ref, nt_ref, x_ref, g_ref, xs_ref, z_scr, xn_ref, zsem, sem, *, min_tiles):
    i = pl.program_id(0)
    rows = x_ref.shape[0]
    xn_ref[...] = _rmsnorm(x_ref[...], g_ref[...])
    last = xs_ref.shape[0] // MOE_TM - 1

    @pl.when(i == 0)
    def _():
        z_scr[...] = jnp.zeros_like(z_scr)
        fills = [pltpu.make_async_copy(
            z_scr, xs_ref.at[pl.ds(pl.multiple_of(fill_ref[e], SUBLANES), MOE_TM)], zsem)
            for e in range(fill_ref.shape[0])]
        for c in fills:
            c.start()
        for c in fills:
            c.wait()
        for j in range(last + 1 - min_tiles):
            tile = nt_ref[0] + j

            @pl.when(tile <= last)
            def _():
                c = pltpu.make_async_copy(
                    z_scr, xs_ref.at[pl.ds(pl.multiple_of(tile * MOE_TM, MOE_TM), MOE_TM)], zsem)
                c.start()
                c.wait()

    def body(r, carry):
        src = xn_ref.at[pl.ds(r, 1)]
        for k in range(TOP_K):
            pltpu.make_async_copy(src, xs_ref.at[pl.ds(dst_ref[TOP_K * r + k], 1)], sem).start(priority=k)
        return carry

    lax.fori_loop(0, rows, body, 0, unroll=8)
    for _ in range(TOP_K):
        pltpu.make_async_copy(xn_ref, xs_ref.at[pl.ds(0, rows)], sem).wait()


def _dispatch(dst, fill, nt, x, g, n_slot_tiles):
    t, d = x.shape
    return pl.pallas_call(
        functools.partial(_dispatch_kernel, min_tiles=(t * TOP_K) // MOE_TM),
        grid=(t // DC_TM,),
        in_specs=[pl.BlockSpec((TOP_K * DC_TM,), lambda i: (i,), memory_space=pltpu.SMEM),
                  pl.BlockSpec(memory_space=pltpu.SMEM), pl.BlockSpec(memory_space=pltpu.SMEM),
                  pl.BlockSpec((DC_TM, d), lambda i: (i, 0)), _const((1, d))],
        out_specs=pl.BlockSpec(memory_space=pl.ANY),
        out_shape=jax.ShapeDtypeStruct(((n_slot_tiles + 1) * MOE_TM, d), F32),
        scratch_shapes=[pltpu.VMEM((MOE_TM, d), F32), pltpu.VMEM((DC_TM, d), F32),
                        pltpu.SemaphoreType.DMA(()), pltpu.SemaphoreType.DMA(())],
        compiler_params=_params("arbitrary"),
        name="dispatch",
    )(dst, fill, nt, x, g)


def _moe_kernel(te_ref, tf_ref, nt_ref, xs_ref, w1_hbm, w3_hbm, w2_hbm, ys_ref,
                s1, s3, s2, w1_scr, w3_scr, w2_scr, sem, *, layer, n_tiles):
    i = pl.program_id(0)
    dff = w1_scr.shape[1]
    nf = dff // MOE_TF
    valid = i < nt_ref[0]
    first = valid & (tf_ref[i] == 1)
    nxt = jnp.minimum(i + 1, n_tiles - 1)
    next_first = (i + 1 < nt_ref[0]) & (tf_ref[nxt] == 1)

    def chunk(e, f, slot):
        cols = pl.ds(f * MOE_TF, MOE_TF)
        return (pltpu.make_async_copy(w1_hbm.at[layer, e, :, cols], s1.at[slot], sem.at[slot]),
                pltpu.make_async_copy(w3_hbm.at[layer, e, :, cols], s3.at[slot], sem.at[slot]),
                pltpu.make_async_copy(w2_hbm.at[layer, e, cols, :], s2.at[slot], sem.at[slot]))

    def start(e, f, slot):
        for c in chunk(e, f, slot):
            c.start()

    def prefetch(e):
        start(e, 0, 0)
        start(e, 1, 1)

    def swiglu(xb, lo, hi):
        h1 = jnp.dot(xb, w1_scr[:, lo:hi], preferred_element_type=F32)
        h3 = jnp.dot(xb, w3_scr[:, lo:hi], preferred_element_type=F32)
        return _bdot(jax.nn.silu(h1) * h3, w2_scr[lo:hi, :])

    @pl.when(first & (i == 0))
    def _():
        prefetch(te_ref[i])

    @pl.when(first)
    def _():
        xb = xs_ref[...].astype(BF16)
        acc = None
        for f in range(nf):
            slot = f % 2
            for c in chunk(te_ref[i], f, slot):
                c.wait()
            lo, hi = f * MOE_TF, (f + 1) * MOE_TF
            w1_scr[:, lo:hi] = s1[slot].astype(BF16)
            w3_scr[:, lo:hi] = s3[slot].astype(BF16)
            w2_scr[lo:hi, :] = s2[slot].astype(BF16)
            if f + 2 < nf:
                start(te_ref[i], f + 2, slot)
            y = swiglu(xb, lo, hi)
            acc = y if acc is None else acc + y
        ys_ref[...] = acc

        @pl.when(next_first)
        def _():
            prefetch(te_ref[nxt])

    @pl.when(valid & jnp.logical_not(first))
    def _():
        @pl.when(next_first)
        def _():
            prefetch(te_ref[nxt])

        xb = xs_ref[...].astype(BF16)
        acc = None
        for lo in range(0, dff, MOE_CHUNK):
            y = swiglu(xb, lo, min(lo + MOE_CHUNK, dff))
            acc = y if acc is None else acc + y
        ys_ref[...] = acc

    @pl.when(jnp.logical_not(valid))
    def _():
        ys_ref[...] = jnp.zeros_like(ys_ref)


def _moe(te, tf, nt, xs, w1, w3, w2, layer, n_tiles):
    d = xs.shape[1]
    dff = w1.shape[3]
    tile = pl.BlockSpec((MOE_TM, d), lambda i, te, tf, nt: (i, 0))
    used = pl.BlockSpec((MOE_TM, d), lambda i, te, tf, nt: (jnp.minimum(i, nt[0] - 1), 0))
    hbm = pl.BlockSpec(memory_space=pl.ANY)
    return pl.pallas_call(
        functools.partial(_moe_kernel, layer=layer, n_tiles=n_tiles),
        grid_spec=pltpu.PrefetchScalarGridSpec(
            num_scalar_prefetch=3,
            grid=(n_tiles,),
            in_specs=[used, hbm, hbm, hbm],
            out_specs=tile,
            scratch_shapes=[pltpu.VMEM((2, d, MOE_TF), F32), pltpu.VMEM((2, d, MOE_TF), F32),
                            pltpu.VMEM((2, MOE_TF, d), F32),
                            pltpu.VMEM((d, dff), BF16), pltpu.VMEM((d, dff), BF16),
                            pltpu.VMEM((dff, d), BF16), pltpu.SemaphoreType.DMA((2,))]),
        out_shape=jax.ShapeDtypeStruct((n_tiles * MOE_TM, d), F32),
        compiler_params=_params("arbitrary"),
        name="moe",
    )(te, tf, nt, xs, w1, w3, w2)


def _combined(dst_ref, x_ref, meta_ref, ys_ref, a_scr, b_scr, sem):
    def body(r, carry):
        pltpu.make_async_copy(ys_ref.at[pl.ds(dst_ref[TOP_K * r], 1)], a_scr.at[pl.ds(r, 1)],
                              sem.at[0]).start(priority=0)
        pltpu.make_async_copy(ys_ref.at[pl.ds(dst_ref[TOP_K * r + 1], 1)], b_scr.at[pl.ds(r, 1)],
                              sem.at[1]).start(priority=1)
        return carry

    rows = x_ref.shape[0]
    lax.fori_loop(0, rows, body, 0, unroll=8)
    pltpu.make_async_copy(ys_ref.at[pl.ds(0, rows)], a_scr, sem.at[0]).wait()
    pltpu.make_async_copy(ys_ref.at[pl.ds(0, rows)], b_scr, sem.at[1]).wait()
    meta = meta_ref[...]
    return x_ref[...] + (meta[:, 4:5] * a_scr[...] + meta[:, 5:6] * b_scr[...])


def _combine_kernel(dst_ref, x_ref, meta_ref, ys_ref, o_ref, a_scr, b_scr, sem):
    o_ref[...] = _combined(dst_ref, x_ref, meta_ref, ys_ref, a_scr, b_scr, sem)


def _combine_final_kernel(dst_ref, x_ref, meta_ref, ys_ref, g_ref, op_ref, os_ref, a_scr, b_scr, sem, *, npt):
    i = pl.program_id(0)
    y = _rmsnorm(_combined(dst_ref, x_ref, meta_ref, ys_ref, a_scr, b_scr, sem), g_ref[...])

    @pl.when(i < npt)
    def _():
        op_ref[...] = y

    @pl.when(i == npt)
    def _():
        os_ref[...] = y


def _combine(dst, x, meta, ys):
    t, d = x.shape
    tile = pl.BlockSpec((DC_TM, d), lambda i: (i, 0))
    return pl.pallas_call(
        _combine_kernel,
        grid=(t // DC_TM,),
        in_specs=[pl.BlockSpec((TOP_K * DC_TM,), lambda i: (i,), memory_space=pltpu.SMEM), tile,
                  pl.BlockSpec((DC_TM, LANES), lambda i: (i, 0)), pl.BlockSpec(memory_space=pl.ANY)],
        out_specs=tile,
        out_shape=jax.ShapeDtypeStruct((t, d), F32),
        scratch_shapes=[pltpu.VMEM((DC_TM, d), F32), pltpu.VMEM((DC_TM, d), F32),
                        pltpu.SemaphoreType.DMA((2,))],
        compiler_params=_params("arbitrary"),
        name="combine",
    )(dst, x, meta, ys)


def _combine_final(dst, x, meta, ys, g, npt):
    t, d = x.shape
    return pl.pallas_call(
        functools.partial(_combine_final_kernel, npt=npt),
        grid=(npt + 1,),
        in_specs=[pl.BlockSpec((TOP_K * TM,), lambda i: (i,), memory_space=pltpu.SMEM),
                  pl.BlockSpec((TM, d), lambda i: (i, 0)), pl.BlockSpec((TM, LANES), lambda i: (i, 0)),
                  pl.BlockSpec(memory_space=pl.ANY), _const((1, d))],
        out_specs=[pl.BlockSpec((TM, d), lambda i: (jnp.minimum(i, npt - 1), 0)),
                   pl.BlockSpec((TM, d), lambda i: (0, 0))],
        out_shape=[jax.ShapeDtypeStruct((npt * TM, d), F32), jax.ShapeDtypeStruct((TM, d), F32)],
        scratch_shapes=[pltpu.VMEM((TM, d), F32), pltpu.VMEM((TM, d), F32), pltpu.SemaphoreType.DMA((2,))],
        compiler_params=_params("arbitrary"),
        name="combine_final",
    )(dst, x, meta, ys, g)


def _moe_ffn(x, g, router, w1, w3, w2, layer, final_g=None, npt=None):
    t, d = x.shape
    ne = router.shape[1]
    assert t % DC_TM == 0 and (TOP_K * DC_TM) % 1024 == 0
    meta, cnt = _router(x, g, router)
    n = cnt[0, :ne].astype(jnp.int32)
    tiles = (n + MOE_TM - 1) // MOE_TM
    ends = jnp.cumsum(tiles)
    n_tiles = (t * TOP_K) // MOE_TM + ne
    off = (ends - tiles) * MOE_TM
    e = meta[:, :TOP_K].astype(jnp.int32)
    dst = (jnp.take(off, e) + meta[:, TOP_K:2 * TOP_K].astype(jnp.int32)).reshape(-1)
    tid = jnp.minimum(jnp.arange(n_tiles, dtype=jnp.int32), ends[-1] - 1)
    te = jnp.sum((tid[:, None] >= ends[None, :]).astype(jnp.int32), axis=1)
    first = (tid == jnp.take(ends - tiles, te)) & (jnp.arange(n_tiles) < ends[-1])
    fill = (off + n) // SUBLANES * SUBLANES
    nt = ends[-1:].astype(jnp.int32)
    xs = _dispatch(dst, fill, nt, x, g, n_tiles)
    ys = _moe(te, first.astype(jnp.int32), nt, xs, w1, w3, w2, layer, n_tiles)
    if final_g is not None:
        return _combine_final(dst, x, meta, ys, final_g, npt)
    return _combine(dst, x, meta, ys)


def _final_kernel(x_ref, g_ref, op_ref, os_ref, *, npt):
    i = pl.program_id(0)
    y = _rmsnorm(x_ref[...], g_ref[...])

    @pl.when(i < npt)
    def _():
        op_ref[...] = y

    @pl.when(i == npt)
    def _():
        os_ref[...] = y


def _final_norm(x, g, *, npt):
    t, d = x.shape
    return pl.pallas_call(
        functools.partial(_final_kernel, npt=npt),
        grid=(npt + 1,),
        in_specs=[pl.BlockSpec((TM, d), lambda i: (i, 0)), _const((1, d))],
        out_specs=[pl.BlockSpec((TM, d), lambda i: (jnp.minimum(i, npt - 1), 0)),
                   pl.BlockSpec((TM, d), lambda i: (0, 0))],
        out_shape=[jax.ShapeDtypeStruct((npt * TM, d), F32), jax.ShapeDtypeStruct((TM, d), F32)],
        compiler_params=_params("arbitrary"),
        name="final_norm",
    )(x, g)


def kernel(x_prompt, x_sample, cache_mem_k, cache_mem_v, state_conv, state_pool, mem_prompt, norm_mix, norm_xa, norm_ffn, norm_mem, norm_final, a_w_in, a_ln_g, a_ln_b, a_w_s, a_b_s, a_w_out, b_w_in, b_conv, b_w_out, c_w_in, c_w_grp, c_scale, c_w_out, xa_wq, xa_wk, xa_wv, xa_wo, f_w1, f_w3, f_w2, m_router, m_w1, m_w3, m_w2):
    nb, seq, d = x_prompt.shape
    bs, ls, _ = x_sample.shape
    depth = norm_mix.shape[0]
    nm = mem_prompt.shape[1]
    assert seq % TM == 0 and TM == bs * ls and (nb * nm) % TM == 0
    tps = seq // TM
    npt = nb * tps
    pos0 = PAST_LEN
    dims = dict(npt=npt, bs=bs, ls=ls)
    bf = lambda w: w.astype(BF16)
    row = lambda v: v.reshape(1, d)
    to_pm = lambda a: a.transpose(1, 0, 2).reshape(-1, d)
    from_pm = lambda a, r: a.reshape(r, bs, d).transpose(1, 0, 2)

    mem_k, mem_v, mem_k5, mem_v5 = _mem_kv(mem_prompt.reshape(nb * nm, d), norm_mem[:, None, :],
                                           bf(xa_wk), bf(xa_wv), nm)
    kp = mem_k.reshape(depth, nb, nm, d)
    vp = mem_v.reshape(depth, nb, nm, d)
    ks, vs = cache_mem_k, cache_mem_v

    x = None
    conv_p, conv_s, pool_p, pool_s, chunk_v = [], [], [], [], []
    for i in range(depth):
        kind, j = i % 3, i // 3
        g = row(norm_mix[i])
        if kind == 0:
            srcs = (x_prompt.reshape(nb * seq, d), to_pm(x_sample), 0) if x is None else (x, x, npt)
            x, v_s = _mixer_a(*srcs, g, bf(a_w_in[j]), row(a_ln_g[j]), row(a_ln_b[j]), a_w_s[j], a_b_s[j],
                              bf(a_w_out[j]), **dims)
            chunk_v.append(from_pm(v_s, ls))
        elif kind == 1:
            x, tail, z_s = _mixer_b(x, g, bf(b_w_in[j]), b_conv[j], to_pm(state_conv[j]), bf(b_w_out[j]),
                                    npt=npt, tps=tps, bs=bs)
            conv_p.append(tail[:, 6:, :])
            conv_s.append(from_pm(z_s, 2))
        else:
            x, tail, h_s = _mixer_c(x, g, bf(c_w_in[j]), to_pm(state_pool[j]), bf(c_w_grp[j]),
                                    row(c_scale[j]), bf(c_w_out[j]), tps=tps, pos0=pos0, **dims)
            pool_p.append(tail[:, 1:, :])
            pool_s.append(jnp.concatenate([state_pool[j], from_pm(h_s, ls)], axis=1)[:, -POOL_CTX:, :])
        x = _xattn(x, row(norm_xa[i]), bf(xa_wq[i]), bf(xa_wo[i]), kp, vp, ks, vs, i, tps=tps, **dims)
        g = row(norm_ffn[i])
        if i % 2 == 0:
            x = _ffn(x, g, bf(f_w1[i // 2]), bf(f_w3[i // 2]), bf(f_w2[i // 2]))
        elif i < depth - 1:
            x = _moe_ffn(x, g, m_router[i // 2], m_w1, m_w3, m_w2, i // 2)
        else:
            x = _moe_ffn(x, g, m_router[i // 2], m_w1, m_w3, m_w2, i // 2, row(norm_final), npt)
    y_p, y_s = x if isinstance(x, (tuple, list)) else _final_norm(x, row(norm_final), npt=npt)

    return (y_p.reshape(nb, seq, d), from_pm(y_s, ls), mem_k5, mem_v5,
            jnp.stack(conv_p), jnp.stack(pool_p), jnp.stack(conv_s), jnp.stack(pool_s), jnp.stack(chunk_v))
```
